```python
import math
import jax, jax.numpy as jnp
from jax import lax
import numpy as np

D_MODEL = 2048
BATCH = 1
SEQ = 8192
DEPTH = 1
DEC_BATCH = 32
DEC_SEQ = 4
PAST_LEN = 16384
PAGE_SIZE = 128

A_GROUPS = 3
A_WINDOWS = (128, 512, 2048)
A_DILATIONS = (1, 4, 16)
A_HEADS = 8
A_HEAD_DIM = 128
A_WIDTH = A_HEADS * A_HEAD_DIM
ROT_DIM = A_HEAD_DIM // 4
ROPE_THETA = 500000.0
A_QKV_COLS = 3 * A_GROUPS * A_WIDTH

GLA_HEADS = 4
GLA_DK = 256
GLA_DV = 512
GLA_LR = 16
GLA_TAU = 16.0
GLA_CHUNK = 64
GLA_QK = GLA_HEADS * GLA_DK
GLA_V = GLA_HEADS * GLA_DV

MOE_GROUPS = 4
MOE_PER_GROUP = 8
N_EXPERTS = MOE_GROUPS * MOE_PER_GROUP
MOE_TOPK = 2
D_EXPERT = 1024
MOE_BLOCK = 128

PLE_DIM = 256
IN_COLS = A_QKV_COLS + 2 * GLA_QK + 2 * GLA_V + GLA_LR + 2 * D_MODEL
LN_EPS = 1e-5

kernel_name = "hybrid_dilated_gla_hmoe_step"


def layer_norm(x, g, b):
    xf = x.astype(jnp.float32)
    mu = jnp.mean(xf, -1, keepdims=True)
    var = jnp.mean(jnp.square(xf - mu), -1, keepdims=True)
    y = (xf - mu) * lax.rsqrt(var + LN_EPS) * g.astype(jnp.float32) + b.astype(jnp.float32)
    return y.astype(x.dtype)


def rope_partial(x, pos):
    half = ROT_DIM // 2
    inv = ROPE_THETA ** (-jnp.arange(half, dtype=jnp.float32) * (2.0 / ROT_DIM))
    ang = pos.astype(jnp.float32)[:, None] * inv[None, :]
    cos = jnp.cos(ang)[:, None, :]
    sin = jnp.sin(ang)[:, None, :]
    xf = x.astype(jnp.float32)
    x1, x2 = xf[..., :half], xf[..., half:ROT_DIM]
    out = jnp.concatenate([x1 * cos - x2 * sin, x2 * cos + x1 * sin, xf[..., ROT_DIM:]], -1)
    return out.astype(x.dtype)


def band_attention(q, k, v, reach):
    N, n, H, E = q.shape
    blk = reach
    nb = -(-n // blk)
    pad_end = nb * blk - n
    qp = jnp.pad(q, ((0, 0), (0, pad_end), (0, 0), (0, 0))).reshape(N, nb, blk, H, E)

    def windows(t):
        tp = jnp.pad(t, ((0, 0), (blk, pad_end), (0, 0), (0, 0))).reshape(N, nb + 1, blk, H, E)
        return jnp.concatenate([tp[:, :-1], tp[:, 1:]], axis=2)

    kw, vw = windows(k), windows(v)
    s = jnp.einsum('nbqhe,nbkhe->nbhqk', qp, kw) * (E ** -0.5)
    dist = jnp.arange(blk)[:, None] + blk - jnp.arange(2 * blk)[None, :]
    kpos = jnp.arange(nb)[:, None] * blk - blk + jnp.arange(2 * blk)[None, :]
    mask = ((dist >= 0) & (dist <= reach))[None] & (kpos >= 0)[:, None, :]
    s = jnp.where(mask[None, :, None], s, -jnp.inf)
    m = jnp.max(s, axis=-1, keepdims=True)
    p = jnp.exp(s - m)
    den = jnp.sum(p, axis=-1)
    o = jnp.einsum('nbhqk,nbkhe->nbqhe', p, vw) / jnp.transpose(den, (0, 1, 3, 2))[..., None]
    lse = jnp.transpose(m[..., 0] + jnp.log(den), (0, 1, 3, 2))
    return o.reshape(N, nb * blk, H, E)[:, :n], lse.reshape(N, nb * blk, H)[:, :n]


def dilated_attention_prompt(q, k, v, d, reach):
    B, T, H, E = q.shape
    n = T // d

    def to_sub(t):
        return t.reshape(B, n, d, H, E).transpose(0, 2, 1, 3, 4).reshape(B * d, n, H, E)

    o, lse = band_attention(to_sub(q), to_sub(k), to_sub(v), reach)
    o = o.reshape(B, d, n, H, E).transpose(0, 2, 1, 3, 4).reshape(B, T, H, E)
    lse = lse.reshape(B, d, n, H).transpose(0, 2, 1, 3).reshape(B, T, H)
    return o, lse


def dilated_attention_sample(q, k, v, kv_buf, d, reach, window):
    B, S, H, E = q.shape
    L = kv_buf.shape[1]
    kv_all = jnp.concatenate([kv_buf, jnp.stack([k, v], axis=2).astype(kv_buf.dtype)], axis=1)
    idx = L + jnp.arange(S)[:, None] - d * jnp.arange(reach + 1)[None, :]
    valid = idx >= 0
    idx_c = jnp.clip(idx, 0)
    kg = kv_all[:, idx_c, 0].astype(jnp.float32)
    vg = kv_all[:, idx_c, 1].astype(jnp.float32)
    s = jnp.einsum('bshe,bsjhe->bshj', q, kg) * (E ** -0.5)
    s = jnp.where(valid[None, :, None, :], s, -jnp.inf)
    m = jnp.max(s, axis=-1, keepdims=True)
    p = jnp.exp(s - m)
    den = jnp.sum(p, axis=-1)
    o = jnp.einsum('bshj,bsjhe->bshe', p, vg) / den[..., None]
    lse = m[..., 0] + jnp.log(den)
    new_len = min(window, L + S)
    return o, lse, kv_all[:, L + S - new_len:]


def gla_chunked(q, k, v, g, s0):
    B, T, H, K = q.shape
    C = GLA_CHUNK if T % GLA_CHUNK == 0 else T
    nc = T // C

    def chunks(t):
        return t.reshape(B, nc, C, H, t.shape[-1]).transpose(1, 0, 3, 2, 4)

    qc, kc, vc, gc = chunks(q), chunks(k), chunks(v), chunks(g)
    b = jnp.cumsum(gc, axis=3)
    b_last = b[:, :, :, -1:]
    q_dec = qc * jnp.exp(b)
    k_inv = kc * jnp.exp(-b)
    k_dec = kc * jnp.exp(b_last - b)
    causal = jnp.tril(jnp.ones((C, C), dtype=bool))
    att = jnp.where(causal, jnp.einsum('nbhck,nbhsk->nbhcs', q_dec, k_inv), 0.0)
    o_intra = jnp.einsum('nbhcs,nbhsv->nbhcv', att, vc)
    decay = jnp.exp(b_last[:, :, :, 0])

    def step(S, inp):
        qd, kd, vv, dc = inp
        o = jnp.einsum('bhck,bhkv->bhcv', qd, S)
        S = dc[..., None] * S + jnp.einsum('bhck,bhcv->bhkv', kd, vv)
        return S, o

    s_fin, o_inter = lax.scan(step, s0, (q_dec, k_dec, vc, decay))
    o = (o_intra + o_inter).transpose(1, 0, 3, 2, 4).reshape(B, T, H, v.shape[-1])
    return o, s_fin


def grouped_experts(xf, e_idx, e_w, w_gate, w_up, w_down):
    N, D = xf.shape
    A = e_idx.shape[0] * e_idx.shape[1]
    E = w_gate.shape[0]
    blk = MOE_BLOCK if A >= MOE_BLOCK * E else 8
    n_blocks = -(-A // blk) + E
    M = n_blocks * blk
    flat_e = e_idx.reshape(-1)
    flat_t = jnp.repeat(jnp.arange(N, dtype=jnp.int32), MOE_TOPK)
    flat_w = e_w.reshape(-1)
    order = jnp.argsort(flat_e)
    se = flat_e[order]
    counts = jnp.zeros((E,), jnp.int32).at[flat_e].add(1)
    starts = jnp.cumsum(counts) - counts
    pcounts = (counts + blk - 1) // blk * blk
    pends = jnp.cumsum(pcounts)
    pstarts = pends - pcounts
    dest = pstarts[se] + jnp.arange(A, dtype=jnp.int32) - starts[se]
    row_tok = jnp.full((M,), N, jnp.int32).at[dest].set(flat_t[order])
    row_w = jnp.zeros((M,), flat_w.dtype).at[dest].set(flat_w[order])
    blk_e = jnp.clip(jnp.searchsorted(pends, jnp.arange(n_blocks, dtype=jnp.int32) * blk, side='right'), 0, E - 1)
    xpad = jnp.concatenate([xf, jnp.zeros((1, D), xf.dtype)], axis=0)
    xb = xpad[row_tok].reshape(n_blocks, blk, D)

    def expert_block(args):
        xblk, e = args
        h = jax.nn.silu(xblk @ w_gate[e]) * (xblk @ w_up[e])
        return h @ w_down[e]

    yb = lax.map(expert_block, (xb, blk_e))
    y = yb.reshape(M, D).astype(jnp.float32) * row_w[:, None]
    out = jnp.zeros((N + 1, D), jnp.float32).at[row_tok].add(y)[:N]
    return out.astype(xf.dtype)


def hier_moe(x, w_r1, w_r2, w_gate, w_up, w_down):
    B, T, D = x.shape
    xf = x.reshape(-1, D)
    N = xf.shape[0]
    g_prob = jax.nn.softmax((xf @ w_r1).astype(jnp.float32), axis=-1)
    g_p, g_idx = lax.top_k(g_prob, 1)
    e_logits = (xf @ w_r2).astype(jnp.float32).reshape(N, MOE_GROUPS, MOE_PER_GROUP)
    sel = jnp.broadcast_to(g_idx[:, :, None], (N, 1, MOE_PER_GROUP))
    e_logits = jnp.take_along_axis(e_logits, sel, axis=1)[:, 0]
    e_top, e_loc = lax.top_k(e_logits, MOE_TOPK)
    e_w = jax.nn.softmax(e_top, axis=-1) * g_p
    e_idx = g_idx * MOE_PER_GROUP + e_loc
    return grouped_experts(xf, e_idx, e_w, w_gate, w_up, w_down).reshape(B, T, D)


def decoder_layer(x, p_l, pos, kv_bufs, gla_s0, alpha, ln1_g, ln1_b, ln2_g, ln2_b, w_in, w_gk2, b_gk,
                  gla_norm_g, w_br_a, w_br_b, w_out, w_r1, w_r2, w_gate, w_up, w_down, w_ple_gate, w_ple):
    B, T, _ = x.shape
    u = x @ w_in
    cuts = [A_QKV_COLS]
    for width in (GLA_QK, GLA_QK, GLA_V, GLA_V, GLA_LR, D_MODEL):
        cuts.append(cuts[-1] + width)
    a_qkv, bq, bk, bv, br, blr, gate_a, gate_b = jnp.split(u, cuts, axis=-1)

    a_qkv = a_qkv.reshape(B, T, 3, A_GROUPS, A_HEADS, A_HEAD_DIM)
    outs, lses, kv_new = [], [], []
    for gi in range(A_GROUPS):
        W, d = A_WINDOWS[gi], A_DILATIONS[gi]
        reach = W // d
        q = rope_partial(a_qkv[:, :, 0, gi], pos).astype(jnp.float32)
        k = rope_partial(a_qkv[:, :, 1, gi], pos)
        v = a_qkv[:, :, 2, gi]
        if kv_bufs is None:
            o, lse = dilated_attention_prompt(q, k.astype(jnp.float32), v.astype(jnp.float32), d, reach)
            kv = jnp.stack([k, v], axis=2)[:, T - min(W, T):]
        else:
            o, lse, kv = dilated_attention_sample(q, k, v, kv_bufs[gi], d, reach, W)
        outs.append(o)
        lses.append(lse)
        kv_new.append(kv)
    wts = jax.nn.softmax(jnp.stack(lses, 0), axis=0)
    o_a = jnp.einsum('gbth,gbthe->bthe', wts, jnp.stack(outs, 0))
    y_a = o_a.reshape(B, T, A_WIDTH).astype(x.dtype) @ w_br_a

    q_g = bq.reshape(B, T, GLA_HEADS, GLA_DK).astype(jnp.float32) * (GLA_DK ** -0.5)
    k_g = bk.reshape(B, T, GLA_HEADS, GLA_DK).astype(jnp.float32)
    v_g = bv.reshape(B, T, GLA_HEADS, GLA_DV).astype(jnp.float32)
    log_a = jax.nn.log_sigmoid((blr @ w_gk2 + b_gk).astype(jnp.float32)).reshape(B, T, GLA_HEADS, GLA_DK) / GLA_TAU
    if gla_s0 is None:
        s0 = jnp.zeros((B, GLA_HEADS, GLA_DK, GLA_DV), jnp.float32)
    else:
        s0 = gla_s0.astype(jnp.float32)
    o_g, s_new = gla_chunked(q_g, k_g, v_g, log_a, s0)
    o_g = o_g * lax.rsqrt(jnp.mean(jnp.square(o_g), -1, keepdims=True) + LN_EPS) * gla_norm_g.astype(jnp.float32)
    o_g = o_g.reshape(B, T, GLA_V) * jax.nn.silu(br.astype(jnp.float32))
    y_b = o_g.astype(x.dtype) @ w_br_b

    mixed = jax.nn.sigmoid(gate_a) * y_a + jax.nn.sigmoid(gate_b) * y_b
    x = layer_norm(alpha * x + mixed @ w_out, ln1_g, ln1_b)

    ple = jax.nn.sigmoid(x @ w_ple_gate) * (p_l.astype(x.dtype) @ w_ple)
    x = layer_norm(alpha * x + hier_moe(x, w_r1, w_r2, w_gate, w_up, w_down) + ple, ln2_g, ln2_b)
    return x, kv_new, s_new


def setup_inputs(seed: int = 0) -> dict:
    key = jax.random.key(seed)
    kit = iter(jax.random.split(key, 32))

    def nrm(shape, scale):
        return jax.random.normal(next(kit), shape, jnp.float32) * scale

    beta = (8.0 * DEPTH) ** -0.25
    kv_shape = lambda w: (DEPTH, DEC_BATCH, min(w, PAST_LEN), 2, A_HEADS, A_HEAD_DIM)
    return {
        'x_prompt': nrm((BATCH, SEQ, D_MODEL), 1.0),
        'x_sample': nrm((DEC_BATCH, DEC_SEQ, D_MODEL), 1.0),
        'p_prompt': nrm((DEPTH, BATCH, SEQ, PLE_DIM), 1.0),
        'p_sample': nrm((DEPTH, DEC_BATCH, DEC_SEQ, PLE_DIM), 1.0),
        'cache_kv_a1': nrm(kv_shape(A_WINDOWS[0]), 1.0),
        'cache_kv_a2': nrm(kv_shape(A_WINDOWS[1]), 1.0),
        'cache_kv_a3': nrm(kv_shape(A_WINDOWS[2]), 1.0),
        'state_gla': nrm((DEPTH, DEC_BATCH, GLA_HEADS, GLA_DK, GLA_DV), 0.5),
        'ln_emb_g': 1.0 + nrm((D_MODEL,), 0.02),
        'ln_emb_b': nrm((D_MODEL,), 0.02),
        'ln1_g': 1.0 + nrm((DEPTH, D_MODEL), 0.02),
        'ln1_b': nrm((DEPTH, D_MODEL), 0.02),
        'ln2_g': 1.0 + nrm((DEPTH, D_MODEL), 0.02),
        'ln2_b': nrm((DEPTH, D_MODEL), 0.02),
        'w_in': nrm((DEPTH, D_MODEL, IN_COLS), D_MODEL ** -0.5),
        'w_gk2': nrm((DEPTH, GLA_LR, GLA_QK), GLA_LR ** -0.5),
        'b_gk': nrm((DEPTH, GLA_QK), 0.02),
        'gla_norm_g': 1.0 + nrm((DEPTH, GLA_DV), 0.02),
        'w_br_a': nrm((DEPTH, A_WIDTH, D_MODEL), A_WIDTH ** -0.5),
        'w_br_b': nrm((DEPTH, GLA_V, D_MODEL), GLA_V ** -0.5),
        'w_out': nrm((DEPTH, D_MODEL, D_MODEL), beta * D_MODEL ** -0.5),
        'w_router_group': nrm((DEPTH, D_MODEL, MOE_GROUPS), D_MODEL ** -0.5),
        'w_router_expert': nrm((DEPTH, D_MODEL, N_EXPERTS), D_MODEL ** -0.5),
        'w_gate': nrm((DEPTH, N_EXPERTS, D_MODEL, D_EXPERT), D_MODEL ** -0.5),
        'w_up': nrm((DEPTH, N_EXPERTS, D_MODEL, D_EXPERT), D_MODEL ** -0.5),
        'w_down': nrm((DEPTH, N_EXPERTS, D_EXPERT, D_MODEL), beta * D_EXPERT ** -0.5),
        'w_ple_gate': nrm((DEPTH, D_MODEL, D_MODEL), D_MODEL ** -0.5),
        'w_ple': nrm((DEPTH, PLE_DIM, D_MODEL), beta * PLE_DIM ** -0.5),
    }


def reference(x_prompt, x_sample, p_prompt, p_sample, cache_kv_a1, cache_kv_a2, cache_kv_a3, state_gla,
              ln_emb_g, ln_emb_b, ln1_g, ln1_b, ln2_g, ln2_b, w_in, w_gk2, b_gk, gla_norm_g, w_br_a, w_br_b,
              w_out, w_router_group, w_router_expert, w_gate, w_up, w_down, w_ple_gate, w_ple):
    alpha = (2.0 * DEPTH) ** 0.25

    def trunk(x, p, pos, kv_caches, gla_states):
        x = layer_norm(x, ln_emb_g, ln_emb_b)
        kv_rows = [[] for _ in range(A_GROUPS)]
        gla_rows = []
        for i in range(DEPTH):
            bufs = None if kv_caches is None else [c[i] for c in kv_caches]
            s0 = None if gla_states is None else gla_states[i]
            x, kv_new, s_new = decoder_layer(
                x, p[i], pos, bufs, s0, alpha, ln1_g[i], ln1_b[i], ln2_g[i], ln2_b[i], w_in[i], w_gk2[i],
                b_gk[i], gla_norm_g[i], w_br_a[i], w_br_b[i], w_out[i], w_router_group[i],
                w_router_expert[i], w_gate[i], w_up[i], w_down[i], w_ple_gate[i], w_ple[i])
            for gi in range(A_GROUPS):
                kv_rows[gi].append(kv_new[gi])
            gla_rows.append(s_new)
        return x, [jnp.stack(r) for r in kv_rows], jnp.stack(gla_rows)

    pos_prompt = jnp.arange(x_prompt.shape[1], dtype=jnp.int32)
    pos_sample = PAST_LEN + jnp.arange(x_sample.shape[1], dtype=jnp.int32)
    y_prompt, kv_p, gla_state_prompt = trunk(x_prompt, p_prompt, pos_prompt, None, None)
    y_sample, kv_s, gla_state_sample = trunk(x_sample, p_sample, pos_sample,
                                             (cache_kv_a1, cache_kv_a2, cache_kv_a3), state_gla)
    kv_a1_prompt, kv_a2_prompt, kv_a3_prompt = kv_p
    kv_a1_sample, kv_a2_sample, kv_a3_sample = kv_s
    return (y_prompt, y_sample, kv_a1_prompt, kv_a2_prompt, kv_a3_prompt, gla_state_prompt,
            kv_a1_sample, kv_a2_sample, kv_a3_sample, gla_state_sample)
```

```python
import functools

import jax
import jax.numpy as jnp
from jax import lax
from jax.experimental import pallas as pl
from jax.experimental.pallas import tpu as pltpu

F32 = jnp.float32
BF16 = jnp.bfloat16

D_MODEL = 2048
SEQ = 8192
DEC_BATCH = 32
DEC_SEQ = 4
PAST_LEN = 16384
N_SAMPLE = DEC_BATCH * DEC_SEQ
M_ROWS = SEQ + N_SAMPLE
A_WINDOWS = (128, 512, 2048)
A_DILATIONS = (1, 4, 16)
A_REACH = 128
A_HEADS = 8
A_HEAD_DIM = 128
A_WIDTH = A_HEADS * A_HEAD_DIM
ROT_DIM = 32
ROPE_THETA = 500000.0
A_QKV_COLS = 9 * A_WIDTH
GLA_HEADS = 4
GLA_DK = 256
GLA_DV = 512
GLA_LR = 16
GLA_TAU = 16.0
GLA_CHUNK = 64
GLA_QK = GLA_HEADS * GLA_DK
GLA_V = GLA_HEADS * GLA_DV
MAIN_COLS = A_QKV_COLS + 2 * GLA_QK + 2 * GLA_V
MOE_GROUPS = 4
MOE_PER_GROUP = 8
N_EXPERTS = 32
D_EXPERT = 1024
PLE_DIM = 256
LN_EPS = 1e-5
ALPHA = 2.0 ** 0.25

LANES = 128
VMEM_LIMIT = 56 * 1024 * 1024
ROW_TILE_PROJ = 1664
ROW_TILE = 832
COL_TILE = 512
LN_ROWS = 128
MOE_BLK = 256
MOE_NBLK = (2 * M_ROWS) // MOE_BLK + N_EXPERTS
MOE_ROWS = MOE_NBLK * MOE_BLK
ATT_SB = 2048
GLA_ROWS = 512


def _cp(sem, vmem=VMEM_LIMIT):
    return pltpu.CompilerParams(dimension_semantics=sem, vmem_limit_bytes=vmem)


def _sigmoid(x):
    return 1.0 / (1.0 + jnp.exp(-x))


def _dot(a, b):
    return jnp.dot(a, b, preferred_element_type=F32)


def _dot_nt(a, b):
    return lax.dot_general(a, b, (((1,), (1,)), ((), ())), preferred_element_type=F32)


def _split3(x):
    h = x.astype(BF16)
    r = x - h.astype(F32)
    m = r.astype(BF16)
    l = (r - m.astype(F32)).astype(BF16)
    return h, m, l


def _layer_norm_rows(x, g, b):
    mu = jnp.mean(x, axis=-1, keepdims=True)
    xc = x - mu
    var = jnp.mean(xc * xc, axis=-1, keepdims=True)
    return xc * lax.rsqrt(var + LN_EPS) * g + b


def _ln_emb_kernel(xp_ref, xs_ref, g_ref, b_ref, of_ref, ob_ref):
    i = pl.program_id(0)
    x = jnp.where(i < SEQ // LN_ROWS, xp_ref[...], xs_ref[...])
    y = _layer_norm_rows(x, g_ref[...], b_ref[...])
    of_ref[...] = y
    ob_ref[...] = y.astype(BF16)


def _ln_emb(xp, xs, g, b):
    npb = SEQ // LN_ROWS
    return pl.pallas_call(
        _ln_emb_kernel,
        grid=(npb + 1,),
        in_specs=[
            pl.BlockSpec((LN_ROWS, D_MODEL), lambda i: (jnp.minimum(i, npb - 1), 0)),
            pl.BlockSpec((N_SAMPLE, D_MODEL), lambda i: (0, 0)),
            pl.BlockSpec((1, D_MODEL), lambda i: (0, 0)),
            pl.BlockSpec((1, D_MODEL), lambda i: (0, 0)),
        ],
        out_specs=[
            pl.BlockSpec((LN_ROWS, D_MODEL), lambda i: (i, 0)),
            pl.BlockSpec((LN_ROWS, D_MODEL), lambda i: (i, 0)),
        ],
        out_shape=[jax.ShapeDtypeStruct((M_ROWS, D_MODEL), F32),
                   jax.ShapeDtypeStruct((M_ROWS, D_MODEL), BF16)],
        compiler_params=_cp(("arbitrary",)),
        name="ln_emb",
    )(xp, xs, g.reshape(1, D_MODEL), b.reshape(1, D_MODEL))


def _proj_kernel(x_ref, w_ref, c_ref, s1_ref, s2_ref, o_ref, *, n_rope_blocks):
    j = pl.program_id(1)
    acc = _dot(x_ref[...], w_ref[...].astype(BF16))

    @pl.when(j >= n_rope_blocks)
    def _():
        o_ref[...] = acc

    @pl.when(j < n_rope_blocks)
    def _():
        c, s1, s2 = c_ref[...], s1_ref[...], s2_ref[...]
        for t in range(COL_TILE // LANES):
            a = acc[:, t * LANES:(t + 1) * LANES]
            o_ref[:, t * LANES:(t + 1) * LANES] = (
                a * c + pltpu.roll(a, LANES - ROT_DIM // 2, 1) * s1 + pltpu.roll(a, ROT_DIM // 2, 1) * s2)


def _project(xb, w, n_cols, n_rope_blocks, rope):
    c, s1, s2 = rope
    tm, tn = ROW_TILE_PROJ, COL_TILE
    return pl.pallas_call(
        functools.partial(_proj_kernel, n_rope_blocks=n_rope_blocks),
        grid=(M_ROWS // tm, n_cols // tn),
        in_specs=[
            pl.BlockSpec((tm, D_MODEL), lambda i, j: (i, 0)),
            pl.BlockSpec((D_MODEL, tn), lambda i, j: (0, j)),
            pl.BlockSpec((tm, LANES), lambda i, j: (i, 0)),
            pl.BlockSpec((tm, LANES), lambda i, j: (i, 0)),
            pl.BlockSpec((tm, LANES), lambda i, j: (i, 0)),
        ],
        out_specs=pl.BlockSpec((tm, tn), lambda i, j: (i, j)),
        out_shape=jax.ShapeDtypeStruct((M_ROWS, n_cols), F32),
        compiler_params=_cp(("arbitrary", "arbitrary")),
        name="in_proj",
    )(xb, w, c, s1, s2)


def _rope_tables():
    half = ROT_DIM // 2
    inv = ROPE_THETA ** (-jnp.arange(half, dtype=F32) * (2.0 / ROT_DIM))
    pos = jnp.concatenate([
        jnp.arange(SEQ, dtype=jnp.int32),
        jnp.tile(PAST_LEN + jnp.arange(DEC_SEQ, dtype=jnp.int32), DEC_BATCH)])
    ang = pos.astype(F32)[:, None] * inv[None, :]
    cos, sin = jnp.cos(ang), jnp.sin(ang)
    ones = jnp.ones((M_ROWS, LANES - ROT_DIM), F32)
    zeros = jnp.zeros((M_ROWS, LANES - ROT_DIM), F32)
    zh = jnp.zeros((M_ROWS, half), F32)
    c = jnp.concatenate([cos, cos, ones], axis=1)
    s1 = jnp.concatenate([-sin, zh, zeros], axis=1)
    s2 = jnp.concatenate([zh, sin, zeros], axis=1)
    return c, s1, s2


def _gk_kernel(x_ref, wlr_ref, wgk_ref, b_ref, o_ref):
    blr = _dot(x_ref[...], wlr_ref[...])
    z = _dot(blr.astype(BF16), wgk_ref[...]) + b_ref[...]
    o_ref[...] = (jnp.minimum(z, 0.0) - jnp.log1p(jnp.exp(-jnp.abs(z)))) * (1.0 / GLA_TAU)


def _gla_gate(xb, w_lr, w_gk2, b_gk):
    wlr = jnp.zeros((D_MODEL, LANES), BF16).at[:, :GLA_LR].set(w_lr.astype(BF16))
    wgk = jnp.zeros((LANES, GLA_QK), BF16).at[:GLA_LR].set(w_gk2.astype(BF16))
    tm = ROW_TILE
    return pl.pallas_call(
        _gk_kernel,
        grid=(M_ROWS // tm,),
        in_specs=[
            pl.BlockSpec((tm, D_MODEL), lambda i: (i, 0)),
            pl.BlockSpec((D_MODEL, LANES), lambda i: (0, 0)),
            pl.BlockSpec((LANES, GLA_QK), lambda i: (0, 0)),
            pl.BlockSpec((1, GLA_QK), lambda i: (0, 0)),
        ],
        out_specs=pl.BlockSpec((tm, GLA_QK), lambda i: (i, 0)),
        out_shape=jax.ShapeDtypeStruct((M_ROWS, GLA_QK), F32),
        compiler_params=_cp(("arbitrary",)),
        name="gla_gate",
    )(xb, wlr, wgk, b_gk.reshape(1, GLA_QK))


def _gla_chunk(q, k, v, g, br, gn, s0, causal, tri_b, eye):
    n = g.shape[0]
    g1, g2, g3 = _split3(g)
    b = _dot(tri_b, g1) + _dot(tri_b, g2) + _dot(tri_b, g3)
    b_last = b[n - 1:n, :]
    q_dec = (q * ((GLA_DK ** -0.5) * jnp.exp(b))).astype(BF16)
    k_inv = (k * jnp.exp(-b)).astype(BF16)
    k_dec = k * jnp.exp(b_last - b)
    vb = v.astype(BF16)
    att = jnp.where(causal, _dot_nt(q_dec, k_inv), 0.0)
    o = _dot(att.astype(BF16), vb) + _dot(q_dec, s0.astype(BF16))
    d1, d2, d3 = _split3(jnp.exp(b_last))
    rid = lax.broadcasted_iota(jnp.int32, (n + GLA_TPAD, GLA_DK), 0)
    rows = jnp.concatenate([k_dec, jnp.zeros((GLA_TPAD, GLA_DK), F32)], axis=0)
    rows = jnp.where(rid == n, d1.astype(F32), rows)
    rows = jnp.where(rid == n + 1, d2.astype(F32), rows)
    rows = jnp.where(rid == n + 2, d3.astype(F32), rows)
    cols = _dot_nt(eye, rows.astype(BF16))
    dcol = cols[:, n:n + 1] + cols[:, n + 1:n + 2] + cols[:, n + 2:n + 3]
    s1 = s0 * dcol + _dot(cols[:, :n].astype(BF16), vb)
    ms = jnp.mean(o * o, axis=-1, keepdims=True)
    return o * lax.rsqrt(ms + LN_EPS) * gn * (br * _sigmoid(br)), s1


GLA_TPAD = 16


def _gla_consts(n):
    r = lax.broadcasted_iota(jnp.int32, (n, n), 0)
    c = lax.broadcasted_iota(jnp.int32, (n, n), 1)
    causal = r >= c
    er = lax.broadcasted_iota(jnp.int32, (GLA_DK, GLA_DK), 0)
    ec = lax.broadcasted_iota(jnp.int32, (GLA_DK, GLA_DK), 1)
    return causal, jnp.where(causal, 1.0, 0.0).astype(BF16), jnp.where(er == ec, 1.0, 0.0).astype(BF16)


def _gla_prompt_kernel(q_ref, k_ref, v_ref, br_ref, g_ref, gn_ref, o_ref, st_ref):
    @pl.when(pl.program_id(1) == 0)
    def _():
        st_ref[...] = jnp.zeros_like(st_ref)

    causal, tri_b, eye = _gla_consts(GLA_CHUNK)
    gn = gn_ref[...]

    def body(cc, carry):
        rows = pl.ds(pl.multiple_of(cc * GLA_CHUNK, GLA_CHUNK), GLA_CHUNK)
        o, s1 = _gla_chunk(q_ref[rows, :], k_ref[rows, :], v_ref[rows, :], g_ref[rows, :],
                           br_ref[rows, :], gn, st_ref[0], causal, tri_b, eye)
        st_ref[0] = s1
        o_ref[rows, :] = o.astype(BF16)
        return carry

    lax.fori_loop(0, GLA_ROWS // GLA_CHUNK, body, 0)


def _gla_prompt(u, gk, gnorm):
    qb0 = A_QKV_COLS // GLA_DK
    kb0 = (A_QKV_COLS + GLA_QK) // GLA_DK
    vb0 = (A_QKV_COLS + 2 * GLA_QK) // GLA_DV
    rb0 = (A_QKV_COLS + 2 * GLA_QK + GLA_V) // GLA_DV
    return pl.pallas_call(
        _gla_prompt_kernel,
        grid=(GLA_HEADS, SEQ // GLA_ROWS),
        in_specs=[
            pl.BlockSpec((GLA_ROWS, GLA_DK), lambda h, c: (c, qb0 + h)),
            pl.BlockSpec((GLA_ROWS, GLA_DK), lambda h, c: (c, kb0 + h)),
            pl.BlockSpec((GLA_ROWS, GLA_DV), lambda h, c: (c, vb0 + h)),
            pl.BlockSpec((GLA_ROWS, GLA_DV), lambda h, c: (c, rb0 + h)),
            pl.BlockSpec((GLA_ROWS, GLA_DK), lambda h, c: (c, h)),
            pl.BlockSpec((1, GLA_DV), lambda h, c: (0, 0)),
        ],
        out_specs=[
            pl.BlockSpec((GLA_ROWS, GLA_DV), lambda h, c: (c, h)),
            pl.BlockSpec((1, GLA_DK, GLA_DV), lambda h, c: (h, 0, 0)),
        ],
        out_shape=[jax.ShapeDtypeStruct((M_ROWS, GLA_V), BF16),
                   jax.ShapeDtypeStruct((GLA_HEADS, GLA_DK, GLA_DV), F32)],
        compiler_params=_cp(("arbitrary", "arbitrary")),
        name="gla_prompt",
    )(u, u, u, u, gk, gnorm.reshape(1, GLA_DV))


GLA_SPAD = 16


def _gla_sample_kernel(q_ref, k_ref, v_ref, br_ref, g_ref, gn_ref, s0_ref, o_ref, s1_ref):
    causal, tri_b, eye = _gla_consts(GLA_SPAD)
    gn = gn_ref[...]
    for h in range(GLA_HEADS):
        ks = slice(h * GLA_DK, (h + 1) * GLA_DK)
        vs = slice(h * GLA_DV, (h + 1) * GLA_DV)
        o, s1 = _gla_chunk(q_ref[0, :, ks], k_ref[0, :, ks], v_ref[0, :, vs], g_ref[0, :, ks],
                           br_ref[0, :, vs], gn, s0_ref[0, h], causal, tri_b, eye)
        s1_ref[0, h] = s1
        o_ref[0, :, vs] = o


def _gla_sample(q, k, v, br, g, gnorm, s0):
    def spec(c):
        return pl.BlockSpec((1, GLA_SPAD, c), lambda b: (b, 0, 0))
    st_spec = pl.BlockSpec((1, GLA_HEADS, GLA_DK, GLA_DV), lambda b: (b, 0, 0, 0))
    return pl.pallas_call(
        _gla_sample_kernel,
        grid=(DEC_BATCH,),
        in_specs=[spec(GLA_QK), spec(GLA_QK), spec(GLA_V), spec(GLA_V), spec(GLA_QK),
                  pl.BlockSpec((1, GLA_DV), lambda b: (0, 0)), st_spec],
        out_specs=[spec(GLA_V), st_spec],
        out_shape=[jax.ShapeDtypeStruct((DEC_BATCH, GLA_SPAD, GLA_V), F32),
                   jax.ShapeDtypeStruct((DEC_BATCH, GLA_HEADS, GLA_DK, GLA_DV), F32)],
        compiler_params=_cp(("arbitrary",)),
        name="gla_sample",
    )(q, k, v, br, g, gnorm.reshape(1, GLA_DV), s0)


def _attn_prompt_kernel(q0, q1, q2, kc0, kp0, vc0, vp0, kc1, kp1, vc1, vp1, kc2, kp2, vc2, vp2,
                        o_ref, kb0, vb0, kb1, vb1, kb2, vb2, acc_ref, m_ref, l_ref):
    sb = pl.program_id(0)
    groups = ((q0, kc0, kp0, vc0, vp0, kb0, vb0, 1),
              (q1, kc1, kp1, vc1, vp1, kb1, vb1, 4),
              (q2, kc2, kp2, vc2, vp2, kb2, vb2, 16))
    scale = A_HEAD_DIM ** -0.5
    row = lax.broadcasted_iota(jnp.int32, (A_REACH, A_REACH), 0)
    col = lax.broadcasted_iota(jnp.int32, (A_REACH, A_REACH), 1)
    cur_ok = col <= row
    prev_ok = col >= row

    for gi, (q_ref, kc, kp, vc, vp, kb, vb, d) in enumerate(groups):
        win = A_REACH * d
        kb[0:win, :] = kp[...]
        kb[win:win + ATT_SB, :] = kc[...]
        vb[0:win, :] = vp[...]
        vb[win:win + ATT_SB, :] = vc[...]
        shift = {1: 0, 4: 2, 16: 4}[d]

        def tile(idx, carry, q_ref=q_ref, kb=kb, vb=vb, d=d, win=win, shift=shift, gi=gi):
            r = idx & (d - 1)
            nb = idx >> shift
            start = r + nb * win
            if d == 1:
                start = pl.multiple_of(start, A_REACH)
                qrows = pl.ds(start, A_REACH)
                crows = pl.ds(start + win, A_REACH)
            else:
                qrows = pl.ds(start, A_REACH, stride=d)
                crows = pl.ds(start + win, A_REACH, stride=d)
            q = q_ref[qrows, :].astype(BF16)
            s_c = _dot_nt(q, kb[crows, :].astype(BF16)) * scale
            s_p = _dot_nt(q, kb[qrows, :].astype(BF16)) * scale
            prev_bias = jnp.where(jnp.logical_or(sb > 0, nb > 0), 0.0, -jnp.inf)
            s_c = jnp.where(cur_ok, s_c, -jnp.inf)
            s_p = jnp.where(prev_ok, s_p, -jnp.inf) + prev_bias
            m_t = jnp.maximum(jnp.max(s_c, axis=1, keepdims=True), jnp.max(s_p, axis=1, keepdims=True))
            p_c = jnp.exp(s_c - m_t)
            p_p = jnp.exp(s_p - m_t)
            l_t = jnp.sum(p_c, axis=1, keepdims=True) + jnp.sum(p_p, axis=1, keepdims=True)
            num = (_dot(p_c.astype(BF16), vb[crows, :].astype(BF16))
                   + _dot(p_p.astype(BF16), vb[qrows, :].astype(BF16)))
            m_t = jnp.broadcast_to(m_t, (A_REACH, LANES))
            l_t = jnp.broadcast_to(l_t, (A_REACH, LANES))
            if gi == 0:
                acc_ref[qrows, :] = num
                m_ref[qrows, :] = m_t
                l_ref[qrows, :] = l_t
            else:
                m_o = m_ref[qrows, :]
                m_n = jnp.maximum(m_o, m_t)
                a = jnp.exp(m_o - m_n)
                b = jnp.exp(m_t - m_n)
                acc_ref[qrows, :] = a * acc_ref[qrows, :] + b * num
                l_ref[qrows, :] = a * l_ref[qrows, :] + b * l_t
                m_ref[qrows, :] = m_n
            return carry

        lax.fori_loop(0, ATT_SB // A_REACH, tile, 0)

    o_ref[...] = (acc_ref[...] / l_ref[...]).astype(BF16)


def _attn_prompt(u):
    nh = A_HEADS
    in_specs, scratch = [], []
    for g in range(3):
        in_specs.append(pl.BlockSpec((ATT_SB, LANES), lambda sb, h, g=g: (sb, g * nh + h)))
    for g, d in enumerate(A_DILATIONS):
        win = A_REACH * d
        per = ATT_SB // win
        for part in (3, 6):
            cb = part * nh + g * nh
            in_specs.append(pl.BlockSpec((ATT_SB, LANES), lambda sb, h, cb=cb: (sb, cb + h)))
            in_specs.append(pl.BlockSpec(
                (win, LANES), lambda sb, h, cb=cb, per=per: (jnp.maximum(sb * per - 1, 0), cb + h)))
        scratch += [pltpu.VMEM((win + ATT_SB, LANES), F32), pltpu.VMEM((win + ATT_SB, LANES), F32)]
    scratch += [pltpu.VMEM((ATT_SB, LANES), F32)] * 3
    return pl.pallas_call(
        _attn_prompt_kernel,
        grid=(SEQ // ATT_SB, nh),
        in_specs=in_specs,
        out_specs=pl.BlockSpec((ATT_SB, LANES), lambda sb, h: (sb, h)),
        out_shape=jax.ShapeDtypeStruct((M_ROWS, A_WIDTH), BF16),
        scratch_shapes=scratch,
        compiler_params=_cp(("arbitrary", "arbitrary")),
        name="attn_prompt",
    )(*([u] * 15))


def _attn_sample_kernel(qkv_ref, c1_ref, c2_ref, c3_ref, o_ref):
    scale = A_HEAD_DIM ** -0.5
    pos = lax.broadcasted_iota(jnp.int32, (A_REACH, 1, 1), 0)

    def merge(state, m_t, l_t, num):
        if state is None:
            return m_t, l_t, num
        m_o, l_o, n_o = state
        m_n = jnp.maximum(m_o, m_t)
        a, b = jnp.exp(m_o - m_n), jnp.exp(m_t - m_n)
        return m_n, a * l_o + b * l_t, a * n_o + b * num

    for s in range(DEC_SEQ):
        state = None
        for g in range(3):
            q = qkv_ref[0, s, g]
            if g == 0:
                kt, vt = c1_ref[0, :, 0], c1_ref[0, :, 1]
                valid = pos >= s
                new_rows = range(s + 1)
            else:
                c_ref = c2_ref if g == 1 else c3_ref
                kt, vt = c_ref[0, :, 2 * s], c_ref[0, :, 2 * s + 1]
                valid = None
                new_rows = (s,)
            sc = jnp.sum(kt * q[None], axis=-1, keepdims=True) * scale
            if valid is not None:
                sc = jnp.where(valid, sc, -jnp.inf)
            s_new = [jnp.sum(qkv_ref[0, t, 3 + g] * q, axis=-1, keepdims=True) * scale for t in new_rows]
            m_t = jnp.max(sc, axis=0)
            for sn in s_new:
                m_t = jnp.maximum(m_t, sn)
            p = jnp.exp(sc - m_t[None])
            l_t = jnp.sum(p, axis=0)
            num = jnp.sum(p * vt, axis=0)
            for t, sn in zip(new_rows, s_new):
                pn = jnp.exp(sn - m_t)
                l_t = l_t + pn
                num = num + pn * qkv_ref[0, t, 6 + g]
            state = merge(state, m_t, l_t, num)
        o_ref[0, s] = state[2] / state[1]


def _attn_sample(qkv_s, c1, c2, c3):
    tile = (A_HEADS, A_HEAD_DIM)
    return pl.pallas_call(
        _attn_sample_kernel,
        grid=(DEC_BATCH,),
        in_specs=[
            pl.BlockSpec((1, DEC_SEQ, 9) + tile, lambda b: (b, 0, 0, 0, 0)),
            pl.BlockSpec((1, A_REACH, 2) + tile, lambda b: (b, 0, 0, 0, 0)),
            pl.BlockSpec((1, A_REACH, 2 * DEC_SEQ) + tile, lambda b: (b, 0, 0, 0, 0)),
            pl.BlockSpec((1, A_REACH, 2 * DEC_SEQ) + tile, lambda b: (b, 0, 0, 0, 0)),
        ],
        out_specs=pl.BlockSpec((1, DEC_SEQ) + tile, lambda b: (b, 0, 0, 0)),
        out_shape=jax.ShapeDtypeStruct((DEC_BATCH, DEC_SEQ) + tile, F32),
        compiler_params=_cp(("arbitrary",)),
        name="attn_sample",
    )(qkv_s, c1, c2, c3)


def _mix_kernel(oa_ref, og_ref, wa_ref, wb_ref, ga_ref, gb_ref, o_ref):
    ya = _dot(oa_ref[...], wa_ref[...])
    yb = _dot(og_ref[...], wb_ref[...])
    o_ref[...] = (_sigmoid(ga_ref[...]) * ya + _sigmoid(gb_ref[...]) * yb).astype(BF16)


def _mix(o_a, o_g, wa, wb, gates):
    tm, tn = ROW_TILE, COL_TILE
    nj = D_MODEL // tn
    return pl.pallas_call(
        _mix_kernel,
        grid=(M_ROWS // tm, nj),
        in_specs=[
            pl.BlockSpec((tm, A_WIDTH), lambda i, j: (i, 0)),
            pl.BlockSpec((tm, GLA_V), lambda i, j: (i, 0)),
            pl.BlockSpec((A_WIDTH, tn), lambda i, j: (0, j)),
            pl.BlockSpec((GLA_V, tn), lambda i, j: (0, j)),
            pl.BlockSpec((tm, tn), lambda i, j: (i, j)),
            pl.BlockSpec((tm, tn), lambda i, j: (i, nj + j)),
        ],
        out_specs=pl.BlockSpec((tm, tn), lambda i, j: (i, j)),
        out_shape=jax.ShapeDtypeStruct((M_ROWS, D_MODEL), BF16),
        compiler_params=_cp(("arbitrary", "arbitrary")),
        name="branch_mix",
    )(o_a, o_g, wa, wb, gates, gates)


def _out_ln_kernel(mx_ref, w_ref, xn_ref, g_ref, b_ref, of_ref, ob_ref, r_ref):
    j = pl.program_id(1)
    nj = D_MODEL // COL_TILE
    r_ref[j] = ALPHA * xn_ref[...] + _dot(mx_ref[...], w_ref[...])

    @pl.when(j == nj - 1)
    def _():
        mu = sum(jnp.sum(r_ref[t], axis=-1, keepdims=True) for t in range(nj)) * (1.0 / D_MODEL)
        var = sum(jnp.sum(jnp.square(r_ref[t] - mu), axis=-1, keepdims=True) for t in range(nj)) * (1.0 / D_MODEL)
        inv = lax.rsqrt(var + LN_EPS)
        for t in range(nj):
            cs = slice(t * COL_TILE, (t + 1) * COL_TILE)
            y = (r_ref[t] - mu) * inv * g_ref[:, cs] + b_ref[:, cs]
            of_ref[:, cs] = y
            ob_ref[:, cs] = y.astype(BF16)


def _out_ln(mixed, w_out, xn, g, b):
    tm, tn = ROW_TILE, COL_TILE
    return pl.pallas_call(
        _out_ln_kernel,
        grid=(M_ROWS // tm, D_MODEL // tn),
        in_specs=[
            pl.BlockSpec((tm, D_MODEL), lambda i, j: (i, 0)),
            pl.BlockSpec((D_MODEL, tn), lambda i, j: (0, j)),
            pl.BlockSpec((tm, tn), lambda i, j: (i, j)),
            pl.BlockSpec((1, D_MODEL), lambda i, j: (0, 0)),
            pl.BlockSpec((1, D_MODEL), lambda i, j: (0, 0)),
        ],
        out_specs=[pl.BlockSpec((tm, D_MODEL), lambda i, j: (i, 0)),
                   pl.BlockSpec((tm, D_MODEL), lambda i, j: (i, 0))],
        out_shape=[jax.ShapeDtypeStruct((M_ROWS, D_MODEL), F32),
                   jax.ShapeDtypeStruct((M_ROWS, D_MODEL), BF16)],
        scratch_shapes=[pltpu.VMEM((D_MODEL // tn, tm, tn), F32)],
        compiler_params=_cp(("arbitrary", "arbitrary")),
        name="out_proj_ln1",
    )(mixed, w_out, xn, g.reshape(1, D_MODEL), b.reshape(1, D_MODEL))


def _ple_kernel(xb_ref, wg_ref, p_ref, wp_ref, x_ref, o_ref):
    gate = _sigmoid(_dot(xb_ref[...], wg_ref[...]))
    o_ref[...] = ALPHA * x_ref[...] + gate * _dot(p_ref[...], wp_ref[...])


def _ple(x1b, wpg, p, wple, x1):
    tm, tn = ROW_TILE, COL_TILE
    return pl.pallas_call(
        _ple_kernel,
        grid=(M_ROWS // tm, D_MODEL // tn),
        in_specs=[
            pl.BlockSpec((tm, D_MODEL), lambda i, j: (i, 0)),
            pl.BlockSpec((D_MODEL, tn), lambda i, j: (0, j)),
            pl.BlockSpec((tm, PLE_DIM), lambda i, j: (i, 0)),
            pl.BlockSpec((PLE_DIM, tn), lambda i, j: (0, j)),
            pl.BlockSpec((tm, tn), lambda i, j: (i, j)),
        ],
        out_specs=pl.BlockSpec((tm, tn), lambda i, j: (i, j)),
        out_shape=jax.ShapeDtypeStruct((M_ROWS, D_MODEL), F32),
        compiler_params=_cp(("arbitrary", "arbitrary")),
        name="ple_residual",
    )(x1b, wpg, p, wple, x1)


def _router_kernel(x_ref, w_ref, idx_ref, wt_ref):
    xh, xm, _ = _split3(x_ref[...])
    wh, wm, _ = _split3(w_ref[...])
    logits = _dot(xh, wh) + _dot(xh, wm) + _dot(xm, wh)
    lane_i = lax.broadcasted_iota(jnp.int32, (x_ref.shape[0], LANES), 1)
    lane = lane_i.astype(F32)
    lane_group = (lane_i >> 3).astype(F32)
    neg = -jnp.inf

    def top1(v):
        mx = jnp.max(v, axis=1, keepdims=True)
        ix = jnp.min(jnp.where(v == mx, lane, float(LANES)), axis=1, keepdims=True)
        return mx, ix

    gl = jnp.where(lane_i < MOE_GROUPS, logits[:, :LANES], neg)
    g_max, g_idx = top1(gl)
    g_p = 1.0 / jnp.sum(jnp.exp(gl - g_max), axis=1, keepdims=True)
    el = logits[:, LANES:]
    in_group = jnp.logical_and(lane_i < N_EXPERTS, lane_group == g_idx)
    e1v = jnp.where(in_group, el, neg)
    m1, i1 = top1(e1v)
    m2, i2 = top1(jnp.where(lane == i1, neg, e1v))
    t = jnp.exp(m2 - m1)
    w1 = g_p / (1.0 + t)
    w2 = g_p * t / (1.0 + t)
    idx_ref[...] = jnp.where(lane_i == 0, i1, jnp.where(lane_i == 1, i2, 0.0)).astype(jnp.int32)
    wt_ref[...] = jnp.where(lane_i == 0, w1, jnp.where(lane_i == 1, w2, 0.0))


def _router(x1, w_r1, w_r2):
    w = jnp.zeros((D_MODEL, 2 * LANES), F32)
    w = w.at[:, :MOE_GROUPS].set(w_r1).at[:, LANES:LANES + N_EXPERTS].set(w_r2)
    tm = ROW_TILE
    return pl.pallas_call(
        _router_kernel,
        grid=(M_ROWS // tm,),
        in_specs=[pl.BlockSpec((tm, D_MODEL), lambda i: (i, 0)),
                  pl.BlockSpec((D_MODEL, 2 * LANES), lambda i: (0, 0))],
        out_specs=[pl.BlockSpec((tm, LANES), lambda i: (i, 0)),
                   pl.BlockSpec((tm, LANES), lambda i: (i, 0))],
        out_shape=[jax.ShapeDtypeStruct((M_ROWS, LANES), jnp.int32),
                   jax.ShapeDtypeStruct((M_ROWS, LANES), F32)],
        compiler_params=_cp(("arbitrary",)),
        name="router",
    )(x1, w)


def _expert_changed(be_ref, i):
    return jnp.logical_or(i == 0, be_ref[i] != be_ref[jnp.maximum(i - 1, 0)])


def _cast_rows(dst_ref, src_ref, n_rows, step=256):
    for r in range(0, n_rows, step):
        dst_ref[r:r + step, :] = src_ref[0, r:r + step, :].astype(BF16)


def _moe_up_kernel(be_ref, nv_ref, x_ref, wg_ref, wu_ref, h_ref, wgb, wub):
    i = pl.program_id(0)

    @pl.when(_expert_changed(be_ref, i))
    def _():
        _cast_rows(wgb, wg_ref, D_MODEL)
        _cast_rows(wub, wu_ref, D_MODEL)

    @pl.when(i < nv_ref[0])
    def _():
        x = x_ref[...]
        a = _dot(x, wgb[...])
        h_ref[...] = (a * _sigmoid(a) * _dot(x, wub[...])).astype(BF16)

    @pl.when(i >= nv_ref[0])
    def _():
        h_ref[...] = jnp.zeros_like(h_ref)


def _moe_down_kernel(be_ref, nv_ref, h_ref, wd_ref, y_ref, wdb):
    i = pl.program_id(0)

    @pl.when(_expert_changed(be_ref, i))
    def _():
        _cast_rows(wdb, wd_ref, D_EXPERT)

    @pl.when(i < nv_ref[0])
    def _():
        y_ref[...] = _dot(h_ref[...], wdb[...])

    @pl.when(i >= nv_ref[0])
    def _():
        y_ref[...] = jnp.zeros_like(y_ref)


def _moe_experts(xs, blk_e, n_valid, w_gate, w_up, w_down):
    h = pl.pallas_call(
        _moe_up_kernel,
        grid_spec=pltpu.PrefetchScalarGridSpec(
            num_scalar_prefetch=2,
            grid=(MOE_NBLK,),
            in_specs=[
                pl.BlockSpec((MOE_BLK, D_MODEL), lambda i, be, nv: (i, 0)),
                pl.BlockSpec((1, D_MODEL, D_EXPERT), lambda i, be, nv: (be[i], 0, 0)),
                pl.BlockSpec((1, D_MODEL, D_EXPERT), lambda i, be, nv: (be[i], 0, 0)),
            ],
            out_specs=pl.BlockSpec((MOE_BLK, D_EXPERT), lambda i, be, nv: (i, 0)),
            scratch_shapes=[pltpu.VMEM((D_MODEL, D_EXPERT), BF16), pltpu.VMEM((D_MODEL, D_EXPERT), BF16)],
        ),
        out_shape=jax.ShapeDtypeStruct((MOE_ROWS, D_EXPERT), BF16),
        compiler_params=_cp(("arbitrary",)),
        name="moe_up",
    )(blk_e, n_valid, xs, w_gate, w_up)
    return pl.pallas_call(
        _moe_down_kernel,
        grid_spec=pltpu.PrefetchScalarGridSpec(
            num_scalar_prefetch=2,
            grid=(MOE_NBLK,),
            in_specs=[
                pl.BlockSpec((MOE_BLK, D_EXPERT), lambda i, be, nv: (i, 0)),
                pl.BlockSpec((1, D_EXPERT, D_MODEL), lambda i, be, nv: (be[i], 0, 0)),
            ],
            out_specs=pl.BlockSpec((MOE_BLK, D_MODEL), lambda i, be, nv: (i, 0)),
            scratch_shapes=[pltpu.VMEM((D_EXPERT, D_MODEL), BF16)],
        ),
        out_shape=jax.ShapeDtypeStruct((MOE_ROWS, D_MODEL), F32),
        compiler_params=_cp(("arbitrary",)),
        name="moe_down",
    )(blk_e, n_valid, h, w_down)


def _moe_plan(e_idx):
    flat_e = e_idx.reshape(-1)
    n_asg = flat_e.shape[0]
    order = jnp.argsort(flat_e)
    se = flat_e[order]
    counts = jnp.zeros((N_EXPERTS,), jnp.int32).at[flat_e].add(1)
    starts = jnp.cumsum(counts) - counts
    pcounts = (counts + MOE_BLK - 1) // MOE_BLK * MOE_BLK
    pends = jnp.cumsum(pcounts)
    pstarts = pends - pcounts
    dest_sorted = pstarts[se] + jnp.arange(n_asg, dtype=jnp.int32) - starts[se]
    dest = jnp.zeros((n_asg,), jnp.int32).at[order].set(dest_sorted)
    row_tok = jnp.zeros((MOE_ROWS,), jnp.int32).at[dest_sorted].set((order // 2).astype(jnp.int32))
    n_valid = (pends[-1] // MOE_BLK).astype(jnp.int32)
    blk_start = jnp.arange(MOE_NBLK, dtype=jnp.int32) * MOE_BLK
    blk_e = jnp.searchsorted(pends, jnp.minimum(blk_start, pends[-1] - 1), side='right')
    blk_e = jnp.clip(blk_e, 0, N_EXPERTS - 1).astype(jnp.int32)
    return dest.reshape(-1, 2), row_tok, blk_e, n_valid.reshape(1)


def _final_kernel(base_ref, y1_ref, y2_ref, wt_ref, g_ref, b_ref, op_ref, os_ref):
    i = pl.program_id(0)
    wt = wt_ref[...]
    r = base_ref[...] + wt[:, 0:1] * y1_ref[...] + wt[:, 1:2] * y2_ref[...]
    y = _layer_norm_rows(r, g_ref[...], b_ref[...])

    @pl.when(i < SEQ // LN_ROWS)
    def _():
        op_ref[...] = y

    @pl.when(i >= SEQ // LN_ROWS)
    def _():
        os_ref[...] = y


def _final(base, yg, wt, g, b):
    npb = SEQ // LN_ROWS
    return pl.pallas_call(
        _final_kernel,
        grid=(npb + 1,),
        in_specs=[
            pl.BlockSpec((LN_ROWS, D_MODEL), lambda i: (i, 0)),
            pl.BlockSpec((LN_ROWS, D_MODEL), lambda i: (i, 0)),
            pl.BlockSpec((LN_ROWS, D_MODEL), lambda i: (i, 1)),
            pl.BlockSpec((LN_ROWS, LANES), lambda i: (i, 0)),
            pl.BlockSpec((1, D_MODEL), lambda i: (0, 0)),
            pl.BlockSpec((1, D_MODEL), lambda i: (0, 0)),
        ],
        out_specs=[pl.BlockSpec((LN_ROWS, D_MODEL), lambda i: (jnp.minimum(i, npb - 1), 0)),
                   pl.BlockSpec((N_SAMPLE, D_MODEL), lambda i: (0, 0))],
        out_shape=[jax.ShapeDtypeStruct((SEQ, D_MODEL), F32),
                   jax.ShapeDtypeStruct((N_SAMPLE, D_MODEL), F32)],
        compiler_params=_cp(("arbitrary",)),
        name="combine_ln2",
    )(base, yg, yg, wt, g.reshape(1, D_MODEL), b.reshape(1, D_MODEL))


def kernel(x_prompt, x_sample, p_prompt, p_sample, cache_kv_a1, cache_kv_a2, cache_kv_a3, state_gla,
           ln_emb_g, ln_emb_b, ln1_g, ln1_b, ln2_g, ln2_b, w_in, w_gk2, b_gk, gla_norm_g, w_br_a, w_br_b,
           w_out, w_router_group, w_router_expert, w_gate, w_up, w_down, w_ple_gate, w_ple):
    xp = x_prompt.reshape(SEQ, D_MODEL)
    xs = x_sample.reshape(N_SAMPLE, D_MODEL)
    xn, xb = _ln_emb(xp, xs, ln_emb_g, ln_emb_b)

    w_in0 = w_in[0]
    rope = _rope_tables()
    u = _project(xb, w_in0, MAIN_COLS, 2 * 3 * A_WIDTH // COL_TILE, rope)
    gates = _project(xb, w_in0[:, MAIN_COLS + GLA_LR:], 2 * D_MODEL, 0, rope)
    gk = _gla_gate(xb, w_in0[:, MAIN_COLS:MAIN_COLS + GLA_LR], w_gk2[0], b_gk[0])

    o_a = _attn_prompt(u)
    u_s = u[SEQ:]
    qkv_s = u_s[:, :A_QKV_COLS].reshape(DEC_BATCH, DEC_SEQ, 9, A_HEADS, A_HEAD_DIM)
    c1 = cache_kv_a1.reshape(DEC_BATCH, A_REACH, 2, A_HEADS, A_HEAD_DIM)
    c2 = cache_kv_a2.reshape(DEC_BATCH, A_REACH, 4 * 2, A_HEADS, A_HEAD_DIM)
    c3 = cache_kv_a3.reshape(DEC_BATCH, A_REACH, 16 * 2, A_HEADS, A_HEAD_DIM)
    o_as = _attn_sample(qkv_s, c1, c2, c3)
    o_a = lax.dynamic_update_slice(o_a, o_as.reshape(N_SAMPLE, A_WIDTH).astype(BF16), (SEQ, 0))

    o_g, st_t = _gla_prompt(u, gk, gla_norm_g[0])

    def spad(a):
        a = a.reshape(DEC_BATCH, DEC_SEQ, a.shape[-1])
        return jnp.pad(a, ((0, 0), (0, GLA_SPAD - DEC_SEQ), (0, 0)))

    c0 = A_QKV_COLS
    o_gs, st_s = _gla_sample(
        spad(u_s[:, c0:c0 + GLA_QK]), spad(u_s[:, c0 + GLA_QK:c0 + 2 * GLA_QK]),
        spad(u_s[:, c0 + 2 * GLA_QK:c0 + 2 * GLA_QK + GLA_V]), spad(u_s[:, c0 + 2 * GLA_QK + GLA_V:]),
        spad(gk[SEQ:]), gla_norm_g[0], state_gla[0])
    o_gs = o_gs[:, :DEC_SEQ].reshape(N_SAMPLE, GLA_V).astype(BF16)
    o_g = lax.dynamic_update_slice(o_g, o_gs, (SEQ, 0))

    mixed = _mix(o_a, o_g, w_br_a[0].astype(BF16), w_br_b[0].astype(BF16), gates)
    x1, x1b = _out_ln(mixed, w_out[0].astype(BF16), xn, ln1_g[0], ln1_b[0])

    p = jnp.concatenate([p_prompt[0].reshape(SEQ, PLE_DIM), p_sample[0].reshape(N_SAMPLE, PLE_DIM)], axis=0)
    base = _ple(x1b, w_ple_gate[0].astype(BF16), p.astype(BF16), w_ple[0].astype(BF16), x1)
    ridx, rwt = _router(x1, w_router_group[0], w_router_expert[0])
    dest, row_tok, blk_e, n_valid = _moe_plan(ridx[:, :2])
    y_rows = _moe_experts(x1b[row_tok], blk_e, n_valid, w_gate[0], w_up[0], w_down[0])
    yg = y_rows[dest].reshape(M_ROWS, 2 * D_MODEL)
    y_p, y_s = _final(base, yg, rwt, ln2_g[0], ln2_b[0])

    def kv_prompt(g):
        w = A_WINDOWS[g]
        k = u[SEQ - w:SEQ, (3 + g) * A_WIDTH:(4 + g) * A_WIDTH].reshape(w, A_HEADS, A_HEAD_DIM)
        v = u[SEQ - w:SEQ, (6 + g) * A_WIDTH:(7 + g) * A_WIDTH].reshape(w, A_HEADS, A_HEAD_DIM)
        return jnp.stack([k, v], axis=1)[None, None]

    def kv_sample(g, cache):
        new = jnp.stack([qkv_s[:, :, 3 + g], qkv_s[:, :, 6 + g]], axis=2)[None]
        return jnp.concatenate([cache[:, :, DEC_SEQ:], new], axis=2)

    gla_state_prompt = st_t[None, None]
    return (y_p.reshape(1, SEQ, D_MODEL), y_s.reshape(DEC_BATCH, DEC_SEQ, D_MODEL),
            kv_prompt(0), kv_prompt(1), kv_prompt(2), gla_state_prompt,
            kv_sample(0, cache_kv_a1), kv_sample(1, cache_kv_a2), kv_sample(2, cache_kv_a3), st_s[None])
```

```python
import functools

import jax
import jax.numpy as jnp
from jax import lax
from jax.experimental import pallas as pl
from jax.experimental.pallas import tpu as pltpu

F32 = jnp.float32
BF16 = jnp.bfloat16

D_MODEL = 2048
SEQ = 8192
DEC_BATCH = 32
DEC_SEQ = 4
PAST_LEN = 16384
N_SAMPLE = DEC_BATCH * DEC_SEQ
M_ROWS = SEQ + N_SAMPLE
A_WINDOWS = (128, 512, 2048)
A_DILATIONS = (1, 4, 16)
A_REACH = 128
A_HEADS = 8
A_HEAD_DIM = 128
A_WIDTH = A_HEADS * A_HEAD_DIM
ROT_DIM = 32
ROPE_THETA = 500000.0
A_QKV_COLS = 9 * A_WIDTH
GLA_HEADS = 4
GLA_DK = 256
GLA_DV = 512
GLA_LR = 16
GLA_TAU = 16.0
GLA_CHUNK = 64
GLA_QK = GLA_HEADS * GLA_DK
GLA_V = GLA_HEADS * GLA_DV
MAIN_COLS = A_QKV_COLS + 2 * GLA_QK + 2 * GLA_V
MOE_GROUPS = 4
MOE_PER_GROUP = 8
N_EXPERTS = 32
D_EXPERT = 1024
PLE_DIM = 256
LN_EPS = 1e-5
ALPHA = 2.0 ** 0.25

LANES = 128
VMEM_LIMIT = 56 * 1024 * 1024
ROW_TILE_PROJ = 1664
ROW_TILE = 832
COL_TILE = 512
LN_ROWS = 128
MOE_BLK = 256
MOE_NBLK = (2 * M_ROWS) // MOE_BLK + N_EXPERTS
MOE_ROWS = MOE_NBLK * MOE_BLK
ATT_SB = 2048
GLA_ROWS = 512
ATT_UNROLL = 4
GLA_UNROLL = 4


def _cp(sem, vmem=VMEM_LIMIT):
    return pltpu.CompilerParams(dimension_semantics=sem, vmem_limit_bytes=vmem)


def _sigmoid(x):
    return 1.0 / (1.0 + jnp.exp(-x))


def _dot(a, b):
    return jnp.dot(a, b, preferred_element_type=F32)


def _dot_nt(a, b):
    return lax.dot_general(a, b, (((1,), (1,)), ((), ())), preferred_element_type=F32)


def _split3(x):
    h = x.astype(BF16)
    r = x - h.astype(F32)
    m = r.astype(BF16)
    l = (r - m.astype(F32)).astype(BF16)
    return h, m, l


def _layer_norm_rows(x, g, b):
    mu = jnp.mean(x, axis=-1, keepdims=True)
    xc = x - mu
    var = jnp.mean(xc * xc, axis=-1, keepdims=True)
    return xc * lax.rsqrt(var + LN_EPS) * g + b


def _ln_emb_kernel(xp_ref, xs_ref, g_ref, b_ref, of_ref, ob_ref):
    i = pl.program_id(0)
    x = jnp.where(i < SEQ // LN_ROWS, xp_ref[...], xs_ref[...])
    y = _layer_norm_rows(x, g_ref[...], b_ref[...])
    of_ref[...] = y
    ob_ref[...] = y.astype(BF16)


def _ln_emb(xp, xs, g, b):
    npb = SEQ // LN_ROWS
    return pl.pallas_call(
        _ln_emb_kernel,
        grid=(npb + 1,),
        in_specs=[
            pl.BlockSpec((LN_ROWS, D_MODEL), lambda i: (jnp.minimum(i, npb - 1), 0)),
            pl.BlockSpec((N_SAMPLE, D_MODEL), lambda i: (0, 0)),
            pl.BlockSpec((1, D_MODEL), lambda i: (0, 0)),
            pl.BlockSpec((1, D_MODEL), lambda i: (0, 0)),
        ],
        out_specs=[
            pl.BlockSpec((LN_ROWS, D_MODEL), lambda i: (i, 0)),
            pl.BlockSpec((LN_ROWS, D_MODEL), lambda i: (i, 0)),
        ],
        out_shape=[jax.ShapeDtypeStruct((M_ROWS, D_MODEL), F32),
                   jax.ShapeDtypeStruct((M_ROWS, D_MODEL), BF16)],
        compiler_params=_cp(("arbitrary",)),
        name="ln_emb",
    )(xp, xs, g.reshape(1, D_MODEL), b.reshape(1, D_MODEL))


def _proj_kernel(x_ref, w_ref, c_ref, s1_ref, s2_ref, o_ref, *, n_rope_blocks):
    j = pl.program_id(1)
    acc = _dot(x_ref[...], w_ref[...].astype(BF16))

    @pl.when(j >= n_rope_blocks)
    def _():
        o_ref[...] = acc

    @pl.when(j < n_rope_blocks)
    def _():
        c, s1, s2 = c_ref[...], s1_ref[...], s2_ref[...]
        for t in range(COL_TILE // LANES):
            a = acc[:, t * LANES:(t + 1) * LANES]
            o_ref[:, t * LANES:(t + 1) * LANES] = (
                a * c + pltpu.roll(a, LANES - ROT_DIM // 2, 1) * s1 + pltpu.roll(a, ROT_DIM // 2, 1) * s2)


def _project(xb, w, n_cols, n_rope_blocks, rope):
    c, s1, s2 = rope
    tm, tn = ROW_TILE_PROJ, COL_TILE
    return pl.pallas_call(
        functools.partial(_proj_kernel, n_rope_blocks=n_rope_blocks),
        grid=(M_ROWS // tm, n_cols // tn),
        in_specs=[
            pl.BlockSpec((tm, D_MODEL), lambda i, j: (i, 0)),
            pl.BlockSpec((D_MODEL, tn), lambda i, j: (0, j)),
            pl.BlockSpec((tm, LANES), lambda i, j: (i, 0)),
            pl.BlockSpec((tm, LANES), lambda i, j: (i, 0)),
            pl.BlockSpec((tm, LANES), lambda i, j: (i, 0)),
        ],
        out_specs=pl.BlockSpec((tm, tn), lambda i, j: (i, j)),
        out_shape=jax.ShapeDtypeStruct((M_ROWS, n_cols), F32),
        compiler_params=_cp(("arbitrary", "arbitrary")),
        name="in_proj",
    )(xb, w, c, s1, s2)


def _rope_tables():
    half = ROT_DIM // 2
    inv = ROPE_THETA ** (-jnp.arange(half, dtype=F32) * (2.0 / ROT_DIM))
    pos = jnp.concatenate([
        jnp.arange(SEQ, dtype=jnp.int32),
        jnp.tile(PAST_LEN + jnp.arange(DEC_SEQ, dtype=jnp.int32), DEC_BATCH)])
    ang = pos.astype(F32)[:, None] * inv[None, :]
    cos, sin = jnp.cos(ang), jnp.sin(ang)
    ones = jnp.ones((M_ROWS, LANES - ROT_DIM), F32)
    zeros = jnp.zeros((M_ROWS, LANES - ROT_DIM), F32)
    zh = jnp.zeros((M_ROWS, half), F32)
    c = jnp.concatenate([cos, cos, ones], axis=1)
    s1 = jnp.concatenate([-sin, zh, zeros], axis=1)
    s2 = jnp.concatenate([zh, sin, zeros], axis=1)
    return c, s1, s2


def _gk_kernel(x_ref, wlr_ref, wgk_ref, b_ref, o_ref):
    blr = _dot(x_ref[...], wlr_ref[...])
    z = _dot(blr.astype(BF16), wgk_ref[...]) + b_ref[...]
    o_ref[...] = (jnp.minimum(z, 0.0) - jnp.log1p(jnp.exp(-jnp.abs(z)))) * (1.0 / GLA_TAU)


def _gla_gate(xb, w_lr, w_gk2, b_gk):
    wlr = jnp.zeros((D_MODEL, LANES), BF16).at[:, :GLA_LR].set(w_lr.astype(BF16))
    wgk = jnp.zeros((LANES, GLA_QK), BF16).at[:GLA_LR].set(w_gk2.astype(BF16))
    tm = ROW_TILE
    return pl.pallas_call(
        _gk_kernel,
        grid=(M_ROWS // tm,),
        in_specs=[
            pl.BlockSpec((tm, D_MODEL), lambda i: (i, 0)),
            pl.BlockSpec((D_MODEL, LANES), lambda i: (0, 0)),
            pl.BlockSpec((LANES, GLA_QK), lambda i: (0, 0)),
            pl.BlockSpec((1, GLA_QK), lambda i: (0, 0)),
        ],
        out_specs=pl.BlockSpec((tm, GLA_QK), lambda i: (i, 0)),
        out_shape=jax.ShapeDtypeStruct((M_ROWS, GLA_QK), F32),
        compiler_params=_cp(("arbitrary",)),
        name="gla_gate",
    )(xb, wlr, wgk, b_gk.reshape(1, GLA_QK))


def _gla_chunk(q, k, v, g, br, gn, s0, causal, tri_b, eye):
    n = g.shape[0]
    g1, g2, g3 = _split3(g)
    b = _dot(tri_b, g1) + _dot(tri_b, g2) + _dot(tri_b, g3)
    b_last = b[n - 1:n, :]
    q_dec = (q * ((GLA_DK ** -0.5) * jnp.exp(b))).astype(BF16)
    k_inv = (k * jnp.exp(-b)).astype(BF16)
    k_dec = k * jnp.exp(b_last - b)
    vb = v.astype(BF16)
    att = jnp.where(causal, _dot_nt(q_dec, k_inv), 0.0)
    o = _dot(att.astype(BF16), vb) + _dot(q_dec, s0.astype(BF16))
    d1, d2, d3 = _split3(jnp.exp(b_last))
    rid = lax.broadcasted_iota(jnp.int32, (n + GLA_TPAD, GLA_DK), 0)
    rows = jnp.concatenate([k_dec, jnp.zeros((GLA_TPAD, GLA_DK), F32)], axis=0)
    rows = jnp.where(rid == n, d1.astype(F32), rows)
    rows = jnp.where(rid == n + 1, d2.astype(F32), rows)
    rows = jnp.where(rid == n + 2, d3.astype(F32), rows)
    cols = _dot_nt(eye, rows.astype(BF16))
    dcol = cols[:, n:n + 1] + cols[:, n + 1:n + 2] + cols[:, n + 2:n + 3]
    s1 = s0 * dcol + _dot(cols[:, :n].astype(BF16), vb)
    ms = jnp.mean(o * o, axis=-1, keepdims=True)
    return o * lax.rsqrt(ms + LN_EPS) * gn * (br * _sigmoid(br)), s1


GLA_TPAD = 16


def _gla_consts(n):
    r = lax.broadcasted_iota(jnp.int32, (n, n), 0)
    c = lax.broadcasted_iota(jnp.int32, (n, n), 1)
    causal = r >= c
    er = lax.broadcasted_iota(jnp.int32, (GLA_DK, GLA_DK), 0)
    ec = lax.broadcasted_iota(jnp.int32, (GLA_DK, GLA_DK), 1)
    return causal, jnp.where(causal, 1.0, 0.0).astype(BF16), jnp.where(er == ec, 1.0, 0.0).astype(BF16)


def _gla_prompt_kernel(q_ref, k_ref, v_ref, br_ref, g_ref, gn_ref, o_ref, st_ref):
    @pl.when(pl.program_id(1) == 0)
    def _():
        st_ref[...] = jnp.zeros_like(st_ref)

    causal, tri_b, eye = _gla_consts(GLA_CHUNK)
    gn = gn_ref[...]

    def body(cc, carry):
        rows = pl.ds(pl.multiple_of(cc * GLA_CHUNK, GLA_CHUNK), GLA_CHUNK)
        o, s1 = _gla_chunk(q_ref[rows, :], k_ref[rows, :], v_ref[rows, :], g_ref[rows, :],
                           br_ref[rows, :], gn, st_ref[0], causal, tri_b, eye)
        st_ref[0] = s1
        o_ref[rows, :] = o.astype(BF16)
        return carry

    lax.fori_loop(0, GLA_ROWS // GLA_CHUNK, body, 0, unroll=GLA_UNROLL)


def _gla_prompt(u, gk, gnorm):
    qb0 = A_QKV_COLS // GLA_DK
    kb0 = (A_QKV_COLS + GLA_QK) // GLA_DK
    vb0 = (A_QKV_COLS + 2 * GLA_QK) // GLA_DV
    rb0 = (A_QKV_COLS + 2 * GLA_QK + GLA_V) // GLA_DV
    return pl.pallas_call(
        _gla_prompt_kernel,
        grid=(GLA_HEADS, SEQ // GLA_ROWS),
        in_specs=[
            pl.BlockSpec((GLA_ROWS, GLA_DK), lambda h, c: (c, qb0 + h)),
            pl.BlockSpec((GLA_ROWS, GLA_DK), lambda h, c: (c, kb0 + h)),
            pl.BlockSpec((GLA_ROWS, GLA_DV), lambda h, c: (c, vb0 + h)),
            pl.BlockSpec((GLA_ROWS, GLA_DV), lambda h, c: (c, rb0 + h)),
            pl.BlockSpec((GLA_ROWS, GLA_DK), lambda h, c: (c, h)),
            pl.BlockSpec((1, GLA_DV), lambda h, c: (0, 0)),
        ],
        out_specs=[
            pl.BlockSpec((GLA_ROWS, GLA_DV), lambda h, c: (c, h)),
            pl.BlockSpec((1, GLA_DK, GLA_DV), lambda h, c: (h, 0, 0)),
        ],
        out_shape=[jax.ShapeDtypeStruct((SEQ, GLA_V), BF16),
                   jax.ShapeDtypeStruct((GLA_HEADS, GLA_DK, GLA_DV), F32)],
        compiler_params=_cp(("arbitrary", "arbitrary")),
        name="gla_prompt",
    )(u, u, u, u, gk, gnorm.reshape(1, GLA_DV))


GLA_SPAD = 16


def _gla_sample_kernel(q_ref, k_ref, v_ref, br_ref, g_ref, gn_ref, s0_ref, o_ref, s1_ref):
    causal, tri_b, eye = _gla_consts(GLA_SPAD)
    gn = gn_ref[...]
    for h in range(GLA_HEADS):
        ks = slice(h * GLA_DK, (h + 1) * GLA_DK)
        vs = slice(h * GLA_DV, (h + 1) * GLA_DV)
        o, s1 = _gla_chunk(q_ref[0, :, ks], k_ref[0, :, ks], v_ref[0, :, vs], g_ref[0, :, ks],
                           br_ref[0, :, vs], gn, s0_ref[0, h], causal, tri_b, eye)
        s1_ref[0, h] = s1
        o_ref[0, :, vs] = o


def _gla_sample(q, k, v, br, g, gnorm, s0):
    def spec(c):
        return pl.BlockSpec((1, GLA_SPAD, c), lambda b: (b, 0, 0))
    st_spec = pl.BlockSpec((1, GLA_HEADS, GLA_DK, GLA_DV), lambda b: (b, 0, 0, 0))
    return pl.pallas_call(
        _gla_sample_kernel,
        grid=(DEC_BATCH,),
        in_specs=[spec(GLA_QK), spec(GLA_QK), spec(GLA_V), spec(GLA_V), spec(GLA_QK),
                  pl.BlockSpec((1, GLA_DV), lambda b: (0, 0)), st_spec],
        out_specs=[spec(GLA_V), st_spec],
        out_shape=[jax.ShapeDtypeStruct((DEC_BATCH, GLA_SPAD, GLA_V), F32),
                   jax.ShapeDtypeStruct((DEC_BATCH, GLA_HEADS, GLA_DK, GLA_DV), F32)],
        compiler_params=_cp(("arbitrary",)),
        name="gla_sample",
    )(q, k, v, br, g, gnorm.reshape(1, GLA_DV), s0)


def _attn_prompt_kernel(q0, q1, q2, kc0, kp0, vc0, vp0, kc1, kp1, vc1, vp1, kc2, kp2, vc2, vp2,
                        o_ref, kb0, vb0, kb1, vb1, kb2, vb2, acc_ref, m_ref, l_ref):
    sb = pl.program_id(0)
    groups = ((q0, kc0, kp0, vc0, vp0, kb0, vb0, 1),
              (q1, kc1, kp1, vc1, vp1, kb1, vb1, 4),
              (q2, kc2, kp2, vc2, vp2, kb2, vb2, 16))
    scale = A_HEAD_DIM ** -0.5
    row = lax.broadcasted_iota(jnp.int32, (A_REACH, A_REACH), 0)
    col = lax.broadcasted_iota(jnp.int32, (A_REACH, A_REACH), 1)
    cur_ok = col <= row
    prev_ok = col >= row

    for gi, (q_ref, kc, kp, vc, vp, kb, vb, d) in enumerate(groups):
        win = A_REACH * d
        kb[0:win, :] = kp[...]
        kb[win:win + ATT_SB, :] = kc[...]
        vb[0:win, :] = vp[...]
        vb[win:win + ATT_SB, :] = vc[...]
        shift = {1: 0, 4: 2, 16: 4}[d]

        def tile(idx, carry, q_ref=q_ref, kb=kb, vb=vb, d=d, win=win, shift=shift, gi=gi):
            r = idx & (d - 1)
            nb = idx >> shift
            start = r + nb * win
            if d == 1:
                start = pl.multiple_of(start, A_REACH)
                qrows = pl.ds(start, A_REACH)
                crows = pl.ds(start + win, A_REACH)
            else:
                qrows = pl.ds(start, A_REACH, stride=d)
                crows = pl.ds(start + win, A_REACH, stride=d)
            q = q_ref[qrows, :].astype(BF16)
            s_c = _dot_nt(q, kb[crows, :].astype(BF16)) * scale
            s_p = _dot_nt(q, kb[qrows, :].astype(BF16)) * scale
            prev_bias = jnp.where(jnp.logical_or(sb > 0, nb > 0), 0.0, -jnp.inf)
            s_c = jnp.where(cur_ok, s_c, -jnp.inf)
            s_p = jnp.where(prev_ok, s_p, -jnp.inf) + prev_bias
            m_t = jnp.maximum(jnp.max(s_c, axis=1, keepdims=True), jnp.max(s_p, axis=1, keepdims=True))
            p_c = jnp.exp(s_c - m_t)
            p_p = jnp.exp(s_p - m_t)
            l_t = jnp.sum(p_c, axis=1, keepdims=True) + jnp.sum(p_p, axis=1, keepdims=True)
            num = (_dot(p_c.astype(BF16), vb[crows, :].astype(BF16))
                   + _dot(p_p.astype(BF16), vb[qrows, :].astype(BF16)))
            m_t = jnp.broadcast_to(m_t, (A_REACH, LANES))
            l_t = jnp.broadcast_to(l_t, (A_REACH, LANES))
            if gi == 0:
                acc_ref[qrows, :] = num
                m_ref[qrows, :] = m_t
                l_ref[qrows, :] = l_t
            else:
                m_o = m_ref[qrows, :]
                m_n = jnp.maximum(m_o, m_t)
                a = jnp.exp(m_o - m_n)
                b = jnp.exp(m_t - m_n)
                acc_ref[qrows, :] = a * acc_ref[qrows, :] + b * num
                l_ref[qrows, :] = a * l_ref[qrows, :] + b * l_t
                m_ref[qrows, :] = m_n
            return carry

        lax.fori_loop(0, ATT_SB // A_REACH, tile, 0, unroll=ATT_UNROLL)

    o_ref[...] = (acc_ref[...] / l_ref[...]).astype(BF16)


def _attn_prompt(u):
    nh = A_HEADS
    in_specs, scratch = [], []
    for g in range(3):
        in_specs.append(pl.BlockSpec((ATT_SB, LANES), lambda sb, h, g=g: (sb, g * nh + h)))
    for g, d in enumerate(A_DILATIONS):
        win = A_REACH * d
        per = ATT_SB // win
        for part in (3, 6):
            cb = part * nh + g * nh
            in_specs.append(pl.BlockSpec((ATT_SB, LANES), lambda sb, h, cb=cb: (sb, cb + h)))
            in_specs.append(pl.BlockSpec(
                (win, LANES), lambda sb, h, cb=cb, per=per: (jnp.maximum(sb * per - 1, 0), cb + h)))
        scratch += [pltpu.VMEM((win + ATT_SB, LANES), F32), pltpu.VMEM((win + ATT_SB, LANES), F32)]
    scratch += [pltpu.VMEM((ATT_SB, LANES), F32)] * 3
    return pl.pallas_call(
        _attn_prompt_kernel,
        grid=(SEQ // ATT_SB, nh),
        in_specs=in_specs,
        out_specs=pl.BlockSpec((ATT_SB, LANES), lambda sb, h: (sb, h)),
        out_shape=jax.ShapeDtypeStruct((SEQ, A_WIDTH), BF16),
        scratch_shapes=scratch,
        compiler_params=_cp(("arbitrary", "arbitrary")),
        name="attn_prompt",
    )(*([u] * 15))


def _attn_sample_kernel(qkv_ref, c1_ref, c2_ref, c3_ref, o_ref):
    scale = A_HEAD_DIM ** -0.5
    pos = lax.broadcasted_iota(jnp.int32, (A_REACH, 1, 1), 0)

    def merge(state, m_t, l_t, num):
        if state is None:
            return m_t, l_t, num
        m_o, l_o, n_o = state
        m_n = jnp.maximum(m_o, m_t)
        a, b = jnp.exp(m_o - m_n), jnp.exp(m_t - m_n)
        return m_n, a * l_o + b * l_t, a * n_o + b * num

    for s in range(DEC_SEQ):
        state = None
        for g in range(3):
            q = qkv_ref[0, s, g]
            if g == 0:
                kt, vt = c1_ref[0, :, 0], c1_ref[0, :, 1]
                valid = pos >= s
                new_rows = range(s + 1)
            else:
                c_ref = c2_ref if g == 1 else c3_ref
                kt, vt = c_ref[0, :, 2 * s], c_ref[0, :, 2 * s + 1]
                valid = None
                new_rows = (s,)
            sc = jnp.sum(kt * q[None], axis=-1, keepdims=True) * scale
            if valid is not None:
                sc = jnp.where(valid, sc, -jnp.inf)
            s_new = [jnp.sum(qkv_ref[0, t, 3 + g] * q, axis=-1, keepdims=True) * scale for t in new_rows]
            m_t = jnp.max(sc, axis=0)
            for sn in s_new:
                m_t = jnp.maximum(m_t, sn)
            p = jnp.exp(sc - m_t[None])
            l_t = jnp.sum(p, axis=0)
            num = jnp.sum(p * vt, axis=0)
            for t, sn in zip(new_rows, s_new):
                pn = jnp.exp(sn - m_t)
                l_t = l_t + pn
                num = num + pn * qkv_ref[0, t, 6 + g]
            state = merge(state, m_t, l_t, num)
        o_ref[0, s] = state[2] / state[1]


def _attn_sample(qkv_s, c1, c2, c3):
    tile = (A_HEADS, A_HEAD_DIM)
    return pl.pallas_call(
        _attn_sample_kernel,
        grid=(DEC_BATCH,),
        in_specs=[
            pl.BlockSpec((1, DEC_SEQ, 9) + tile, lambda b: (b, 0, 0, 0, 0)),
            pl.BlockSpec((1, A_REACH, 2) + tile, lambda b: (b, 0, 0, 0, 0)),
            pl.BlockSpec((1, A_REACH, 2 * DEC_SEQ) + tile, lambda b: (b, 0, 0, 0, 0)),
            pl.BlockSpec((1, A_REACH, 2 * DEC_SEQ) + tile, lambda b: (b, 0, 0, 0, 0)),
        ],
        out_specs=pl.BlockSpec((1, DEC_SEQ) + tile, lambda b: (b, 0, 0, 0)),
        out_shape=jax.ShapeDtypeStruct((DEC_BATCH, DEC_SEQ) + tile, F32),
        compiler_params=_cp(("arbitrary",)),
        name="attn_sample",
    )(qkv_s, c1, c2, c3)


def _mix_kernel(oa_ref, og_ref, wa_ref, wb_ref, ga_ref, gb_ref, o_ref):
    ya = _dot(oa_ref[...], wa_ref[...])
    yb = _dot(og_ref[...], wb_ref[...])
    o_ref[...] = (_sigmoid(ga_ref[...]) * ya + _sigmoid(gb_ref[...]) * yb).astype(BF16)


def _mix(o_a, o_g, wa, wb, gates):
    tm, tn = ROW_TILE, COL_TILE
    nj = D_MODEL // tn
    return pl.pallas_call(
        _mix_kernel,
        grid=(M_ROWS // tm, nj),
        in_specs=[
            pl.BlockSpec((tm, A_WIDTH), lambda i, j: (i, 0)),
            pl.BlockSpec((tm, GLA_V), lambda i, j: (i, 0)),
            pl.BlockSpec((A_WIDTH, tn), lambda i, j: (0, j)),
            pl.BlockSpec((GLA_V, tn), lambda i, j: (0, j)),
            pl.BlockSpec((tm, tn), lambda i, j: (i, j)),
            pl.BlockSpec((tm, tn), lambda i, j: (i, nj + j)),
        ],
        out_specs=pl.BlockSpec((tm, tn), lambda i, j: (i, j)),
        out_shape=jax.ShapeDtypeStruct((M_ROWS, D_MODEL), BF16),
        compiler_params=_cp(("arbitrary", "arbitrary")),
        name="branch_mix",
    )(o_a, o_g, wa, wb, gates, gates)


def _out_ln_kernel(mx_ref, w_ref, xn_ref, g_ref, b_ref, of_ref, ob_ref, r_ref):
    j = pl.program_id(1)
    nj = D_MODEL // COL_TILE
    r_ref[j] = ALPHA * xn_ref[...] + _dot(mx_ref[...], w_ref[...])

    @pl.when(j == nj - 1)
    def _():
        mu = sum(jnp.sum(r_ref[t], axis=-1, keepdims=True) for t in range(nj)) * (1.0 / D_MODEL)
        var = sum(jnp.sum(jnp.square(r_ref[t] - mu), axis=-1, keepdims=True) for t in range(nj)) * (1.0 / D_MODEL)
        inv = lax.rsqrt(var + LN_EPS)
        for t in range(nj):
            cs = slice(t * COL_TILE, (t + 1) * COL_TILE)
            y = (r_ref[t] - mu) * inv * g_ref[:, cs] + b_ref[:, cs]
            of_ref[:, cs] = y
            ob_ref[:, cs] = y.astype(BF16)


def _out_ln(mixed, w_out, xn, g, b):
    tm, tn = ROW_TILE, COL_TILE
    return pl.pallas_call(
        _out_ln_kernel,
        grid=(M_ROWS // tm, D_MODEL // tn),
        in_specs=[
            pl.BlockSpec((tm, D_MODEL), lambda i, j: (i, 0)),
            pl.BlockSpec((D_MODEL, tn), lambda i, j: (0, j)),
            pl.BlockSpec((tm, tn), lambda i, j: (i, j)),
            pl.BlockSpec((1, D_MODEL), lambda i, j: (0, 0)),
            pl.BlockSpec((1, D_MODEL), lambda i, j: (0, 0)),
        ],
        out_specs=[pl.BlockSpec((tm, D_MODEL), lambda i, j: (i, 0)),
                   pl.BlockSpec((tm, D_MODEL), lambda i, j: (i, 0))],
        out_shape=[jax.ShapeDtypeStruct((M_ROWS, D_MODEL), F32),
                   jax.ShapeDtypeStruct((M_ROWS, D_MODEL), BF16)],
        scratch_shapes=[pltpu.VMEM((D_MODEL // tn, tm, tn), F32)],
        compiler_params=_cp(("arbitrary", "arbitrary")),
        name="out_proj_ln1",
    )(mixed, w_out, xn, g.reshape(1, D_MODEL), b.reshape(1, D_MODEL))


def _ple_kernel(xb_ref, wg_ref, p_ref, wp_ref, x_ref, o_ref):
    gate = _sigmoid(_dot(xb_ref[...], wg_ref[...]))
    o_ref[...] = ALPHA * x_ref[...] + gate * _dot(p_ref[...], wp_ref[...])


def _ple(x1b, wpg, p, wple, x1):
    tm, tn = ROW_TILE, COL_TILE
    return pl.pallas_call(
        _ple_kernel,
        grid=(M_ROWS // tm, D_MODEL // tn),
        in_specs=[
            pl.BlockSpec((tm, D_MODEL), lambda i, j: (i, 0)),
            pl.BlockSpec((D_MODEL, tn), lambda i, j: (0, j)),
            pl.BlockSpec((tm, PLE_DIM), lambda i, j: (i, 0)),
            pl.BlockSpec((PLE_DIM, tn), lambda i, j: (0, j)),
            pl.BlockSpec((tm, tn), lambda i, j: (i, j)),
        ],
        out_specs=pl.BlockSpec((tm, tn), lambda i, j: (i, j)),
        out_shape=jax.ShapeDtypeStruct((M_ROWS, D_MODEL), F32),
        compiler_params=_cp(("arbitrary", "arbitrary")),
        name="ple_residual",
    )(x1b, wpg, p, wple, x1)


def _router_kernel(x_ref, w_ref, idx_ref, wt_ref):
    xh, xm, _ = _split3(x_ref[...])
    wh, wm, _ = _split3(w_ref[...])
    logits = _dot(xh, wh) + _dot(xh, wm) + _dot(xm, wh)
    lane_i = lax.broadcasted_iota(jnp.int32, (x_ref.shape[0], LANES), 1)
    lane = lane_i.astype(F32)
    lane_group = (lane_i >> 3).astype(F32)
    neg = -jnp.inf

    def top1(v):
        mx = jnp.max(v, axis=1, keepdims=True)
        ix = jnp.min(jnp.where(v == mx, lane, float(LANES)), axis=1, keepdims=True)
        return mx, ix

    gl = jnp.where(lane_i < MOE_GROUPS, logits[:, :LANES], neg)
    g_max, g_idx = top1(gl)
    g_p = 1.0 / jnp.sum(jnp.exp(gl - g_max), axis=1, keepdims=True)
    el = logits[:, LANES:]
    in_group = jnp.logical_and(lane_i < N_EXPERTS, lane_group == g_idx)
    e1v = jnp.where(in_group, el, neg)
    m1, i1 = top1(e1v)
    m2, i2 = top1(jnp.where(lane == i1, neg, e1v))
    t = jnp.exp(m2 - m1)
    w1 = g_p / (1.0 + t)
    w2 = g_p * t / (1.0 + t)
    idx_ref[...] = jnp.where(lane_i == 0, i1, jnp.where(lane_i == 1, i2, 0.0)).astype(jnp.int32)
    wt_ref[...] = jnp.where(lane_i == 0, w1, jnp.where(lane_i == 1, w2, 0.0))


def _router(x1, w_r1, w_r2):
    w = jnp.zeros((D_MODEL, 2 * LANES), F32)
    w = w.at[:, :MOE_GROUPS].set(w_r1).at[:, LANES:LANES + N_EXPERTS].set(w_r2)
    tm = ROW_TILE
    return pl.pallas_call(
        _router_kernel,
        grid=(M_ROWS // tm,),
        in_specs=[pl.BlockSpec((tm, D_MODEL), lambda i: (i, 0)),
                  pl.BlockSpec((D_MODEL, 2 * LANES), lambda i: (0, 0))],
        out_specs=[pl.BlockSpec((tm, LANES), lambda i: (i, 0)),
                   pl.BlockSpec((tm, LANES), lambda i: (i, 0))],
        out_shape=[jax.ShapeDtypeStruct((M_ROWS, LANES), jnp.int32),
                   jax.ShapeDtypeStruct((M_ROWS, LANES), F32)],
        compiler_params=_cp(("arbitrary",)),
        name="router",
    )(x1, w)


def _expert_changed(be_ref, i):
    return jnp.logical_or(i == 0, be_ref[i] != be_ref[jnp.maximum(i - 1, 0)])


def _cast_rows(dst_ref, src_ref, n_rows, step=256):
    for r in range(0, n_rows, step):
        dst_ref[r:r + step, :] = src_ref[0, r:r + step, :].astype(BF16)


def _moe_up_kernel(be_ref, nv_ref, x_ref, wg_ref, wu_ref, h_ref, wgb, wub):
    i = pl.program_id(0)

    @pl.when(_expert_changed(be_ref, i))
    def _():
        _cast_rows(wgb, wg_ref, D_MODEL)
        _cast_rows(wub, wu_ref, D_MODEL)

    @pl.when(i < nv_ref[0])
    def _():
        x = x_ref[...]
        a = _dot(x, wgb[...])
        h_ref[...] = (a * _sigmoid(a) * _dot(x, wub[...])).astype(BF16)

    @pl.when(i >= nv_ref[0])
    def _():
        h_ref[...] = jnp.zeros_like(h_ref)


def _moe_down_kernel(be_ref, nv_ref, h_ref, wd_ref, y_ref, wdb):
    i = pl.program_id(0)

    @pl.when(_expert_changed(be_ref, i))
    def _():
        _cast_rows(wdb, wd_ref, D_EXPERT)

    @pl.when(i < nv_ref[0])
    def _():
        y_ref[...] = _dot(h_ref[...], wdb[...])

    @pl.when(i >= nv_ref[0])
    def _():
        y_ref[...] = jnp.zeros_like(y_ref)


def _moe_experts(xs, blk_e, n_valid, w_gate, w_up, w_down):
    h = pl.pallas_call(
        _moe_up_kernel,
        grid_spec=pltpu.PrefetchScalarGridSpec(
            num_scalar_prefetch=2,
            grid=(MOE_NBLK,),
            in_specs=[
                pl.BlockSpec((MOE_BLK, D_MODEL), lambda i, be, nv: (i, 0)),
                pl.BlockSpec((1, D_MODEL, D_EXPERT), lambda i, be, nv: (be[i], 0, 0)),
                pl.BlockSpec((1, D_MODEL, D_EXPERT), lambda i, be, nv: (be[i], 0, 0)),
            ],
            out_specs=pl.BlockSpec((MOE_BLK, D_EXPERT), lambda i, be, nv: (i, 0)),
            scratch_shapes=[pltpu.VMEM((D_MODEL, D_EXPERT), BF16), pltpu.VMEM((D_MODEL, D_EXPERT), BF16)],
        ),
        out_shape=jax.ShapeDtypeStruct((MOE_ROWS, D_EXPERT), BF16),
        compiler_params=_cp(("arbitrary",)),
        name="moe_up",
    )(blk_e, n_valid, xs, w_gate, w_up)
    return pl.pallas_call(
        _moe_down_kernel,
        grid_spec=pltpu.PrefetchScalarGridSpec(
            num_scalar_prefetch=2,
            grid=(MOE_NBLK,),
            in_specs=[
                pl.BlockSpec((MOE_BLK, D_EXPERT), lambda i, be, nv: (i, 0)),
                pl.BlockSpec((1, D_EXPERT, D_MODEL), lambda i, be, nv: (be[i], 0, 0)),
            ],
            out_specs=pl.BlockSpec((MOE_BLK, D_MODEL), lambda i, be, nv: (i, 0)),
            scratch_shapes=[pltpu.VMEM((D_EXPERT, D_MODEL), BF16)],
        ),
        out_shape=jax.ShapeDtypeStruct((MOE_ROWS, D_MODEL), F32),
        compiler_params=_cp(("arbitrary",)),
        name="moe_down",
    )(blk_e, n_valid, h, w_down)


def _moe_plan(e_idx):
    flat_e = e_idx.reshape(-1)
    n_asg = flat_e.shape[0]
    order = jnp.argsort(flat_e)
    se = flat_e[order]
    counts = jnp.zeros((N_EXPERTS,), jnp.int32).at[flat_e].add(1)
    starts = jnp.cumsum(counts) - counts
    pcounts = (counts + MOE_BLK - 1) // MOE_BLK * MOE_BLK
    pends = jnp.cumsum(pcounts)
    pstarts = pends - pcounts
    dest_sorted = pstarts[se] + jnp.arange(n_asg, dtype=jnp.int32) - starts[se]
    dest = jnp.zeros((n_asg,), jnp.int32).at[order].set(dest_sorted)
    row_tok = jnp.zeros((MOE_ROWS,), jnp.int32).at[dest_sorted].set((order // 2).astype(jnp.int32))
    n_valid = (pends[-1] // MOE_BLK).astype(jnp.int32)
    blk_start = jnp.arange(MOE_NBLK, dtype=jnp.int32) * MOE_BLK
    blk_e = jnp.searchsorted(pends, jnp.minimum(blk_start, pends[-1] - 1), side='right')
    blk_e = jnp.clip(blk_e, 0, N_EXPERTS - 1).astype(jnp.int32)
    return dest.reshape(-1, 2), row_tok, blk_e, n_valid.reshape(1)


def _final_row_copy(dest_ref, y_hbm, ybuf, sem, blk, slot, k, t):
    src = dest_ref[k * M_ROWS + blk * LN_ROWS + t]
    return pltpu.make_async_copy(y_hbm.at[pl.ds(src, 1)], ybuf.at[slot, pl.ds(k * LN_ROWS + t, 1)], sem.at[slot])


def _final_kernel(dest_ref, base_ref, wt_ref, g_ref, b_ref, y_hbm, op_ref, os_ref, ybuf, sem):
    i = pl.program_id(0)
    n_rows = 2 * LN_ROWS

    def for_rows(blk, slot, action):
        for k in range(2):
            def body(t, carry, k=k):
                action(_final_row_copy(dest_ref, y_hbm, ybuf, sem, blk, slot, k, t))
                return carry
            lax.fori_loop(0, LN_ROWS, body, 0, unroll=8)

    @pl.when(i == 0)
    def _():
        for_rows(0, 0, lambda cp: cp.start())

    @pl.when(i + 1 < pl.num_programs(0))
    def _():
        for_rows(i + 1, (i + 1) & 1, lambda cp: cp.start())

    slot = i & 1
    for_rows(i, slot, lambda cp: cp.wait())

    wt = wt_ref[...]
    r = base_ref[...] + wt[:, 0:1] * ybuf[slot, 0:LN_ROWS, :] + wt[:, 1:2] * ybuf[slot, LN_ROWS:n_rows, :]
    y = _layer_norm_rows(r, g_ref[...], b_ref[...])

    @pl.when(i < SEQ // LN_ROWS)
    def _():
        op_ref[...] = y

    @pl.when(i >= SEQ // LN_ROWS)
    def _():
        os_ref[...] = y


def _final(dest_flat, base, y_rows, wt, g, b):
    npb = SEQ // LN_ROWS
    return pl.pallas_call(
        _final_kernel,
        grid_spec=pltpu.PrefetchScalarGridSpec(
            num_scalar_prefetch=1,
            grid=(npb + 1,),
            in_specs=[
                pl.BlockSpec((LN_ROWS, D_MODEL), lambda i, d: (i, 0)),
                pl.BlockSpec((LN_ROWS, LANES), lambda i, d: (i, 0)),
                pl.BlockSpec((1, D_MODEL), lambda i, d: (0, 0)),
                pl.BlockSpec((1, D_MODEL), lambda i, d: (0, 0)),
                pl.BlockSpec(memory_space=pl.ANY),
            ],
            out_specs=[pl.BlockSpec((LN_ROWS, D_MODEL), lambda i, d: (jnp.minimum(i, npb - 1), 0)),
                       pl.BlockSpec((N_SAMPLE, D_MODEL), lambda i, d: (0, 0))],
            scratch_shapes=[pltpu.VMEM((2, 2 * LN_ROWS, D_MODEL), F32), pltpu.SemaphoreType.DMA((2,))],
        ),
        out_shape=[jax.ShapeDtypeStruct((SEQ, D_MODEL), F32),
                   jax.ShapeDtypeStruct((N_SAMPLE, D_MODEL), F32)],
        compiler_params=_cp(("arbitrary",)),
        name="combine_ln2",
    )(dest_flat, base, wt, g.reshape(1, D_MODEL), b.reshape(1, D_MODEL), y_rows)


CACHE_SPLIT = 4


def _cache_shift_copies(caches, news, outs, sem):
    copies = []
    nb = DEC_BATCH // CACHE_SPLIT
    for c, n, o, w in zip(caches, news, outs, A_WINDOWS):
        for part in range(CACHE_SPLIT):
            bs = pl.ds(part * nb, nb)
            copies.append(pltpu.make_async_copy(
                c.at[0, bs, pl.ds(DEC_SEQ, w - DEC_SEQ)], o.at[0, bs, pl.ds(0, w - DEC_SEQ)], sem.at[len(copies)]))
        copies.append(pltpu.make_async_copy(n, o.at[0, :, pl.ds(w - DEC_SEQ, DEC_SEQ)], sem.at[len(copies)]))
    return copies


def _cache_shift_kernel(c1, c2, c3, n1, n2, n3, o1, o2, o3, sem):
    copies = _cache_shift_copies((c1, c2, c3), (n1, n2, n3), (o1, o2, o3), sem)
    for cp in copies:
        cp.start()
    for cp in copies:
        cp.wait()


def _cache_shift(caches, news):
    any_spec = pl.BlockSpec(memory_space=pl.ANY)
    return pl.pallas_call(
        _cache_shift_kernel,
        in_specs=[any_spec] * 6,
        out_specs=[any_spec] * 3,
        out_shape=[jax.ShapeDtypeStruct(c.shape, c.dtype) for c in caches],
        scratch_shapes=[pltpu.SemaphoreType.DMA((3 * (CACHE_SPLIT + 1),))],
        name="cache_shift",
    )(*caches, *news)


def kernel(x_prompt, x_sample, p_prompt, p_sample, cache_kv_a1, cache_kv_a2, cache_kv_a3, state_gla,
           ln_emb_g, ln_emb_b, ln1_g, ln1_b, ln2_g, ln2_b, w_in, w_gk2, b_gk, gla_norm_g, w_br_a, w_br_b,
           w_out, w_router_group, w_router_expert, w_gate, w_up, w_down, w_ple_gate, w_ple):
    xp = x_prompt.reshape(SEQ, D_MODEL)
    xs = x_sample.reshape(N_SAMPLE, D_MODEL)
    xn, xb = _ln_emb(xp, xs, ln_emb_g, ln_emb_b)

    w_in0 = w_in[0]
    rope = _rope_tables()
    u = _project(xb, w_in0, MAIN_COLS, 2 * 3 * A_WIDTH // COL_TILE, rope)
    gates = _project(xb, w_in0[:, MAIN_COLS + GLA_LR:], 2 * D_MODEL, 0, rope)
    gk = _gla_gate(xb, w_in0[:, MAIN_COLS:MAIN_COLS + GLA_LR], w_gk2[0], b_gk[0])

    o_a = _attn_prompt(u)
    u_s = u[SEQ:]
    qkv_s = u_s[:, :A_QKV_COLS].reshape(DEC_BATCH, DEC_SEQ, 9, A_HEADS, A_HEAD_DIM)
    c1 = cache_kv_a1.reshape(DEC_BATCH, A_REACH, 2, A_HEADS, A_HEAD_DIM)
    c2 = cache_kv_a2.reshape(DEC_BATCH, A_REACH, 4 * 2, A_HEADS, A_HEAD_DIM)
    c3 = cache_kv_a3.reshape(DEC_BATCH, A_REACH, 16 * 2, A_HEADS, A_HEAD_DIM)
    o_as = _attn_sample(qkv_s, c1, c2, c3)
    o_a = jnp.concatenate([o_a, o_as.reshape(N_SAMPLE, A_WIDTH).astype(BF16)], axis=0)

    o_g, st_t = _gla_prompt(u, gk, gla_norm_g[0])

    def spad(a):
        a = a.reshape(DEC_BATCH, DEC_SEQ, a.shape[-1])
        return jnp.pad(a, ((0, 0), (0, GLA_SPAD - DEC_SEQ), (0, 0)))

    c0 = A_QKV_COLS
    o_gs, st_s = _gla_sample(
        spad(u_s[:, c0:c0 + GLA_QK]), spad(u_s[:, c0 + GLA_QK:c0 + 2 * GLA_QK]),
        spad(u_s[:, c0 + 2 * GLA_QK:c0 + 2 * GLA_QK + GLA_V]), spad(u_s[:, c0 + 2 * GLA_QK + GLA_V:]),
        spad(gk[SEQ:]), gla_norm_g[0], state_gla[0])
    o_gs = o_gs[:, :DEC_SEQ].reshape(N_SAMPLE, GLA_V).astype(BF16)
    o_g = jnp.concatenate([o_g, o_gs], axis=0)

    mixed = _mix(o_a, o_g, w_br_a[0].astype(BF16), w_br_b[0].astype(BF16), gates)
    x1, x1b = _out_ln(mixed, w_out[0].astype(BF16), xn, ln1_g[0], ln1_b[0])

    p = jnp.concatenate([p_prompt[0].reshape(SEQ, PLE_DIM), p_sample[0].reshape(N_SAMPLE, PLE_DIM)], axis=0)
    base = _ple(x1b, w_ple_gate[0].astype(BF16), p.astype(BF16), w_ple[0].astype(BF16), x1)
    ridx, rwt = _router(x1, w_router_group[0], w_router_expert[0])
    dest, row_tok, blk_e, n_valid = _moe_plan(ridx[:, :2])
    y_rows = _moe_experts(x1b[row_tok], blk_e, n_valid, w_gate[0], w_up[0], w_down[0])
    y_p, y_s = _final(dest.T.reshape(-1), base, y_rows, rwt, ln2_g[0], ln2_b[0])

    def kv_prompt(g):
        w = A_WINDOWS[g]
        k = u[SEQ - w:SEQ, (3 + g) * A_WIDTH:(4 + g) * A_WIDTH].reshape(w, A_HEADS, A_HEAD_DIM)
        v = u[SEQ - w:SEQ, (6 + g) * A_WIDTH:(7 + g) * A_WIDTH].reshape(w, A_HEADS, A_HEAD_DIM)
        return jnp.stack([k, v], axis=1)[None, None]

    kv_new = [jnp.stack([qkv_s[:, :, 3 + g], qkv_s[:, :, 6 + g]], axis=2) for g in range(3)]
    kv_s1, kv_s2, kv_s3 = _cache_shift((cache_kv_a1, cache_kv_a2, cache_kv_a3), kv_new)

    gla_state_prompt = st_t[None, None]
    return (y_p.reshape(1, SEQ, D_MODEL), y_s.reshape(DEC_BATCH, DEC_SEQ, D_MODEL),
            kv_prompt(0), kv_prompt(1), kv_prompt(2), gla_state_prompt,
            kv_s1, kv_s2, kv_s3, st_s[None])
```

```python
import functools

import jax
import jax.numpy as jnp
from jax import lax
from jax.experimental import pallas as pl
from jax.experimental.pallas import tpu as pltpu

F32 = jnp.float32
BF16 = jnp.bfloat16

D_MODEL = 2048
SEQ = 8192
DEC_BATCH = 32
DEC_SEQ = 4
PAST_LEN = 16384
N_SAMPLE = DEC_BATCH * DEC_SEQ
M_ROWS = SEQ + N_SAMPLE
A_WINDOWS = (128, 512, 2048)
A_DILATIONS = (1, 4, 16)
A_REACH = 128
A_HEADS = 8
A_HEAD_DIM = 128
A_WIDTH = A_HEADS * A_HEAD_DIM
ROT_DIM = 32
ROPE_THETA = 500000.0
A_QKV_COLS = 9 * A_WIDTH
GLA_HEADS = 4
GLA_DK = 256
GLA_DV = 512
GLA_LR = 16
GLA_TAU = 16.0
GLA_CHUNK = 64
GLA_QK = GLA_HEADS * GLA_DK
GLA_V = GLA_HEADS * GLA_DV
MAIN_COLS = A_QKV_COLS + 2 * GLA_QK + 2 * GLA_V
MOE_GROUPS = 4
MOE_PER_GROUP = 8
N_EXPERTS = 32
D_EXPERT = 1024
PLE_DIM = 256
LN_EPS = 1e-5
ALPHA = 2.0 ** 0.25

LANES = 128
VMEM_LIMIT = 56 * 1024 * 1024
ROW_TILE_PROJ = 1664
ROW_TILE = 832
COL_TILE = 512
LN_ROWS = 128
MOE_BLK = 256
MOE_NBLK = (2 * M_ROWS) // MOE_BLK + N_EXPERTS
MOE_ROWS = MOE_NBLK * MOE_BLK
ATT_SB = 2048
GLA_ROWS = 512
ATT_UNROLL = 4
GLA_UNROLL = 4


def _cp(sem, vmem=VMEM_LIMIT):
    return pltpu.CompilerParams(dimension_semantics=sem, vmem_limit_bytes=vmem)


def _sigmoid(x):
    return 1.0 / (1.0 + jnp.exp(-x))


def _dot(a, b):
    return jnp.dot(a, b, preferred_element_type=F32)


def _dot_nt(a, b):
    return lax.dot_general(a, b, (((1,), (1,)), ((), ())), preferred_element_type=F32)


def _split3(x):
    h = x.astype(BF16)
    r = x - h.astype(F32)
    m = r.astype(BF16)
    l = (r - m.astype(F32)).astype(BF16)
    return h, m, l


def _layer_norm_rows(x, g, b):
    mu = jnp.mean(x, axis=-1, keepdims=True)
    xc = x - mu
    var = jnp.mean(xc * xc, axis=-1, keepdims=True)
    return xc * lax.rsqrt(var + LN_EPS) * g + b


def _ln_emb_kernel(xp_ref, xs_ref, g_ref, b_ref, of_ref, ob_ref):
    i = pl.program_id(0)
    x = jnp.where(i < SEQ // LN_ROWS, xp_ref[...], xs_ref[...])
    y = _layer_norm_rows(x, g_ref[...], b_ref[...])
    of_ref[...] = y
    ob_ref[...] = y.astype(BF16)


def _ln_emb(xp, xs, g, b):
    npb = SEQ // LN_ROWS
    return pl.pallas_call(
        _ln_emb_kernel,
        grid=(npb + 1,),
        in_specs=[
            pl.BlockSpec((LN_ROWS, D_MODEL), lambda i: (jnp.minimum(i, npb - 1), 0)),
            pl.BlockSpec((N_SAMPLE, D_MODEL), lambda i: (0, 0)),
            pl.BlockSpec((1, D_MODEL), lambda i: (0, 0)),
            pl.BlockSpec((1, D_MODEL), lambda i: (0, 0)),
        ],
        out_specs=[
            pl.BlockSpec((LN_ROWS, D_MODEL), lambda i: (i, 0)),
            pl.BlockSpec((LN_ROWS, D_MODEL), lambda i: (i, 0)),
        ],
        out_shape=[jax.ShapeDtypeStruct((M_ROWS, D_MODEL), F32),
                   jax.ShapeDtypeStruct((M_ROWS, D_MODEL), BF16)],
        compiler_params=_cp(("arbitrary",)),
        name="ln_emb",
    )(xp, xs, g.reshape(1, D_MODEL), b.reshape(1, D_MODEL))


def _proj_kernel(x_ref, w_ref, c_ref, s1_ref, s2_ref, o_ref, *, n_rope_blocks):
    j = pl.program_id(1)
    acc = _dot(x_ref[...], w_ref[...].astype(BF16))

    @pl.when(j >= n_rope_blocks)
    def _():
        o_ref[...] = acc

    @pl.when(j < n_rope_blocks)
    def _():
        c, s1, s2 = c_ref[...], s1_ref[...], s2_ref[...]
        for t in range(COL_TILE // LANES):
            a = acc[:, t * LANES:(t + 1) * LANES]
            o_ref[:, t * LANES:(t + 1) * LANES] = (
                a * c + pltpu.roll(a, LANES - ROT_DIM // 2, 1) * s1 + pltpu.roll(a, ROT_DIM // 2, 1) * s2)


def _project(xb, w, n_cols, n_rope_blocks, rope):
    c, s1, s2 = rope
    tm, tn = ROW_TILE_PROJ, COL_TILE
    return pl.pallas_call(
        functools.partial(_proj_kernel, n_rope_blocks=n_rope_blocks),
        grid=(M_ROWS // tm, n_cols // tn),
        in_specs=[
            pl.BlockSpec((tm, D_MODEL), lambda i, j: (i, 0)),
            pl.BlockSpec((D_MODEL, tn), lambda i, j: (0, j)),
            pl.BlockSpec((tm, LANES), lambda i, j: (i, 0)),
            pl.BlockSpec((tm, LANES), lambda i, j: (i, 0)),
            pl.BlockSpec((tm, LANES), lambda i, j: (i, 0)),
        ],
        out_specs=pl.BlockSpec((tm, tn), lambda i, j: (i, j)),
        out_shape=jax.ShapeDtypeStruct((M_ROWS, n_cols), F32),
        compiler_params=_cp(("arbitrary", "arbitrary")),
        name="in_proj",
    )(xb, w, c, s1, s2)


def _rope_tables():
    half = ROT_DIM // 2
    inv = ROPE_THETA ** (-jnp.arange(half, dtype=F32) * (2.0 / ROT_DIM))
    pos = jnp.concatenate([
        jnp.arange(SEQ, dtype=jnp.int32),
        jnp.tile(PAST_LEN + jnp.arange(DEC_SEQ, dtype=jnp.int32), DEC_BATCH)])
    ang = pos.astype(F32)[:, None] * inv[None, :]
    cos, sin = jnp.cos(ang), jnp.sin(ang)
    ones = jnp.ones((M_ROWS, LANES - ROT_DIM), F32)
    zeros = jnp.zeros((M_ROWS, LANES - ROT_DIM), F32)
    zh = jnp.zeros((M_ROWS, half), F32)
    c = jnp.concatenate([cos, cos, ones], axis=1)
    s1 = jnp.concatenate([-sin, zh, zeros], axis=1)
    s2 = jnp.concatenate([zh, sin, zeros], axis=1)
    return c, s1, s2


def _gk_kernel(x_ref, wlr_ref, wgk_ref, b_ref, o_ref):
    blr = _dot(x_ref[...], wlr_ref[...])
    z = _dot(blr.astype(BF16), wgk_ref[...]) + b_ref[...]
    o_ref[...] = (jnp.minimum(z, 0.0) - jnp.log1p(jnp.exp(-jnp.abs(z)))) * (1.0 / GLA_TAU)


def _gla_gate(xb, w_lr, w_gk2, b_gk):
    wlr = jnp.zeros((D_MODEL, LANES), BF16).at[:, :GLA_LR].set(w_lr.astype(BF16))
    wgk = jnp.zeros((LANES, GLA_QK), BF16).at[:GLA_LR].set(w_gk2.astype(BF16))
    tm = ROW_TILE
    return pl.pallas_call(
        _gk_kernel,
        grid=(M_ROWS // tm,),
        in_specs=[
            pl.BlockSpec((tm, D_MODEL), lambda i: (i, 0)),
            pl.BlockSpec((D_MODEL, LANES), lambda i: (0, 0)),
            pl.BlockSpec((LANES, GLA_QK), lambda i: (0, 0)),
            pl.BlockSpec((1, GLA_QK), lambda i: (0, 0)),
        ],
        out_specs=pl.BlockSpec((tm, GLA_QK), lambda i: (i, 0)),
        out_shape=jax.ShapeDtypeStruct((M_ROWS, GLA_QK), F32),
        compiler_params=_cp(("arbitrary",)),
        name="gla_gate",
    )(xb, wlr, wgk, b_gk.reshape(1, GLA_QK))


def _gla_chunk(q, k, v, g, br, gn, s0, causal, tri_b, eye):
    n = g.shape[0]
    g1, g2, g3 = _split3(g)
    b = _dot(tri_b, g1) + _dot(tri_b, g2) + _dot(tri_b, g3)
    b_last = b[n - 1:n, :]
    q_dec = (q * ((GLA_DK ** -0.5) * jnp.exp(b))).astype(BF16)
    k_inv = (k * jnp.exp(-b)).astype(BF16)
    k_dec = k * jnp.exp(b_last - b)
    vb = v.astype(BF16)
    att = jnp.where(causal, _dot_nt(q_dec, k_inv), 0.0)
    o = _dot(att.astype(BF16), vb) + _dot(q_dec, s0.astype(BF16))
    d1, d2, d3 = _split3(jnp.exp(b_last))
    rid = lax.broadcasted_iota(jnp.int32, (n + GLA_TPAD, GLA_DK), 0)
    rows = jnp.concatenate([k_dec, jnp.zeros((GLA_TPAD, GLA_DK), F32)], axis=0)
    rows = jnp.where(rid == n, d1.astype(F32), rows)
    rows = jnp.where(rid == n + 1, d2.astype(F32), rows)
    rows = jnp.where(rid == n + 2, d3.astype(F32), rows)
    cols = _dot_nt(eye, rows.astype(BF16))
    dcol = cols[:, n:n + 1] + cols[:, n + 1:n + 2] + cols[:, n + 2:n + 3]
    s1 = s0 * dcol + _dot(cols[:, :n].astype(BF16), vb)
    ms = jnp.mean(o * o, axis=-1, keepdims=True)
    return o * lax.rsqrt(ms + LN_EPS) * gn * (br * _sigmoid(br)), s1


GLA_TPAD = 16


def _gla_consts(n):
    r = lax.broadcasted_iota(jnp.int32, (n, n), 0)
    c = lax.broadcasted_iota(jnp.int32, (n, n), 1)
    causal = r >= c
    er = lax.broadcasted_iota(jnp.int32, (GLA_DK, GLA_DK), 0)
    ec = lax.broadcasted_iota(jnp.int32, (GLA_DK, GLA_DK), 1)
    return causal, jnp.where(causal, 1.0, 0.0).astype(BF16), jnp.where(er == ec, 1.0, 0.0).astype(BF16)


def _gla_prompt_kernel(q_ref, k_ref, v_ref, br_ref, g_ref, gn_ref, o_ref, st_ref):
    @pl.when(pl.program_id(1) == 0)
    def _():
        st_ref[...] = jnp.zeros_like(st_ref)

    causal, tri_b, eye = _gla_consts(GLA_CHUNK)
    gn = gn_ref[...]

    def body(cc, carry):
        rows = pl.ds(pl.multiple_of(cc * GLA_CHUNK, GLA_CHUNK), GLA_CHUNK)
        o, s1 = _gla_chunk(q_ref[rows, :], k_ref[rows, :], v_ref[rows, :], g_ref[rows, :],
                           br_ref[rows, :], gn, st_ref[0], causal, tri_b, eye)
        st_ref[0] = s1
        o_ref[rows, :] = o.astype(BF16)
        return carry

    lax.fori_loop(0, GLA_ROWS // GLA_CHUNK, body, 0, unroll=GLA_UNROLL)


def _gla_prompt(u, gk, gnorm):
    qb0 = A_QKV_COLS // GLA_DK
    kb0 = (A_QKV_COLS + GLA_QK) // GLA_DK
    vb0 = (A_QKV_COLS + 2 * GLA_QK) // GLA_DV
    rb0 = (A_QKV_COLS + 2 * GLA_QK + GLA_V) // GLA_DV
    return pl.pallas_call(
        _gla_prompt_kernel,
        grid=(GLA_HEADS, SEQ // GLA_ROWS),
        in_specs=[
            pl.BlockSpec((GLA_ROWS, GLA_DK), lambda h, c: (c, qb0 + h)),
            pl.BlockSpec((GLA_ROWS, GLA_DK), lambda h, c: (c, kb0 + h)),
            pl.BlockSpec((GLA_ROWS, GLA_DV), lambda h, c: (c, vb0 + h)),
            pl.BlockSpec((GLA_ROWS, GLA_DV), lambda h, c: (c, rb0 + h)),
            pl.BlockSpec((GLA_ROWS, GLA_DK), lambda h, c: (c, h)),
            pl.BlockSpec((1, GLA_DV), lambda h, c: (0, 0)),
        ],
        out_specs=[
            pl.BlockSpec((GLA_ROWS, GLA_DV), lambda h, c: (c, h)),
            pl.BlockSpec((1, GLA_DK, GLA_DV), lambda h, c: (h, 0, 0)),
        ],
        out_shape=[jax.ShapeDtypeStruct((SEQ, GLA_V), BF16),
                   jax.ShapeDtypeStruct((GLA_HEADS, GLA_DK, GLA_DV), F32)],
        compiler_params=_cp(("arbitrary", "arbitrary")),
        name="gla_prompt",
    )(u, u, u, u, gk, gnorm.reshape(1, GLA_DV))


GLA_SPAD = 16


def _gla_sample_kernel(q_ref, k_ref, v_ref, br_ref, g_ref, gn_ref, s0_ref, o_ref, s1_ref):
    causal, tri_b, eye = _gla_consts(GLA_SPAD)
    gn = gn_ref[...]
    for h in range(GLA_HEADS):
        ks = slice(h * GLA_DK, (h + 1) * GLA_DK)
        vs = slice(h * GLA_DV, (h + 1) * GLA_DV)
        o, s1 = _gla_chunk(q_ref[0, :, ks], k_ref[0, :, ks], v_ref[0, :, vs], g_ref[0, :, ks],
                           br_ref[0, :, vs], gn, s0_ref[0, h], causal, tri_b, eye)
        s1_ref[0, h] = s1
        o_ref[0, :, vs] = o


def _gla_sample(q, k, v, br, g, gnorm, s0):
    def spec(c):
        return pl.BlockSpec((1, GLA_SPAD, c), lambda b: (b, 0, 0))
    st_spec = pl.BlockSpec((1, GLA_HEADS, GLA_DK, GLA_DV), lambda b: (b, 0, 0, 0))
    return pl.pallas_call(
        _gla_sample_kernel,
        grid=(DEC_BATCH,),
        in_specs=[spec(GLA_QK), spec(GLA_QK), spec(GLA_V), spec(GLA_V), spec(GLA_QK),
                  pl.BlockSpec((1, GLA_DV), lambda b: (0, 0)), st_spec],
        out_specs=[spec(GLA_V), st_spec],
        out_shape=[jax.ShapeDtypeStruct((DEC_BATCH, GLA_SPAD, GLA_V), F32),
                   jax.ShapeDtypeStruct((DEC_BATCH, GLA_HEADS, GLA_DK, GLA_DV), F32)],
        compiler_params=_cp(("arbitrary",)),
        name="gla_sample",
    )(q, k, v, br, g, gnorm.reshape(1, GLA_DV), s0)


def _attn_prompt_kernel(q0, q1, q2, kc0, kp0, vc0, vp0, kc1, kp1, vc1, vp1, kc2, kp2, vc2, vp2,
                        o_ref, kb0, vb0, kb1, vb1, kb2, vb2, acc_ref, m_ref, l_ref):
    sb = pl.program_id(0)
    groups = ((q0, kc0, kp0, vc0, vp0, kb0, vb0, 1),
              (q1, kc1, kp1, vc1, vp1, kb1, vb1, 4),
              (q2, kc2, kp2, vc2, vp2, kb2, vb2, 16))
    scale = A_HEAD_DIM ** -0.5
    n_win = 2 * A_REACH
    row = lax.broadcasted_iota(jnp.int32, (A_REACH, n_win), 0)
    col = lax.broadcasted_iota(jnp.int32, (A_REACH, n_win), 1)
    in_reach = jnp.logical_and(col >= row, col <= row + A_REACH)
    in_prev = col < A_REACH
    ones = jnp.ones((n_win, LANES), BF16)

    for gi, (q_ref, kc, kp, vc, vp, kb, vb, d) in enumerate(groups):
        win = A_REACH * d
        kb[0:win, :] = kp[...]
        kb[win:win + ATT_SB, :] = kc[...]
        vb[0:win, :] = vp[...]
        vb[win:win + ATT_SB, :] = vc[...]
        shift = {1: 0, 4: 2, 16: 4}[d]

        def tile(idx, carry, q_ref=q_ref, kb=kb, vb=vb, d=d, win=win, shift=shift, gi=gi):
            r = idx & (d - 1)
            nb = idx >> shift
            start = r + nb * win
            if d == 1:
                start = pl.multiple_of(start, A_REACH)
                qrows = pl.ds(start, A_REACH)
                wrows = pl.ds(start, n_win)
            else:
                qrows = pl.ds(start, A_REACH, stride=d)
                wrows = pl.ds(start, n_win, stride=d)
            q = q_ref[qrows, :].astype(BF16)
            s = _dot_nt(q, kb[wrows, :].astype(BF16)) * scale
            prev_bias = jnp.where(jnp.logical_or(sb > 0, nb > 0), 0.0, -jnp.inf)
            s = jnp.where(in_reach, s, -jnp.inf) + jnp.where(in_prev, prev_bias, 0.0)
            m_t = jnp.max(s, axis=1, keepdims=True)
            p = jnp.exp(s - m_t).astype(BF16)
            l_t = _dot(p, ones)
            num = _dot(p, vb[wrows, :].astype(BF16))
            m_t = jnp.broadcast_to(m_t, (A_REACH, LANES))
            if gi == 0:
                acc_ref[qrows, :] = num
                m_ref[qrows, :] = m_t
                l_ref[qrows, :] = l_t
            else:
                m_o = m_ref[qrows, :]
                m_n = jnp.maximum(m_o, m_t)
                a = jnp.exp(m_o - m_n)
                b = jnp.exp(m_t - m_n)
                acc_ref[qrows, :] = a * acc_ref[qrows, :] + b * num
                l_ref[qrows, :] = a * l_ref[qrows, :] + b * l_t
                m_ref[qrows, :] = m_n
            return carry

        lax.fori_loop(0, ATT_SB // A_REACH, tile, 0, unroll=ATT_UNROLL)

    o_ref[...] = (acc_ref[...] / l_ref[...]).astype(BF16)


def _attn_prompt(u):
    nh = A_HEADS
    in_specs, scratch = [], []
    for g in range(3):
        in_specs.append(pl.BlockSpec((ATT_SB, LANES), lambda sb, h, g=g: (sb, g * nh + h)))
    for g, d in enumerate(A_DILATIONS):
        win = A_REACH * d
        per = ATT_SB // win
        for part in (3, 6):
            cb = part * nh + g * nh
            in_specs.append(pl.BlockSpec((ATT_SB, LANES), lambda sb, h, cb=cb: (sb, cb + h)))
            in_specs.append(pl.BlockSpec(
                (win, LANES), lambda sb, h, cb=cb, per=per: (jnp.maximum(sb * per - 1, 0), cb + h)))
        scratch += [pltpu.VMEM((win + ATT_SB, LANES), F32), pltpu.VMEM((win + ATT_SB, LANES), F32)]
    scratch += [pltpu.VMEM((ATT_SB, LANES), F32)] * 3
    return pl.pallas_call(
        _attn_prompt_kernel,
        grid=(SEQ // ATT_SB, nh),
        in_specs=in_specs,
        out_specs=pl.BlockSpec((ATT_SB, LANES), lambda sb, h: (sb, h)),
        out_shape=jax.ShapeDtypeStruct((SEQ, A_WIDTH), BF16),
        scratch_shapes=scratch,
        compiler_params=_cp(("arbitrary", "arbitrary")),
        name="attn_prompt",
    )(*([u] * 15))


def _attn_sample_kernel(qkv_ref, c1_ref, c2_ref, c3_ref, o_ref):
    scale = A_HEAD_DIM ** -0.5
    pos = lax.broadcasted_iota(jnp.int32, (A_REACH, 1, 1), 0)

    def merge(state, m_t, l_t, num):
        if state is None:
            return m_t, l_t, num
        m_o, l_o, n_o = state
        m_n = jnp.maximum(m_o, m_t)
        a, b = jnp.exp(m_o - m_n), jnp.exp(m_t - m_n)
        return m_n, a * l_o + b * l_t, a * n_o + b * num

    for s in range(DEC_SEQ):
        state = None
        for g in range(3):
            q = qkv_ref[0, s, g]
            if g == 0:
                kt, vt = c1_ref[0, :, 0], c1_ref[0, :, 1]
                valid = pos >= s
                new_rows = range(s + 1)
            else:
                c_ref = c2_ref if g == 1 else c3_ref
                kt, vt = c_ref[0, :, 2 * s], c_ref[0, :, 2 * s + 1]
                valid = None
                new_rows = (s,)
            sc = jnp.sum(kt * q[None], axis=-1, keepdims=True) * scale
            if valid is not None:
                sc = jnp.where(valid, sc, -jnp.inf)
            s_new = [jnp.sum(qkv_ref[0, t, 3 + g] * q, axis=-1, keepdims=True) * scale for t in new_rows]
            m_t = jnp.max(sc, axis=0)
            for sn in s_new:
                m_t = jnp.maximum(m_t, sn)
            p = jnp.exp(sc - m_t[None])
            l_t = jnp.sum(p, axis=0)
            num = jnp.sum(p * vt, axis=0)
            for t, sn in zip(new_rows, s_new):
                pn = jnp.exp(sn - m_t)
                l_t = l_t + pn
                num = num + pn * qkv_ref[0, t, 6 + g]
            state = merge(state, m_t, l_t, num)
        o_ref[0, s] = state[2] / state[1]


def _attn_sample(qkv_s, c1, c2, c3):
    tile = (A_HEADS, A_HEAD_DIM)
    return pl.pallas_call(
        _attn_sample_kernel,
        grid=(DEC_BATCH,),
        in_specs=[
            pl.BlockSpec((1, DEC_SEQ, 9) + tile, lambda b: (b, 0, 0, 0, 0)),
            pl.BlockSpec((1, A_REACH, 2) + tile, lambda b: (b, 0, 0, 0, 0)),
            pl.BlockSpec((1, A_REACH, 2 * DEC_SEQ) + tile, lambda b: (b, 0, 0, 0, 0)),
            pl.BlockSpec((1, A_REACH, 2 * DEC_SEQ) + tile, lambda b: (b, 0, 0, 0, 0)),
        ],
        out_specs=pl.BlockSpec((1, DEC_SEQ) + tile, lambda b: (b, 0, 0, 0)),
        out_shape=jax.ShapeDtypeStruct((DEC_BATCH, DEC_SEQ) + tile, F32),
        compiler_params=_cp(("arbitrary",)),
        name="attn_sample",
    )(qkv_s, c1, c2, c3)


def _mix_kernel(oa_ref, og_ref, wa_ref, wb_ref, ga_ref, gb_ref, o_ref):
    ya = _dot(oa_ref[...], wa_ref[...])
    yb = _dot(og_ref[...], wb_ref[...])
    o_ref[...] = (_sigmoid(ga_ref[...]) * ya + _sigmoid(gb_ref[...]) * yb).astype(BF16)


def _mix(o_a, o_g, wa, wb, gates):
    tm, tn = ROW_TILE, COL_TILE
    nj = D_MODEL // tn
    return pl.pallas_call(
        _mix_kernel,
        grid=(M_ROWS // tm, nj),
        in_specs=[
            pl.BlockSpec((tm, A_WIDTH), lambda i, j: (i, 0)),
            pl.BlockSpec((tm, GLA_V), lambda i, j: (i, 0)),
            pl.BlockSpec((A_WIDTH, tn), lambda i, j: (0, j)),
            pl.BlockSpec((GLA_V, tn), lambda i, j: (0, j)),
            pl.BlockSpec((tm, tn), lambda i, j: (i, j)),
            pl.BlockSpec((tm, tn), lambda i, j: (i, nj + j)),
        ],
        out_specs=pl.BlockSpec((tm, tn), lambda i, j: (i, j)),
        out_shape=jax.ShapeDtypeStruct((M_ROWS, D_MODEL), BF16),
        compiler_params=_cp(("arbitrary", "arbitrary")),
        name="branch_mix",
    )(o_a, o_g, wa, wb, gates, gates)


def _out_ln_kernel(mx_ref, w_ref, xn_ref, g_ref, b_ref, of_ref, ob_ref, r_ref):
    j = pl.program_id(1)
    nj = D_MODEL // COL_TILE
    r_ref[j] = ALPHA * xn_ref[...] + _dot(mx_ref[...], w_ref[...])

    @pl.when(j == nj - 1)
    def _():
        mu = sum(jnp.sum(r_ref[t], axis=-1, keepdims=True) for t in range(nj)) * (1.0 / D_MODEL)
        var = sum(jnp.sum(jnp.square(r_ref[t] - mu), axis=-1, keepdims=True) for t in range(nj)) * (1.0 / D_MODEL)
        inv = lax.rsqrt(var + LN_EPS)
        for t in range(nj):
            cs = slice(t * COL_TILE, (t + 1) * COL_TILE)
            y = (r_ref[t] - mu) * inv * g_ref[:, cs] + b_ref[:, cs]
            of_ref[:, cs] = y
            ob_ref[:, cs] = y.astype(BF16)


def _out_ln(mixed, w_out, xn, g, b):
    tm, tn = ROW_TILE, COL_TILE
    return pl.pallas_call(
        _out_ln_kernel,
        grid=(M_ROWS // tm, D_MODEL // tn),
        in_specs=[
            pl.BlockSpec((tm, D_MODEL), lambda i, j: (i, 0)),
            pl.BlockSpec((D_MODEL, tn), lambda i, j: (0, j)),
            pl.BlockSpec((tm, tn), lambda i, j: (i, j)),
            pl.BlockSpec((1, D_MODEL), lambda i, j: (0, 0)),
            pl.BlockSpec((1, D_MODEL), lambda i, j: (0, 0)),
        ],
        out_specs=[pl.BlockSpec((tm, D_MODEL), lambda i, j: (i, 0)),
                   pl.BlockSpec((tm, D_MODEL), lambda i, j: (i, 0))],
        out_shape=[jax.ShapeDtypeStruct((M_ROWS, D_MODEL), F32),
                   jax.ShapeDtypeStruct((M_ROWS, D_MODEL), BF16)],
        scratch_shapes=[pltpu.VMEM((D_MODEL // tn, tm, tn), F32)],
        compiler_params=_cp(("arbitrary", "arbitrary")),
        name="out_proj_ln1",
    )(mixed, w_out, xn, g.reshape(1, D_MODEL), b.reshape(1, D_MODEL))


def _ple_kernel(xb_ref, wg_ref, p_ref, wp_ref, x_ref, o_ref):
    gate = _sigmoid(_dot(xb_ref[...], wg_ref[...]))
    o_ref[...] = ALPHA * x_ref[...] + gate * _dot(p_ref[...], wp_ref[...])


def _ple(x1b, wpg, p, wple, x1):
    tm, tn = ROW_TILE, COL_TILE
    return pl.pallas_call(
        _ple_kernel,
        grid=(M_ROWS // tm, D_MODEL // tn),
        in_specs=[
            pl.BlockSpec((tm, D_MODEL), lambda i, j: (i, 0)),
            pl.BlockSpec((D_MODEL, tn), lambda i, j: (0, j)),
            pl.BlockSpec((tm, PLE_DIM), lambda i, j: (i, 0)),
            pl.BlockSpec((PLE_DIM, tn), lambda i, j: (0, j)),
            pl.BlockSpec((tm, tn), lambda i, j: (i, j)),
        ],
        out_specs=pl.BlockSpec((tm, tn), lambda i, j: (i, j)),
        out_shape=jax.ShapeDtypeStruct((M_ROWS, D_MODEL), F32),
        compiler_params=_cp(("arbitrary", "arbitrary")),
        name="ple_residual",
    )(x1b, wpg, p, wple, x1)


def _router_kernel(x_ref, w_ref, idx_ref, wt_ref):
    xh, xm, _ = _split3(x_ref[...])
    wh, wm, _ = _split3(w_ref[...])
    logits = _dot(xh, wh) + _dot(xh, wm) + _dot(xm, wh)
    lane_i = lax.broadcasted_iota(jnp.int32, (x_ref.shape[0], LANES), 1)
    lane = lane_i.astype(F32)
    lane_group = (lane_i >> 3).astype(F32)
    neg = -jnp.inf

    def top1(v):
        mx = jnp.max(v, axis=1, keepdims=True)
        ix = jnp.min(jnp.where(v == mx, lane, float(LANES)), axis=1, keepdims=True)
        return mx, ix

    gl = jnp.where(lane_i < MOE_GROUPS, logits[:, :LANES], neg)
    g_max, g_idx = top1(gl)
    g_p = 1.0 / jnp.sum(jnp.exp(gl - g_max), axis=1, keepdims=True)
    el = logits[:, LANES:]
    in_group = jnp.logical_and(lane_i < N_EXPERTS, lane_group == g_idx)
    e1v = jnp.where(in_group, el, neg)
    m1, i1 = top1(e1v)
    m2, i2 = top1(jnp.where(lane == i1, neg, e1v))
    t = jnp.exp(m2 - m1)
    w1 = g_p / (1.0 + t)
    w2 = g_p * t / (1.0 + t)
    idx_ref[...] = jnp.where(lane_i == 0, i1, jnp.where(lane_i == 1, i2, 0.0)).astype(jnp.int32)
    wt_ref[...] = jnp.where(lane_i == 0, w1, jnp.where(lane_i == 1, w2, 0.0))


def _router(x1, w_r1, w_r2):
    w = jnp.zeros((D_MODEL, 2 * LANES), F32)
    w = w.at[:, :MOE_GROUPS].set(w_r1).at[:, LANES:LANES + N_EXPERTS].set(w_r2)
    tm = ROW_TILE
    return pl.pallas_call(
        _router_kernel,
        grid=(M_ROWS // tm,),
        in_specs=[pl.BlockSpec((tm, D_MODEL), lambda i: (i, 0)),
                  pl.BlockSpec((D_MODEL, 2 * LANES), lambda i: (0, 0))],
        out_specs=[pl.BlockSpec((tm, LANES), lambda i: (i, 0)),
                   pl.BlockSpec((tm, LANES), lambda i: (i, 0))],
        out_shape=[jax.ShapeDtypeStruct((M_ROWS, LANES), jnp.int32),
                   jax.ShapeDtypeStruct((M_ROWS, LANES), F32)],
        compiler_params=_cp(("arbitrary",)),
        name="router",
    )(x1, w)


def _moe_kernel(be_ref, nx_ref, nv_ref, tok_ref, x_hbm, wg_hbm, wu_hbm, wd_hbm, y_ref,
                sg, su, sd, wgb, wub, wdb, xbuf, wsem, xsem):
    i = pl.program_id(0)
    nv = nv_ref[0]
    e = be_ref[i]

    def weight_copies(ex):
        return (pltpu.make_async_copy(wg_hbm.at[ex], sg, wsem.at[0]),
                pltpu.make_async_copy(wu_hbm.at[ex], su, wsem.at[1]),
                pltpu.make_async_copy(wd_hbm.at[ex], sd, wsem.at[2]))

    def for_rows(blk, slot, action):
        def body(r, carry):
            src = tok_ref[blk * MOE_BLK + r]
            action(pltpu.make_async_copy(x_hbm.at[pl.ds(src, 1)], xbuf.at[slot, pl.ds(r, 1)], xsem.at[slot]))
            return carry
        lax.fori_loop(0, MOE_BLK, body, 0, unroll=8)

    @pl.when(i == 0)
    def _():
        for cp in weight_copies(e):
            cp.start()
        for_rows(0, 0, lambda cp: cp.start())

    @pl.when(i + 1 < nv)
    def _():
        for_rows(i + 1, (i + 1) & 1, lambda cp: cp.start())

    first_of_expert = jnp.logical_or(i == 0, e != be_ref[jnp.maximum(i - 1, 0)])

    @pl.when(jnp.logical_and(i < nv, first_of_expert))
    def _():
        for cp in weight_copies(e):
            cp.wait()
        for r in range(0, D_MODEL, MOE_BLK):
            wgb[r:r + MOE_BLK, :] = sg[r:r + MOE_BLK, :].astype(BF16)
            wub[r:r + MOE_BLK, :] = su[r:r + MOE_BLK, :].astype(BF16)
        for r in range(0, D_EXPERT, MOE_BLK):
            wdb[r:r + MOE_BLK, :] = sd[r:r + MOE_BLK, :].astype(BF16)

        @pl.when(nx_ref[i] >= 0)
        def _():
            for cp in weight_copies(nx_ref[i]):
                cp.start()

    @pl.when(i < nv)
    def _():
        slot = i & 1
        for_rows(i, slot, lambda cp: cp.wait())
        x = xbuf[slot].astype(BF16)
        a = _dot(x, wgb[...])
        h = (a * _sigmoid(a) * _dot(x, wub[...])).astype(BF16)
        y_ref[...] = _dot(h, wdb[...])

    @pl.when(i >= nv)
    def _():
        y_ref[...] = jnp.zeros_like(y_ref)


def _moe_experts(x1, row_tok, blk_e, blk_next, n_valid, w_gate, w_up, w_down):
    any_spec = pl.BlockSpec(memory_space=pl.ANY)
    return pl.pallas_call(
        _moe_kernel,
        grid_spec=pltpu.PrefetchScalarGridSpec(
            num_scalar_prefetch=4,
            grid=(MOE_NBLK,),
            in_specs=[any_spec, any_spec, any_spec, any_spec],
            out_specs=pl.BlockSpec((MOE_BLK, D_MODEL), lambda i, be, nx, nv, tok: (i, 0)),
            scratch_shapes=[
                pltpu.VMEM((D_MODEL, D_EXPERT), F32), pltpu.VMEM((D_MODEL, D_EXPERT), F32),
                pltpu.VMEM((D_EXPERT, D_MODEL), F32),
                pltpu.VMEM((D_MODEL, D_EXPERT), BF16), pltpu.VMEM((D_MODEL, D_EXPERT), BF16),
                pltpu.VMEM((D_EXPERT, D_MODEL), BF16),
                pltpu.VMEM((2, MOE_BLK, D_MODEL), F32),
                pltpu.SemaphoreType.DMA((3,)), pltpu.SemaphoreType.DMA((2,)),
            ],
        ),
        out_shape=jax.ShapeDtypeStruct((MOE_ROWS, D_MODEL), F32),
        compiler_params=_cp(("arbitrary",)),
        name="moe_experts",
    )(blk_e, blk_next, n_valid, row_tok, x1, w_gate, w_up, w_down)


def _moe_plan(e_idx):
    flat_e = e_idx.reshape(-1)
    n_asg = flat_e.shape[0]
    order = jnp.argsort(flat_e)
    se = flat_e[order]
    counts = jnp.zeros((N_EXPERTS,), jnp.int32).at[flat_e].add(1)
    starts = jnp.cumsum(counts) - counts
    pcounts = (counts + MOE_BLK - 1) // MOE_BLK * MOE_BLK
    pends = jnp.cumsum(pcounts)
    pstarts = pends - pcounts
    dest_sorted = pstarts[se] + jnp.arange(n_asg, dtype=jnp.int32) - starts[se]
    dest = jnp.zeros((n_asg,), jnp.int32).at[order].set(dest_sorted)
    row_tok = jnp.zeros((MOE_ROWS,), jnp.int32).at[dest_sorted].set((order // 2).astype(jnp.int32))
    n_valid = (pends[-1] // MOE_BLK).astype(jnp.int32)
    blk_start = jnp.arange(MOE_NBLK, dtype=jnp.int32) * MOE_BLK
    blk_e = jnp.searchsorted(pends, jnp.minimum(blk_start, pends[-1] - 1), side='right')
    blk_e = jnp.clip(blk_e, 0, N_EXPERTS - 1).astype(jnp.int32)
    ids = jnp.arange(N_EXPERTS, dtype=jnp.int32)
    later = jnp.where((counts[None, :] > 0) & (ids[None, :] > ids[:, None]), ids[None, :], N_EXPERTS)
    next_e = jnp.min(later, axis=1)
    next_e = jnp.where(next_e == N_EXPERTS, -1, next_e).astype(jnp.int32)
    return dest.reshape(-1, 2), row_tok, blk_e, next_e[blk_e], n_valid.reshape(1)


def _final_row_copy(dest_ref, y_hbm, ybuf, sem, blk, slot, k, t):
    src = dest_ref[k * M_ROWS + blk * LN_ROWS + t]
    return pltpu.make_async_copy(y_hbm.at[pl.ds(src, 1)], ybuf.at[slot, pl.ds(k * LN_ROWS + t, 1)], sem.at[slot])


def _final_kernel(dest_ref, base_ref, wt_ref, g_ref, b_ref, y_hbm, op_ref, os_ref, ybuf, sem):
    i = pl.program_id(0)
    n_rows = 2 * LN_ROWS

    def for_rows(blk, slot, action):
        for k in range(2):
            def body(t, carry, k=k):
                action(_final_row_copy(dest_ref, y_hbm, ybuf, sem, blk, slot, k, t))
                return carry
            lax.fori_loop(0, LN_ROWS, body, 0, unroll=8)

    @pl.when(i == 0)
    def _():
        for_rows(0, 0, lambda cp: cp.start())

    @pl.when(i + 1 < pl.num_programs(0))
    def _():
        for_rows(i + 1, (i + 1) & 1, lambda cp: cp.start())

    slot = i & 1
    for_rows(i, slot, lambda cp: cp.wait())

    wt = wt_ref[...]
    r = base_ref[...] + wt[:, 0:1] * ybuf[slot, 0:LN_ROWS, :] + wt[:, 1:2] * ybuf[slot, LN_ROWS:n_rows, :]
    y = _layer_norm_rows(r, g_ref[...], b_ref[...])

    @pl.when(i < SEQ // LN_ROWS)
    def _():
        op_ref[...] = y

    @pl.when(i >= SEQ // LN_ROWS)
    def _():
        os_ref[...] = y


def _final(dest_flat, base, y_rows, wt, g, b):
    npb = SEQ // LN_ROWS
    return pl.pallas_call(
        _final_kernel,
        grid_spec=pltpu.PrefetchScalarGridSpec(
            num_scalar_prefetch=1,
            grid=(npb + 1,),
            in_specs=[
                pl.BlockSpec((LN_ROWS, D_MODEL), lambda i, d: (i, 0)),
                pl.BlockSpec((LN_ROWS, LANES), lambda i, d: (i, 0)),
                pl.BlockSpec((1, D_MODEL), lambda i, d: (0, 0)),
                pl.BlockSpec((1, D_MODEL), lambda i, d: (0, 0)),
                pl.BlockSpec(memory_space=pl.ANY),
            ],
            out_specs=[pl.BlockSpec((LN_ROWS, D_MODEL), lambda i, d: (jnp.minimum(i, npb - 1), 0)),
                       pl.BlockSpec((N_SAMPLE, D_MODEL), lambda i, d: (0, 0))],
            scratch_shapes=[pltpu.VMEM((2, 2 * LN_ROWS, D_MODEL), F32), pltpu.SemaphoreType.DMA((2,))],
        ),
        out_shape=[jax.ShapeDtypeStruct((SEQ, D_MODEL), F32),
                   jax.ShapeDtypeStruct((N_SAMPLE, D_MODEL), F32)],
        compiler_params=_cp(("arbitrary",)),
        name="combine_ln2",
    )(dest_flat, base, wt, g.reshape(1, D_MODEL), b.reshape(1, D_MODEL), y_rows)


SHIFT_SLOTS = 4
SHIFT_CHUNK = 511


def _cache_shift_kernel(c1, c2, c3, n1, n2, n3, o1, o2, o3, buf, rsem, wsem, nsem):
    new_copies = [pltpu.make_async_copy(n, o.at[0, :, pl.ds(w - DEC_SEQ, DEC_SEQ)], nsem.at[g])
                  for g, (n, o, w) in enumerate(zip((n1, n2, n3), (o1, o2, o3), A_WINDOWS))]
    for cp in new_copies:
        cp.start()

    pieces = []
    for c, o, w in zip((c1, c2, c3), (o1, o2, o3), A_WINDOWS):
        chunk = min(w - DEC_SEQ, SHIFT_CHUNK)
        for b in range(DEC_BATCH):
            for j in range((w - DEC_SEQ) // chunk):
                pieces.append((c, o, b, j * chunk, chunk))

    def rd(p):
        c, _, b, j0, n = pieces[p]
        s = p % SHIFT_SLOTS
        return pltpu.make_async_copy(c.at[0, b, pl.ds(DEC_SEQ + j0, n)], buf.at[s, pl.ds(0, n)], rsem.at[s])

    def wr(p):
        _, o, b, j0, n = pieces[p]
        s = p % SHIFT_SLOTS
        return pltpu.make_async_copy(buf.at[s, pl.ds(0, n)], o.at[0, b, pl.ds(j0, n)], wsem.at[s])

    n_pieces = len(pieces)
    for p in range(SHIFT_SLOTS):
        rd(p).start()
    for p in range(n_pieces):
        rd(p).wait()
        wr(p).start()
        if p >= 1 and p - 1 + SHIFT_SLOTS < n_pieces:
            wr(p - 1).wait()
            rd(p - 1 + SHIFT_SLOTS).start()
    for p in range(max(n_pieces - SHIFT_SLOTS, 0), n_pieces):
        wr(p).wait()
    for cp in new_copies:
        cp.wait()


def _cache_shift(caches, news):
    any_spec = pl.BlockSpec(memory_space=pl.ANY)
    vmem_spec = pl.BlockSpec(memory_space=pltpu.VMEM)
    return pl.pallas_call(
        _cache_shift_kernel,
        in_specs=[any_spec] * 3 + [vmem_spec] * 3,
        out_specs=[any_spec] * 3,
        out_shape=[jax.ShapeDtypeStruct(c.shape, c.dtype) for c in caches],
        scratch_shapes=[pltpu.VMEM((SHIFT_SLOTS, SHIFT_CHUNK + 1, 2, A_HEADS, A_HEAD_DIM), F32),
                        pltpu.SemaphoreType.DMA((SHIFT_SLOTS,)), pltpu.SemaphoreType.DMA((SHIFT_SLOTS,)),
                        pltpu.SemaphoreType.DMA((3,))],
        compiler_params=pltpu.CompilerParams(vmem_limit_bytes=VMEM_LIMIT),
        name="cache_shift",
    )(*caches, *news)


def kernel(x_prompt, x_sample, p_prompt, p_sample, cache_kv_a1, cache_kv_a2, cache_kv_a3, state_gla,
           ln_emb_g, ln_emb_b, ln1_g, ln1_b, ln2_g, ln2_b, w_in, w_gk2, b_gk, gla_norm_g, w_br_a, w_br_b,
           w_out, w_router_group, w_router_expert, w_gate, w_up, w_down, w_ple_gate, w_ple):
    xp = x_prompt.reshape(SEQ, D_MODEL)
    xs = x_sample.reshape(N_SAMPLE, D_MODEL)
    xn, xb = _ln_emb(xp, xs, ln_emb_g, ln_emb_b)

    w_in0 = w_in[0]
    rope = _rope_tables()
    u = _project(xb, w_in0, MAIN_COLS, 2 * 3 * A_WIDTH // COL_TILE, rope)
    gates = _project(xb, w_in0[:, MAIN_COLS + GLA_LR:], 2 * D_MODEL, 0, rope)
    gk = _gla_gate(xb, w_in0[:, MAIN_COLS:MAIN_COLS + GLA_LR], w_gk2[0], b_gk[0])

    o_a = _attn_prompt(u)
    u_s = u[SEQ:]
    qkv_s = u_s[:, :A_QKV_COLS].reshape(DEC_BATCH, DEC_SEQ, 9, A_HEADS, A_HEAD_DIM)
    c1 = cache_kv_a1.reshape(DEC_BATCH, A_REACH, 2, A_HEADS, A_HEAD_DIM)
    c2 = cache_kv_a2.reshape(DEC_BATCH, A_REACH, 4 * 2, A_HEADS, A_HEAD_DIM)
    c3 = cache_kv_a3.reshape(DEC_BATCH, A_REACH, 16 * 2, A_HEADS, A_HEAD_DIM)
    o_as = _attn_sample(qkv_s, c1, c2, c3)
    o_a = jnp.concatenate([o_a, o_as.reshape(N_SAMPLE, A_WIDTH).astype(BF16)], axis=0)

    o_g, st_t = _gla_prompt(u, gk, gla_norm_g[0])

    def spad(a):
        a = a.reshape(DEC_BATCH, DEC_SEQ, a.shape[-1])
        return jnp.pad(a, ((0, 0), (0, GLA_SPAD - DEC_SEQ), (0, 0)))

    c0 = A_QKV_COLS
    o_gs, st_s = _gla_sample(
        spad(u_s[:, c0:c0 + GLA_QK]), spad(u_s[:, c0 + GLA_QK:c0 + 2 * GLA_QK]),
        spad(u_s[:, c0 + 2 * GLA_QK:c0 + 2 * GLA_QK + GLA_V]), spad(u_s[:, c0 + 2 * GLA_QK + GLA_V:]),
        spad(gk[SEQ:]), gla_norm_g[0], state_gla[0])
    o_gs = o_gs[:, :DEC_SEQ].reshape(N_SAMPLE, GLA_V).astype(BF16)
    o_g = jnp.concatenate([o_g, o_gs], axis=0)

    mixed = _mix(o_a, o_g, w_br_a[0].astype(BF16), w_br_b[0].astype(BF16), gates)
    x1, x1b = _out_ln(mixed, w_out[0].astype(BF16), xn, ln1_g[0], ln1_b[0])

    p = jnp.concatenate([p_prompt[0].reshape(SEQ, PLE_DIM), p_sample[0].reshape(N_SAMPLE, PLE_DIM)], axis=0)
    base = _ple(x1b, w_ple_gate[0].astype(BF16), p.astype(BF16), w_ple[0].astype(BF16), x1)
    ridx, rwt = _router(x1, w_router_group[0], w_router_expert[0])
    dest, row_tok, blk_e, blk_next, n_valid = _moe_plan(ridx[:, :2])
    y_rows = _moe_experts(x1, row_tok, blk_e, blk_next, n_valid, w_gate[0], w_up[0], w_down[0])
    y_p, y_s = _final(dest.T.reshape(-1), base, y_rows, rwt, ln2_g[0], ln2_b[0])

    def kv_prompt(g):
        w = A_WINDOWS[g]
        k = u[SEQ - w:SEQ, (3 + g) * A_WIDTH:(4 + g) * A_WIDTH].reshape(w, A_HEADS, A_HEAD_DIM)
        v = u[SEQ - w:SEQ, (6 + g) * A_WIDTH:(7 + g) * A_WIDTH].reshape(w, A_HEADS, A_HEAD_DIM)
        return jnp.stack([k, v], axis=1)[None, None]

    kv_new = [jnp.stack([qkv_s[:, :, 3 + g], qkv_s[:, :, 6 + g]], axis=2) for g in range(3)]
    kv_s1, kv_s2, kv_s3 = _cache_shift((cache_kv_a1, cache_kv_a2, cache_kv_a3), kv_new)

    gla_state_prompt = st_t[None, None]
    return (y_p.reshape(1, SEQ, D_MODEL), y_s.reshape(DEC_BATCH, DEC_SEQ, D_MODEL),
            kv_prompt(0), kv_prompt(1), kv_prompt(2), gla_state_prompt,
            kv_s1, kv_s2, kv_s3, st_s[None])
```

```python
import functools

import jax
import jax.numpy as jnp
from jax import lax
from jax.experimental import pallas as pl
from jax.experimental.pallas import tpu as pltpu

F32 = jnp.float32
BF16 = jnp.bfloat16

D_MODEL = 2048
SEQ = 8192
DEC_BATCH = 32
DEC_SEQ = 4
PAST_LEN = 16384
N_SAMPLE = DEC_BATCH * DEC_SEQ
M_ROWS = SEQ + N_SAMPLE
A_WINDOWS = (128, 512, 2048)
A_DILATIONS = (1, 4, 16)
A_REACH = 128
A_HEADS = 8
A_HEAD_DIM = 128
A_WIDTH = A_HEADS * A_HEAD_DIM
ROT_DIM = 32
ROPE_THETA = 500000.0
A_QKV_COLS = 9 * A_WIDTH
GLA_HEADS = 4
GLA_DK = 256
GLA_DV = 512
GLA_LR = 16
GLA_TAU = 16.0
GLA_CHUNK = 64
GLA_QK = GLA_HEADS * GLA_DK
GLA_V = GLA_HEADS * GLA_DV
MAIN_COLS = A_QKV_COLS + 2 * GLA_QK + 2 * GLA_V
MOE_GROUPS = 4
MOE_PER_GROUP = 8
N_EXPERTS = 32
D_EXPERT = 1024
PLE_DIM = 256
LN_EPS = 1e-5
ALPHA = 2.0 ** 0.25

LANES = 128
VMEM_LIMIT = 56 * 1024 * 1024
ROW_TILE_PROJ = 1664
ROW_TILE = 832
MERGE_ROWS = 416
COL_TILE = 512
LN_ROWS = 128
MOE_BLK = 256
MOE_NBLK = (2 * M_ROWS) // MOE_BLK + N_EXPERTS
MOE_ROWS = MOE_NBLK * MOE_BLK
ATT_SB = 2048
GLA_ROWS = 512
ATT_VMEM_LIMIT = 60 * 1024 * 1024
SHIFT_SLOTS = 4
SHIFT_CHUNK = 511
ATT_UNROLL = 4
GLA_UNROLL = 4


def _cp(sem, vmem=VMEM_LIMIT):
    return pltpu.CompilerParams(dimension_semantics=sem, vmem_limit_bytes=vmem)


def _sigmoid(x):
    return 1.0 / (1.0 + jnp.exp(-x))


def _dot(a, b):
    return jnp.dot(a, b, preferred_element_type=F32)


def _dot_nt(a, b):
    return lax.dot_general(a, b, (((1,), (1,)), ((), ())), preferred_element_type=F32)


def _split3(x):
    h = x.astype(BF16)
    r = x - h.astype(F32)
    m = r.astype(BF16)
    l = (r - m.astype(F32)).astype(BF16)
    return h, m, l


def _layer_norm_rows(x, g, b):
    mu = jnp.mean(x, axis=-1, keepdims=True)
    xc = x - mu
    var = jnp.mean(xc * xc, axis=-1, keepdims=True)
    return xc * lax.rsqrt(var + LN_EPS) * g + b


def _ln_emb_kernel(xp_ref, xs_ref, g_ref, b_ref, of_ref, ob_ref):
    i = pl.program_id(0)
    x = jnp.where(i < SEQ // LN_ROWS, xp_ref[...], xs_ref[...])
    y = _layer_norm_rows(x, g_ref[...], b_ref[...])
    of_ref[...] = y
    ob_ref[...] = y.astype(BF16)


def _ln_emb(xp, xs, g, b):
    npb = SEQ // LN_ROWS
    return pl.pallas_call(
        _ln_emb_kernel,
        grid=(npb + 1,),
        in_specs=[
            pl.BlockSpec((LN_ROWS, D_MODEL), lambda i: (jnp.minimum(i, npb - 1), 0)),
            pl.BlockSpec((N_SAMPLE, D_MODEL), lambda i: (0, 0)),
            pl.BlockSpec((1, D_MODEL), lambda i: (0, 0)),
            pl.BlockSpec((1, D_MODEL), lambda i: (0, 0)),
        ],
        out_specs=[
            pl.BlockSpec((LN_ROWS, D_MODEL), lambda i: (i, 0)),
            pl.BlockSpec((LN_ROWS, D_MODEL), lambda i: (i, 0)),
        ],
        out_shape=[jax.ShapeDtypeStruct((M_ROWS, D_MODEL), F32),
                   jax.ShapeDtypeStruct((M_ROWS, D_MODEL), BF16)],
        compiler_params=_cp(("arbitrary",)),
        name="ln_emb",
    )(xp, xs, g.reshape(1, D_MODEL), b.reshape(1, D_MODEL))


def _proj_kernel(x_ref, w_ref, c_ref, s1_ref, s2_ref, o_ref, *, n_rope_blocks):
    j = pl.program_id(1)
    acc = _dot_nt(x_ref[...], w_ref[...].astype(BF16))

    @pl.when(j >= n_rope_blocks)
    def _():
        o_ref[...] = acc.astype(o_ref.dtype)

    @pl.when(j < n_rope_blocks)
    def _():
        c, s1, s2 = c_ref[...], s1_ref[...], s2_ref[...]
        for t in range(COL_TILE // LANES):
            a = acc[:, t * LANES:(t + 1) * LANES]
            o_ref[:, t * LANES:(t + 1) * LANES] = (
                a * c + pltpu.roll(a, LANES - ROT_DIM // 2, 1) * s1 + pltpu.roll(a, ROT_DIM // 2, 1) * s2
            ).astype(o_ref.dtype)


def _project(xb, w_t, row0, n_cols, n_rope_blocks, rope, out_dtype):
    c, s1, s2 = rope
    tm, tn = ROW_TILE_PROJ, COL_TILE
    jb0 = row0 // tn
    return pl.pallas_call(
        functools.partial(_proj_kernel, n_rope_blocks=n_rope_blocks),
        grid=(M_ROWS // tm, n_cols // tn),
        in_specs=[
            pl.BlockSpec((tm, D_MODEL), lambda i, j: (i, 0)),
            pl.BlockSpec((tn, D_MODEL), lambda i, j: (jb0 + j, 0)),
            pl.BlockSpec((tm, LANES), lambda i, j: (i, 0)),
            pl.BlockSpec((tm, LANES), lambda i, j: (i, 0)),
            pl.BlockSpec((tm, LANES), lambda i, j: (i, 0)),
        ],
        out_specs=pl.BlockSpec((tm, tn), lambda i, j: (i, j)),
        out_shape=jax.ShapeDtypeStruct((M_ROWS, n_cols), out_dtype),
        compiler_params=_cp(("arbitrary", "arbitrary")),
        name="in_proj",
    )(xb, w_t, c, s1, s2)


def _rope_tables():
    half = ROT_DIM // 2
    inv = ROPE_THETA ** (-jnp.arange(half, dtype=F32) * (2.0 / ROT_DIM))
    pos = jnp.concatenate([
        jnp.arange(SEQ, dtype=jnp.int32),
        jnp.tile(PAST_LEN + jnp.arange(DEC_SEQ, dtype=jnp.int32), DEC_BATCH)])
    ang = pos.astype(F32)[:, None] * inv[None, :]
    cos, sin = jnp.cos(ang), jnp.sin(ang)
    ones = jnp.ones((M_ROWS, LANES - ROT_DIM), F32)
    zeros = jnp.zeros((M_ROWS, LANES - ROT_DIM), F32)
    zh = jnp.zeros((M_ROWS, half), F32)
    c = jnp.concatenate([cos, cos, ones], axis=1)
    s1 = jnp.concatenate([-sin, zh, zeros], axis=1)
    s2 = jnp.concatenate([zh, sin, zeros], axis=1)
    return c, s1, s2


def _gk_kernel(x_ref, wlr_ref, wgk_ref, b_ref, o_ref):
    blr = _dot_nt(x_ref[...], wlr_ref[...])
    z = _dot(blr.astype(BF16), wgk_ref[...]) + b_ref[...]
    o_ref[...] = (jnp.minimum(z, 0.0) - jnp.log1p(jnp.exp(-jnp.abs(z)))) * (1.0 / GLA_TAU)


def _gla_gate(xb, w_lr_t, w_gk2, b_gk):
    wlr = jnp.zeros((LANES, D_MODEL), BF16).at[:GLA_LR].set(w_lr_t.astype(BF16))
    wgk = jnp.zeros((LANES, GLA_QK), BF16).at[:GLA_LR].set(w_gk2.astype(BF16))
    tm = ROW_TILE
    return pl.pallas_call(
        _gk_kernel,
        grid=(M_ROWS // tm,),
        in_specs=[
            pl.BlockSpec((tm, D_MODEL), lambda i: (i, 0)),
            pl.BlockSpec((LANES, D_MODEL), lambda i: (0, 0)),
            pl.BlockSpec((LANES, GLA_QK), lambda i: (0, 0)),
            pl.BlockSpec((1, GLA_QK), lambda i: (0, 0)),
        ],
        out_specs=pl.BlockSpec((tm, GLA_QK), lambda i: (i, 0)),
        out_shape=jax.ShapeDtypeStruct((M_ROWS, GLA_QK), F32),
        compiler_params=_cp(("arbitrary",)),
        name="gla_gate",
    )(xb, wlr, wgk, b_gk.reshape(1, GLA_QK))


def _gla_chunk(q, k, v, g, br, gn, s0, causal, tri_b, eye):
    n = g.shape[0]
    q, k, br = q.astype(F32), k.astype(F32), br.astype(F32)
    g1, g2, g3 = _split3(g)
    b = _dot(tri_b, g1) + _dot(tri_b, g2) + _dot(tri_b, g3)
    b_last = b[n - 1:n, :]
    q_dec = (q * ((GLA_DK ** -0.5) * jnp.exp(b))).astype(BF16)
    k_inv = (k * jnp.exp(-b)).astype(BF16)
    k_dec = k * jnp.exp(b_last - b)
    vb = v.astype(BF16)
    att = jnp.where(causal, _dot_nt(q_dec, k_inv), 0.0)
    o = _dot(att.astype(BF16), vb) + _dot(q_dec, s0.astype(BF16))
    d1, d2, d3 = _split3(jnp.exp(b_last))
    rid = lax.broadcasted_iota(jnp.int32, (n + GLA_TPAD, GLA_DK), 0)
    rows = jnp.concatenate([k_dec, jnp.zeros((GLA_TPAD, GLA_DK), F32)], axis=0)
    rows = jnp.where(rid == n, d1.astype(F32), rows)
    rows = jnp.where(rid == n + 1, d2.astype(F32), rows)
    rows = jnp.where(rid == n + 2, d3.astype(F32), rows)
    cols = _dot_nt(eye, rows.astype(BF16))
    dcol = cols[:, n:n + 1] + cols[:, n + 1:n + 2] + cols[:, n + 2:n + 3]
    s1 = s0 * dcol + _dot(cols[:, :n].astype(BF16), vb)
    ms = jnp.mean(o * o, axis=-1, keepdims=True)
    return o * lax.rsqrt(ms + LN_EPS) * gn * (br * _sigmoid(br)), s1


GLA_TPAD = 16


def _gla_consts(n):
    r = lax.broadcasted_iota(jnp.int32, (n, n), 0)
    c = lax.broadcasted_iota(jnp.int32, (n, n), 1)
    causal = r >= c
    er = lax.broadcasted_iota(jnp.int32, (GLA_DK, GLA_DK), 0)
    ec = lax.broadcasted_iota(jnp.int32, (GLA_DK, GLA_DK), 1)
    return causal, jnp.where(causal, 1.0, 0.0).astype(BF16), jnp.where(er == ec, 1.0, 0.0).astype(BF16)


def _gla_prompt_kernel(q_ref, k_ref, v_ref, br_ref, g_ref, gn_ref, o_ref, st_ref):
    @pl.when(pl.program_id(1) == 0)
    def _():
        st_ref[...] = jnp.zeros_like(st_ref)

    causal, tri_b, eye = _gla_consts(GLA_CHUNK)
    gn = gn_ref[...]

    def body(cc, carry):
        rows = pl.ds(pl.multiple_of(cc * GLA_CHUNK, GLA_CHUNK), GLA_CHUNK)
        o, s1 = _gla_chunk(q_ref[rows, :], k_ref[rows, :], v_ref[rows, :], g_ref[rows, :],
                           br_ref[rows, :], gn, st_ref[0], causal, tri_b, eye)
        st_ref[0] = s1
        o_ref[rows, :] = o.astype(BF16)
        return carry

    lax.fori_loop(0, GLA_ROWS // GLA_CHUNK, body, 0, unroll=GLA_UNROLL)


def _gla_prompt(u, gk, gnorm):
    qb0 = 0
    kb0 = GLA_QK // GLA_DK
    vb0 = (2 * GLA_QK) // GLA_DV
    rb0 = (2 * GLA_QK + GLA_V) // GLA_DV
    return pl.pallas_call(
        _gla_prompt_kernel,
        grid=(GLA_HEADS, SEQ // GLA_ROWS),
        in_specs=[
            pl.BlockSpec((GLA_ROWS, GLA_DK), lambda h, c: (c, qb0 + h)),
            pl.BlockSpec((GLA_ROWS, GLA_DK), lambda h, c: (c, kb0 + h)),
            pl.BlockSpec((GLA_ROWS, GLA_DV), lambda h, c: (c, vb0 + h)),
            pl.BlockSpec((GLA_ROWS, GLA_DV), lambda h, c: (c, rb0 + h)),
            pl.BlockSpec((GLA_ROWS, GLA_DK), lambda h, c: (c, h)),
            pl.BlockSpec((1, GLA_DV), lambda h, c: (0, 0)),
        ],
        out_specs=[
            pl.BlockSpec((GLA_ROWS, GLA_DV), lambda h, c: (c, h)),
            pl.BlockSpec((1, GLA_DK, GLA_DV), lambda h, c: (h, 0, 0)),
        ],
        out_shape=[jax.ShapeDtypeStruct((SEQ, GLA_V), BF16),
                   jax.ShapeDtypeStruct((GLA_HEADS, GLA_DK, GLA_DV), F32)],
        compiler_params=_cp(("arbitrary", "arbitrary")),
        name="gla_prompt",
    )(u, u, u, u, gk, gnorm.reshape(1, GLA_DV))


GLA_SPAD = 16


def _gla_sample_kernel(q_ref, k_ref, v_ref, br_ref, g_ref, gn_ref, s0_ref, o_ref, s1_ref):
    causal, tri_b, eye = _gla_consts(GLA_SPAD)
    gn = gn_ref[...]
    for h in range(GLA_HEADS):
        ks = slice(h * GLA_DK, (h + 1) * GLA_DK)
        vs = slice(h * GLA_DV, (h + 1) * GLA_DV)
        o, s1 = _gla_chunk(q_ref[0, :, ks], k_ref[0, :, ks], v_ref[0, :, vs], g_ref[0, :, ks],
                           br_ref[0, :, vs], gn, s0_ref[0, h], causal, tri_b, eye)
        s1_ref[0, h] = s1
        o_ref[0, :, vs] = o


def _gla_sample(q, k, v, br, g, gnorm, s0):
    def spec(c):
        return pl.BlockSpec((1, GLA_SPAD, c), lambda b: (b, 0, 0))
    st_spec = pl.BlockSpec((1, GLA_HEADS, GLA_DK, GLA_DV), lambda b: (b, 0, 0, 0))
    return pl.pallas_call(
        _gla_sample_kernel,
        grid=(DEC_BATCH,),
        in_specs=[spec(GLA_QK), spec(GLA_QK), spec(GLA_V), spec(GLA_V), spec(GLA_QK),
                  pl.BlockSpec((1, GLA_DV), lambda b: (0, 0)), st_spec],
        out_specs=[spec(GLA_V), st_spec],
        out_shape=[jax.ShapeDtypeStruct((DEC_BATCH, GLA_SPAD, GLA_V), F32),
                   jax.ShapeDtypeStruct((DEC_BATCH, GLA_HEADS, GLA_DK, GLA_DV), F32)],
        compiler_params=_cp(("arbitrary",)),
        name="gla_sample",
    )(q, k, v, br, g, gnorm.reshape(1, GLA_DV), s0)


class _CacheShiftInBackground:
    def __init__(self, step, n_steps, new_ref, cache_hbm, out_hbm, buf, rsem, wsem, nsem):
        self.step, self.n_steps = step, n_steps
        self.new_ref, self.cache, self.out = new_ref, cache_hbm, out_hbm
        self.buf, self.rsem, self.wsem, self.nsem = buf, rsem, wsem, nsem

    def _read(self, b, k):
        src = self.cache.at[0, b, pl.ds(DEC_SEQ + k * SHIFT_CHUNK, SHIFT_CHUNK)]
        return pltpu.make_async_copy(src, self.buf.at[k, pl.ds(0, SHIFT_CHUNK)], self.rsem.at[k])

    def _write(self, b, k):
        dst = self.out.at[0, b, pl.ds(k * SHIFT_CHUNK, SHIFT_CHUNK)]
        return pltpu.make_async_copy(self.buf.at[k, pl.ds(0, SHIFT_CHUNK)], dst, self.wsem.at[k])

    def _new_rows(self):
        w = A_WINDOWS[2]
        return pltpu.make_async_copy(self.new_ref, self.out.at[0, :, pl.ds(w - DEC_SEQ, DEC_SEQ)], self.nsem.at[0])

    def top_of_step(self):
        @pl.when(self.step == 0)
        def _():
            self._new_rows().start()

        @pl.when(self.step > 0)
        def _():
            for k in range(SHIFT_SLOTS):
                self._write(self.step - 1, k).wait()

        for k in range(SHIFT_SLOTS):
            self._read(self.step, k).start()

    def turn_reads_into_writes(self):
        for k in range(SHIFT_SLOTS):
            self._read(self.step, k).wait()
            self._write(self.step, k).start()

    def end_of_step(self):
        @pl.when(self.step == self.n_steps - 1)
        def _():
            for k in range(SHIFT_SLOTS):
                self._write(self.step, k).wait()
            self._new_rows().wait()


def _attn_prompt_kernel(q0, q1, q2, kc0, kp0, vc0, vp0, kc1, kp1, vc1, vp1, kc2, kp2, vc2, vp2,
                        new3_ref, cache3_hbm, o_ref, kv3_hbm,
                        kb0, vb0, kb1, vb1, kb2, vb2, acc_ref, m_ref, l_ref, cbuf, rsem, wsem, nsem):
    sb = pl.program_id(0)
    shifter = _CacheShiftInBackground(sb * pl.num_programs(1) + pl.program_id(1), DEC_BATCH,
                                      new3_ref, cache3_hbm, kv3_hbm, cbuf, rsem, wsem, nsem)
    shifter.top_of_step()
    groups = ((q0, kc0, kp0, vc0, vp0, kb0, vb0, 1),
              (q1, kc1, kp1, vc1, vp1, kb1, vb1, 4),
              (q2, kc2, kp2, vc2, vp2, kb2, vb2, 16))
    scale = A_HEAD_DIM ** -0.5
    n_win = 2 * A_REACH
    row = lax.broadcasted_iota(jnp.int32, (A_REACH, n_win), 0)
    col = lax.broadcasted_iota(jnp.int32, (A_REACH, n_win), 1)
    in_reach = jnp.logical_and(col >= row, col <= row + A_REACH)
    in_prev = col < A_REACH
    ones = jnp.ones((n_win, LANES), BF16)

    for gi, (q_ref, kc, kp, vc, vp, kb, vb, d) in enumerate(groups):
        win = A_REACH * d
        kb[0:win, :] = kp[...]
        kb[win:win + ATT_SB, :] = kc[...]
        vb[0:win, :] = vp[...]
        vb[win:win + ATT_SB, :] = vc[...]
        shift = {1: 0, 4: 2, 16: 4}[d]

        def tile(idx, carry, q_ref=q_ref, kb=kb, vb=vb, d=d, win=win, shift=shift, gi=gi):
            r = idx & (d - 1)
            nb = idx >> shift
            start = r + nb * win
            if d == 1:
                start = pl.multiple_of(start, A_REACH)
                qrows = pl.ds(start, A_REACH)
                wrows = pl.ds(start, n_win)
            else:
                qrows = pl.ds(start, A_REACH, stride=d)
                wrows = pl.ds(start, n_win, stride=d)
            q = q_ref[qrows, :].astype(BF16)
            s = _dot_nt(q, kb[wrows, :].astype(BF16)) * scale
            prev_bias = jnp.where(jnp.logical_or(sb > 0, nb > 0), 0.0, -jnp.inf)
            s = jnp.where(in_reach, s, -jnp.inf) + jnp.where(in_prev, prev_bias, 0.0)
            m_t = jnp.max(s, axis=1, keepdims=True)
            p = jnp.exp(s - m_t).astype(BF16)
            l_t = _dot(p, ones)
            num = _dot(p, vb[wrows, :].astype(BF16))
            m_t = jnp.broadcast_to(m_t, (A_REACH, LANES))
            if gi == 0:
                acc_ref[qrows, :] = num
                m_ref[qrows, :] = m_t
                l_ref[qrows, :] = l_t
            else:
                m_o = m_ref[qrows, :]
                m_n = jnp.maximum(m_o, m_t)
                a = jnp.exp(m_o - m_n)
                b = jnp.exp(m_t - m_n)
                acc_ref[qrows, :] = a * acc_ref[qrows, :] + b * num
                l_ref[qrows, :] = a * l_ref[qrows, :] + b * l_t
                m_ref[qrows, :] = m_n
            return carry

        lax.fori_loop(0, ATT_SB // A_REACH, tile, 0, unroll=ATT_UNROLL)
        if gi == 1:
            shifter.turn_reads_into_writes()

    o_ref[...] = (acc_ref[...] / l_ref[...]).astype(BF16)
    shifter.end_of_step()


def _attn_prompt(u, new3, cache3):
    nh = A_HEADS
    assert (SEQ // ATT_SB) * nh == DEC_BATCH and SHIFT_SLOTS * SHIFT_CHUNK == A_WINDOWS[2] - DEC_SEQ
    in_specs, scratch = [], []
    for g in range(3):
        in_specs.append(pl.BlockSpec((ATT_SB, LANES), lambda sb, h, g=g: (sb, g * nh + h)))
    for g, d in enumerate(A_DILATIONS):
        win = A_REACH * d
        per = ATT_SB // win
        for part in (3, 6):
            cb = part * nh + g * nh
            in_specs.append(pl.BlockSpec((ATT_SB, LANES), lambda sb, h, cb=cb: (sb, cb + h)))
            in_specs.append(pl.BlockSpec(
                (win, LANES), lambda sb, h, cb=cb, per=per: (jnp.maximum(sb * per - 1, 0), cb + h)))
        scratch += [pltpu.VMEM((win + ATT_SB, LANES), F32), pltpu.VMEM((win + ATT_SB, LANES), F32)]
    scratch += [pltpu.VMEM((ATT_SB, LANES), F32)] * 3
    scratch += [pltpu.VMEM((SHIFT_SLOTS, SHIFT_CHUNK + 1, 2, A_HEADS, A_HEAD_DIM), F32),
                pltpu.SemaphoreType.DMA((SHIFT_SLOTS,)), pltpu.SemaphoreType.DMA((SHIFT_SLOTS,)),
                pltpu.SemaphoreType.DMA((1,))]
    in_specs += [pl.BlockSpec(new3.shape, lambda sb, h: (0, 0, 0, 0, 0)), pl.BlockSpec(memory_space=pl.ANY)]
    return pl.pallas_call(
        _attn_prompt_kernel,
        grid=(SEQ // ATT_SB, nh),
        in_specs=in_specs,
        out_specs=[pl.BlockSpec((ATT_SB, LANES), lambda sb, h: (sb, h)), pl.BlockSpec(memory_space=pl.ANY)],
        out_shape=[jax.ShapeDtypeStruct((SEQ, A_WIDTH), BF16), jax.ShapeDtypeStruct(cache3.shape, cache3.dtype)],
        scratch_shapes=scratch,
        compiler_params=_cp(("arbitrary", "arbitrary"), vmem=ATT_VMEM_LIMIT),
        name="attn_prompt",
    )(*([u] * 15), new3, cache3)


def _attn_sample_kernel(qkv_ref, c1_ref, c2_ref, c3_ref, o_ref):
    scale = A_HEAD_DIM ** -0.5
    pos = lax.broadcasted_iota(jnp.int32, (A_REACH, 1, 1), 0)

    def merge(state, m_t, l_t, num):
        if state is None:
            return m_t, l_t, num
        m_o, l_o, n_o = state
        m_n = jnp.maximum(m_o, m_t)
        a, b = jnp.exp(m_o - m_n), jnp.exp(m_t - m_n)
        return m_n, a * l_o + b * l_t, a * n_o + b * num

    for s in range(DEC_SEQ):
        state = None
        for g in range(3):
            q = qkv_ref[0, s, g]
            if g == 0:
                kt, vt = c1_ref[0, :, 0], c1_ref[0, :, 1]
                valid = pos >= s
                new_rows = range(s + 1)
            else:
                c_ref = c2_ref if g == 1 else c3_ref
                kt, vt = c_ref[0, :, 2 * s], c_ref[0, :, 2 * s + 1]
                valid = None
                new_rows = (s,)
            sc = jnp.sum(kt * q[None], axis=-1, keepdims=True) * scale
            if valid is not None:
                sc = jnp.where(valid, sc, -jnp.inf)
            s_new = [jnp.sum(qkv_ref[0, t, 3 + g] * q, axis=-1, keepdims=True) * scale for t in new_rows]
            m_t = jnp.max(sc, axis=0)
            for sn in s_new:
                m_t = jnp.maximum(m_t, sn)
            p = jnp.exp(sc - m_t[None])
            l_t = jnp.sum(p, axis=0)
            num = jnp.sum(p * vt, axis=0)
            for t, sn in zip(new_rows, s_new):
                pn = jnp.exp(sn - m_t)
                l_t = l_t + pn
                num = num + pn * qkv_ref[0, t, 6 + g]
            state = merge(state, m_t, l_t, num)
        o_ref[0, s] = state[2] / state[1]


def _attn_sample(qkv_s, c1, c2, c3):
    tile = (A_HEADS, A_HEAD_DIM)
    return pl.pallas_call(
        _attn_sample_kernel,
        grid=(DEC_BATCH,),
        in_specs=[
            pl.BlockSpec((1, DEC_SEQ, 9) + tile, lambda b: (b, 0, 0, 0, 0)),
            pl.BlockSpec((1, A_REACH, 2) + tile, lambda b: (b, 0, 0, 0, 0)),
            pl.BlockSpec((1, A_REACH, 2 * DEC_SEQ) + tile, lambda b: (b, 0, 0, 0, 0)),
            pl.BlockSpec((1, A_REACH, 2 * DEC_SEQ) + tile, lambda b: (b, 0, 0, 0, 0)),
        ],
        out_specs=pl.BlockSpec((1, DEC_SEQ) + tile, lambda b: (b, 0, 0, 0)),
        out_shape=jax.ShapeDtypeStruct((DEC_BATCH, DEC_SEQ) + tile, F32),
        compiler_params=_cp(("arbitrary",)),
        name="attn_sample",
    )(qkv_s, c1, c2, c3)


def _load_resident(step, pairs):
    @pl.when(step == 0)
    def _():
        for src, dst in pairs:
            pltpu.sync_copy(src, dst)


def _merge_ln1_kernel(oa_ref, og_ref, gt_ref, xn_ref, g_ref, b_ref, wa_hbm, wb_hbm, wo_hbm,
                      x1_ref, wa, wb, wo, mixed, resid):
    _load_resident(pl.program_id(0), ((wa_hbm, wa), (wb_hbm, wb), (wo_hbm, wo)))
    nj = D_MODEL // COL_TILE
    oa, og = oa_ref[...], og_ref[...]
    for t in range(nj):
        cs = slice(t * COL_TILE, (t + 1) * COL_TILE)
        gs = slice(D_MODEL + t * COL_TILE, D_MODEL + (t + 1) * COL_TILE)
        ya = _dot(oa, wa[:, cs])
        yb = _dot(og, wb[:, cs])
        ga, gb = gt_ref[:, cs].astype(F32), gt_ref[:, gs].astype(F32)
        mixed[:, cs] = (_sigmoid(ga) * ya + _sigmoid(gb) * yb).astype(BF16)
    mx = mixed[...]
    for t in range(nj):
        cs = slice(t * COL_TILE, (t + 1) * COL_TILE)
        resid[:, cs] = ALPHA * xn_ref[:, cs] + _dot(mx, wo[:, cs])
    x1_ref[...] = _layer_norm_rows(resid[...], g_ref[...], b_ref[...])


def _merge_ln1(o_a, o_g, gates, xn, g, b, wa, wb, wo):
    tm = MERGE_ROWS
    any_spec = pl.BlockSpec(memory_space=pl.ANY)
    return pl.pallas_call(
        _merge_ln1_kernel,
        grid=(M_ROWS // tm,),
        in_specs=[
            pl.BlockSpec((tm, A_WIDTH), lambda i: (i, 0)),
            pl.BlockSpec((tm, GLA_V), lambda i: (i, 0)),
            pl.BlockSpec((tm, 2 * D_MODEL), lambda i: (i, 0)),
            pl.BlockSpec((tm, D_MODEL), lambda i: (i, 0)),
            pl.BlockSpec((1, D_MODEL), lambda i: (0, 0)),
            pl.BlockSpec((1, D_MODEL), lambda i: (0, 0)),
            any_spec, any_spec, any_spec,
        ],
        out_specs=pl.BlockSpec((tm, D_MODEL), lambda i: (i, 0)),
        out_shape=jax.ShapeDtypeStruct((M_ROWS, D_MODEL), F32),
        scratch_shapes=[pltpu.VMEM((A_WIDTH, D_MODEL), BF16), pltpu.VMEM((GLA_V, D_MODEL), BF16),
                        pltpu.VMEM((D_MODEL, D_MODEL), BF16),
                        pltpu.VMEM((tm, D_MODEL), BF16), pltpu.VMEM((tm, D_MODEL), F32)],
        compiler_params=_cp(("arbitrary",)),
        name="merge_ln1",
    )(o_a, o_g, gates, xn, g.reshape(1, D_MODEL), b.reshape(1, D_MODEL), wa, wb, wo)


def _ple_kernel(x_ref, p_ref, wg_hbm, wp_hbm, o_ref, wg, wp):
    _load_resident(pl.program_id(0), ((wg_hbm, wg), (wp_hbm, wp)))
    xb = x_ref[...].astype(BF16)
    p = p_ref[...]
    for t in range(D_MODEL // COL_TILE):
        cs = slice(t * COL_TILE, (t + 1) * COL_TILE)
        gate = _sigmoid(_dot(xb, wg[:, cs]))
        o_ref[:, cs] = ALPHA * x_ref[:, cs] + gate * _dot(p, wp[:, cs])


def _ple(x1, p, wpg, wple):
    tm = ROW_TILE
    any_spec = pl.BlockSpec(memory_space=pl.ANY)
    return pl.pallas_call(
        _ple_kernel,
        grid=(M_ROWS // tm,),
        in_specs=[
            pl.BlockSpec((tm, D_MODEL), lambda i: (i, 0)),
            pl.BlockSpec((tm, PLE_DIM), lambda i: (i, 0)),
            any_spec, any_spec,
        ],
        out_specs=pl.BlockSpec((tm, D_MODEL), lambda i: (i, 0)),
        out_shape=jax.ShapeDtypeStruct((M_ROWS, D_MODEL), F32),
        scratch_shapes=[pltpu.VMEM((D_MODEL, D_MODEL), BF16), pltpu.VMEM((PLE_DIM, D_MODEL), BF16)],
        compiler_params=_cp(("arbitrary",)),
        name="ple_residual",
    )(x1, p, wpg, wple)


def _router_kernel(x_ref, w_ref, idx_ref, wt_ref):
    xh, xm, _ = _split3(x_ref[...])
    wh, wm, _ = _split3(w_ref[...])
    logits = _dot(xh, wh) + _dot(xh, wm) + _dot(xm, wh)
    lane_i = lax.broadcasted_iota(jnp.int32, (x_ref.shape[0], LANES), 1)
    lane = lane_i.astype(F32)
    lane_group = (lane_i >> 3).astype(F32)
    neg = -jnp.inf

    def top1(v):
        mx = jnp.max(v, axis=1, keepdims=True)
        ix = jnp.min(jnp.where(v == mx, lane, float(LANES)), axis=1, keepdims=True)
        return mx, ix

    gl = jnp.where(lane_i < MOE_GROUPS, logits[:, :LANES], neg)
    g_max, g_idx = top1(gl)
    g_p = 1.0 / jnp.sum(jnp.exp(gl - g_max), axis=1, keepdims=True)
    el = logits[:, LANES:]
    in_group = jnp.logical_and(lane_i < N_EXPERTS, lane_group == g_idx)
    e1v = jnp.where(in_group, el, neg)
    m1, i1 = top1(e1v)
    m2, i2 = top1(jnp.where(lane == i1, neg, e1v))
    t = jnp.exp(m2 - m1)
    w1 = g_p / (1.0 + t)
    w2 = g_p * t / (1.0 + t)
    idx_ref[...] = jnp.where(lane_i == 0, i1, jnp.where(lane_i == 1, i2, 0.0)).astype(jnp.int32)
    wt_ref[...] = jnp.where(lane_i == 0, w1, jnp.where(lane_i == 1, w2, 0.0))


def _router(x1, w_r1, w_r2):
    w = jnp.zeros((D_MODEL, 2 * LANES), F32)
    w = w.at[:, :MOE_GROUPS].set(w_r1).at[:, LANES:LANES + N_EXPERTS].set(w_r2)
    tm = ROW_TILE
    return pl.pallas_call(
        _router_kernel,
        grid=(M_ROWS // tm,),
        in_specs=[pl.BlockSpec((tm, D_MODEL), lambda i: (i, 0)),
                  pl.BlockSpec((D_MODEL, 2 * LANES), lambda i: (0, 0))],
        out_specs=[pl.BlockSpec((tm, LANES), lambda i: (i, 0)),
                   pl.BlockSpec((tm, LANES), lambda i: (i, 0))],
        out_shape=[jax.ShapeDtypeStruct((M_ROWS, LANES), jnp.int32),
                   jax.ShapeDtypeStruct((M_ROWS, LANES), F32)],
        compiler_params=_cp(("arbitrary",)),
        name="router",
    )(x1, w)


WEIGHT_DMA_PRIORITY = 1


def _moe_kernel(be_ref, nx_ref, nv_ref, tok_ref, x_hbm, wg_hbm, wu_hbm, wd_hbm, y_ref,
                sg, su, sd, wgb, wub, wdb, xbuf, wsem, xsem):
    i = pl.program_id(0)
    nv = nv_ref[0]
    e = be_ref[i]

    def weight_copies(ex):
        return (pltpu.make_async_copy(wg_hbm.at[ex], sg, wsem.at[0]),
                pltpu.make_async_copy(wu_hbm.at[ex], su, wsem.at[1]),
                pltpu.make_async_copy(wd_hbm.at[ex], sd, wsem.at[2]))

    def for_rows(blk, slot, action):
        def body(r, carry):
            src = tok_ref[blk * MOE_BLK + r]
            action(pltpu.make_async_copy(x_hbm.at[pl.ds(src, 1)], xbuf.at[slot, pl.ds(r, 1)], xsem.at[slot]))
            return carry
        lax.fori_loop(0, MOE_BLK, body, 0, unroll=8)

    @pl.when(i == 0)
    def _():
        for cp in weight_copies(e):
            cp.start(priority=WEIGHT_DMA_PRIORITY)
        for_rows(0, 0, lambda cp: cp.start())

    @pl.when(i + 1 < nv)
    def _():
        for_rows(i + 1, (i + 1) & 1, lambda cp: cp.start())

    first_of_expert = jnp.logical_or(i == 0, e != be_ref[jnp.maximum(i - 1, 0)])

    @pl.when(jnp.logical_and(i < nv, first_of_expert))
    def _():
        for cp in weight_copies(e):
            cp.wait()
        for r in range(0, D_MODEL, MOE_BLK):
            wgb[r:r + MOE_BLK, :] = sg[r:r + MOE_BLK, :].astype(BF16)
            wub[r:r + MOE_BLK, :] = su[r:r + MOE_BLK, :].astype(BF16)
        for r in range(0, D_EXPERT, MOE_BLK):
            wdb[r:r + MOE_BLK, :] = sd[r:r + MOE_BLK, :].astype(BF16)

        @pl.when(nx_ref[i] >= 0)
        def _():
            for cp in weight_copies(nx_ref[i]):
                cp.start(priority=WEIGHT_DMA_PRIORITY)

    @pl.when(i < nv)
    def _():
        slot = i & 1
        for_rows(i, slot, lambda cp: cp.wait())
        x = xbuf[slot].astype(BF16)
        a = _dot(x, wgb[...])
        h = (a * _sigmoid(a) * _dot(x, wub[...])).astype(BF16)
        y_ref[...] = _dot(h, wdb[...])

    @pl.when(i >= nv)
    def _():
        y_ref[...] = jnp.zeros_like(y_ref)


def _moe_experts(x1, row_tok, blk_e, blk_next, n_valid, w_gate, w_up, w_down):
    any_spec = pl.BlockSpec(memory_space=pl.ANY)
    return pl.pallas_call(
        _moe_kernel,
        grid_spec=pltpu.PrefetchScalarGridSpec(
            num_scalar_prefetch=4,
            grid=(MOE_NBLK,),
            in_specs=[any_spec, any_spec, any_spec, any_spec],
            out_specs=pl.BlockSpec((MOE_BLK, D_MODEL), lambda i, be, nx, nv, tok: (i, 0)),
            scratch_shapes=[
                pltpu.VMEM((D_MODEL, D_EXPERT), F32), pltpu.VMEM((D_MODEL, D_EXPERT), F32),
                pltpu.VMEM((D_EXPERT, D_MODEL), F32),
                pltpu.VMEM((D_MODEL, D_EXPERT), BF16), pltpu.VMEM((D_MODEL, D_EXPERT), BF16),
                pltpu.VMEM((D_EXPERT, D_MODEL), BF16),
                pltpu.VMEM((2, MOE_BLK, D_MODEL), F32),
                pltpu.SemaphoreType.DMA((3,)), pltpu.SemaphoreType.DMA((2,)),
            ],
        ),
        out_shape=jax.ShapeDtypeStruct((MOE_ROWS, D_MODEL), F32),
        compiler_params=_cp(("arbitrary",)),
        name="moe_experts",
    )(blk_e, blk_next, n_valid, row_tok, x1, w_gate, w_up, w_down)


def _moe_plan(e_idx):
    flat_e = e_idx.reshape(-1)
    n_asg = flat_e.shape[0]
    order = jnp.argsort(flat_e)
    se = flat_e[order]
    counts = jnp.zeros((N_EXPERTS,), jnp.int32).at[flat_e].add(1)
    starts = jnp.cumsum(counts) - counts
    pcounts = (counts + MOE_BLK - 1) // MOE_BLK * MOE_BLK
    pends = jnp.cumsum(pcounts)
    pstarts = pends - pcounts
    dest_sorted = pstarts[se] + jnp.arange(n_asg, dtype=jnp.int32) - starts[se]
    dest = jnp.zeros((n_asg,), jnp.int32).at[order].set(dest_sorted)
    row_tok = jnp.zeros((MOE_ROWS,), jnp.int32).at[dest_sorted].set((order // 2).astype(jnp.int32))
    n_valid = (pends[-1] // MOE_BLK).astype(jnp.int32)
    blk_start = jnp.arange(MOE_NBLK, dtype=jnp.int32) * MOE_BLK
    blk_e = jnp.searchsorted(pends, jnp.minimum(blk_start, pends[-1] - 1), side='right')
    blk_e = jnp.clip(blk_e, 0, N_EXPERTS - 1).astype(jnp.int32)
    ids = jnp.arange(N_EXPERTS, dtype=jnp.int32)
    later = jnp.where((counts[None, :] > 0) & (ids[None, :] > ids[:, None]), ids[None, :], N_EXPERTS)
    next_e = jnp.min(later, axis=1)
    next_e = jnp.where(next_e == N_EXPERTS, -1, next_e).astype(jnp.int32)
    return dest.reshape(-1, 2), row_tok, blk_e, next_e[blk_e], n_valid.reshape(1)


def _final_row_copy(dest_ref, y_hbm, ybuf, sem, blk, slot, k, t):
    src = dest_ref[k * M_ROWS + blk * LN_ROWS + t]
    return pltpu.make_async_copy(y_hbm.at[pl.ds(src, 1)], ybuf.at[slot, pl.ds(k * LN_ROWS + t, 1)], sem.at[slot])


def _final_kernel(dest_ref, base_ref, wt_ref, g_ref, b_ref, y_hbm, op_ref, os_ref, ybuf, sem):
    i = pl.program_id(0)
    n_rows = 2 * LN_ROWS

    def for_rows(blk, slot, action):
        for k in range(2):
            def body(t, carry, k=k):
                action(_final_row_copy(dest_ref, y_hbm, ybuf, sem, blk, slot, k, t))
                return carry
            lax.fori_loop(0, LN_ROWS, body, 0, unroll=8)

    @pl.when(i == 0)
    def _():
        for_rows(0, 0, lambda cp: cp.start())

    @pl.when(i + 1 < pl.num_programs(0))
    def _():
        for_rows(i + 1, (i + 1) & 1, lambda cp: cp.start())

    slot = i & 1
    for_rows(i, slot, lambda cp: cp.wait())

    wt = wt_ref[...]
    r = base_ref[...] + wt[:, 0:1] * ybuf[slot, 0:LN_ROWS, :] + wt[:, 1:2] * ybuf[slot, LN_ROWS:n_rows, :]
    y = _layer_norm_rows(r, g_ref[...], b_ref[...])

    @pl.when(i < SEQ // LN_ROWS)
    def _():
        op_ref[...] = y

    @pl.when(i >= SEQ // LN_ROWS)
    def _():
        os_ref[...] = y


def _final(dest_flat, base, y_rows, wt, g, b):
    npb = SEQ // LN_ROWS
    return pl.pallas_call(
        _final_kernel,
        grid_spec=pltpu.PrefetchScalarGridSpec(
            num_scalar_prefetch=1,
            grid=(npb + 1,),
            in_specs=[
                pl.BlockSpec((LN_ROWS, D_MODEL), lambda i, d: (i, 0)),
                pl.BlockSpec((LN_ROWS, LANES), lambda i, d: (i, 0)),
                pl.BlockSpec((1, D_MODEL), lambda i, d: (0, 0)),
                pl.BlockSpec((1, D_MODEL), lambda i, d: (0, 0)),
                pl.BlockSpec(memory_space=pl.ANY),
            ],
            out_specs=[pl.BlockSpec((LN_ROWS, D_MODEL), lambda i, d: (jnp.minimum(i, npb - 1), 0)),
                       pl.BlockSpec((N_SAMPLE, D_MODEL), lambda i, d: (0, 0))],
            scratch_shapes=[pltpu.VMEM((2, 2 * LN_ROWS, D_MODEL), F32), pltpu.SemaphoreType.DMA((2,))],
        ),
        out_shape=[jax.ShapeDtypeStruct((SEQ, D_MODEL), F32),
                   jax.ShapeDtypeStruct((N_SAMPLE, D_MODEL), F32)],
        compiler_params=_cp(("arbitrary",)),
        name="combine_ln2",
    )(dest_flat, base, wt, g.reshape(1, D_MODEL), b.reshape(1, D_MODEL), y_rows)


def _cache_shift_kernel(c1, c2, n1, n2, o1, o2, buf, rsem, wsem, nsem):
    windows = A_WINDOWS[:2]
    new_copies = [pltpu.make_async_copy(n, o.at[0, :, pl.ds(w - DEC_SEQ, DEC_SEQ)], nsem.at[g])
                  for g, (n, o, w) in enumerate(zip((n1, n2), (o1, o2), windows))]
    for cp in new_copies:
        cp.start()

    pieces = []
    for c, o, w in zip((c1, c2), (o1, o2), windows):
        chunk = min(w - DEC_SEQ, SHIFT_CHUNK)
        for b in range(DEC_BATCH):
            for j in range((w - DEC_SEQ) // chunk):
                pieces.append((c, o, b, j * chunk, chunk))

    def rd(p):
        c, _, b, j0, n = pieces[p]
        s = p % SHIFT_SLOTS
        return pltpu.make_async_copy(c.at[0, b, pl.ds(DEC_SEQ + j0, n)], buf.at[s, pl.ds(0, n)], rsem.at[s])

    def wr(p):
        _, o, b, j0, n = pieces[p]
        s = p % SHIFT_SLOTS
        return pltpu.make_async_copy(buf.at[s, pl.ds(0, n)], o.at[0, b, pl.ds(j0, n)], wsem.at[s])

    n_pieces = len(pieces)
    for p in range(SHIFT_SLOTS):
        rd(p).start()
    for p in range(n_pieces):
        rd(p).wait()
        wr(p).start()
        if p >= 1 and p - 1 + SHIFT_SLOTS < n_pieces:
            wr(p - 1).wait()
            rd(p - 1 + SHIFT_SLOTS).start()
    for p in range(max(n_pieces - SHIFT_SLOTS, 0), n_pieces):
        wr(p).wait()
    for cp in new_copies:
        cp.wait()


def _cache_shift(caches, news):
    any_spec = pl.BlockSpec(memory_space=pl.ANY)
    vmem_spec = pl.BlockSpec(memory_space=pltpu.VMEM)
    return pl.pallas_call(
        _cache_shift_kernel,
        in_specs=[any_spec] * 2 + [vmem_spec] * 2,
        out_specs=[any_spec] * 2,
        out_shape=[jax.ShapeDtypeStruct(c.shape, c.dtype) for c in caches],
        scratch_shapes=[pltpu.VMEM((SHIFT_SLOTS, SHIFT_CHUNK + 1, 2, A_HEADS, A_HEAD_DIM), F32),
                        pltpu.SemaphoreType.DMA((SHIFT_SLOTS,)), pltpu.SemaphoreType.DMA((SHIFT_SLOTS,)),
                        pltpu.SemaphoreType.DMA((2,))],
        compiler_params=pltpu.CompilerParams(vmem_limit_bytes=VMEM_LIMIT),
        name="cache_shift",
    )(*caches, *news)


def kernel(x_prompt, x_sample, p_prompt, p_sample, cache_kv_a1, cache_kv_a2, cache_kv_a3, state_gla,
           ln_emb_g, ln_emb_b, ln1_g, ln1_b, ln2_g, ln2_b, w_in, w_gk2, b_gk, gla_norm_g, w_br_a, w_br_b,
           w_out, w_router_group, w_router_expert, w_gate, w_up, w_down, w_ple_gate, w_ple):
    xp = x_prompt.reshape(SEQ, D_MODEL)
    xs = x_sample.reshape(N_SAMPLE, D_MODEL)
    xn, xb = _ln_emb(xp, xs, ln_emb_g, ln_emb_b)

    w_t = jnp.transpose(w_in[0])
    rope = _rope_tables()
    u = _project(xb, w_t, 0, A_QKV_COLS, 2 * 3 * A_WIDTH // COL_TILE, rope, F32)
    u_gla = _project(xb, w_t, A_QKV_COLS, MAIN_COLS - A_QKV_COLS, 0, rope, BF16)
    gates = _project(xb, w_t[MAIN_COLS + GLA_LR:], 0, 2 * D_MODEL, 0, rope, BF16)
    gk = _gla_gate(xb, w_t[MAIN_COLS:MAIN_COLS + GLA_LR], w_gk2[0], b_gk[0])

    qkv_s = u[SEQ:].reshape(DEC_BATCH, DEC_SEQ, 9, A_HEADS, A_HEAD_DIM)
    kv_new = [jnp.stack([qkv_s[:, :, 3 + g], qkv_s[:, :, 6 + g]], axis=2) for g in range(3)]
    o_a, kv_s3 = _attn_prompt(u, kv_new[2], cache_kv_a3)
    c1 = cache_kv_a1.reshape(DEC_BATCH, A_REACH, 2, A_HEADS, A_HEAD_DIM)
    c2 = cache_kv_a2.reshape(DEC_BATCH, A_REACH, 4 * 2, A_HEADS, A_HEAD_DIM)
    c3 = cache_kv_a3.reshape(DEC_BATCH, A_REACH, 16 * 2, A_HEADS, A_HEAD_DIM)
    o_as = _attn_sample(qkv_s, c1, c2, c3)
    o_a = jnp.concatenate([o_a, o_as.reshape(N_SAMPLE, A_WIDTH).astype(BF16)], axis=0)

    o_g, st_t = _gla_prompt(u_gla, gk, gla_norm_g[0])

    def spad(a):
        a = a.astype(F32).reshape(DEC_BATCH, DEC_SEQ, a.shape[-1])
        return jnp.pad(a, ((0, 0), (0, GLA_SPAD - DEC_SEQ), (0, 0)))

    ug_s = u_gla[SEQ:]
    o_gs, st_s = _gla_sample(
        spad(ug_s[:, :GLA_QK]), spad(ug_s[:, GLA_QK:2 * GLA_QK]),
        spad(ug_s[:, 2 * GLA_QK:2 * GLA_QK + GLA_V]), spad(ug_s[:, 2 * GLA_QK + GLA_V:]),
        spad(gk[SEQ:]), gla_norm_g[0], state_gla[0])
    o_gs = o_gs[:, :DEC_SEQ].reshape(N_SAMPLE, GLA_V).astype(BF16)
    o_g = jnp.concatenate([o_g, o_gs], axis=0)

    x1 = _merge_ln1(o_a, o_g, gates, xn, ln1_g[0], ln1_b[0],
                    w_br_a[0].astype(BF16), w_br_b[0].astype(BF16), w_out[0].astype(BF16))

    p = jnp.concatenate([p_prompt[0].reshape(SEQ, PLE_DIM), p_sample[0].reshape(N_SAMPLE, PLE_DIM)], axis=0)
    base = _ple(x1, p.astype(BF16), w_ple_gate[0].astype(BF16), w_ple[0].astype(BF16))
    ridx, rwt = _router(x1, w_router_group[0], w_router_expert[0])
    dest, row_tok, blk_e, blk_next, n_valid = _moe_plan(ridx[:, :2])
    y_rows = _moe_experts(x1, row_tok, blk_e, blk_next, n_valid, w_gate[0], w_up[0], w_down[0])
    y_p, y_s = _final(dest.T.reshape(-1), base, y_rows, rwt, ln2_g[0], ln2_b[0])

    def kv_prompt(g):
        w = A_WINDOWS[g]
        k = u[SEQ - w:SEQ, (3 + g) * A_WIDTH:(4 + g) * A_WIDTH].reshape(w, A_HEADS, A_HEAD_DIM)
        v = u[SEQ - w:SEQ, (6 + g) * A_WIDTH:(7 + g) * A_WIDTH].reshape(w, A_HEADS, A_HEAD_DIM)
        return jnp.stack([k, v], axis=1)[None, None]

    kv_s1, kv_s2 = _cache_shift((cache_kv_a1, cache_kv_a2), kv_new[:2])

    gla_state_prompt = st_t[None, None]
    return (y_p.reshape(1, SEQ, D_MODEL), y_s.reshape(DEC_BATCH, DEC_SEQ, D_MODEL),
            kv_prompt(0), kv_prompt(1), kv_prompt(2), gla_state_prompt,
            kv_s1, kv_s2, kv_s3, st_s[None])
```

```python
import functools

import jax
import jax.numpy as jnp
import numpy as np
from jax import lax
from jax.experimental import pallas as pl
from jax.experimental.pallas import tpu as pltpu

F32 = jnp.float32
BF16 = jnp.bfloat16

D_MODEL = 2048
SEQ = 8192
DEC_BATCH = 32
DEC_SEQ = 4
PAST_LEN = 16384
N_SAMPLE = DEC_BATCH * DEC_SEQ
M_ROWS = SEQ + N_SAMPLE
A_WINDOWS = (128, 512, 2048)
A_DILATIONS = (1, 4, 16)
A_REACH = 128
A_HEADS = 8
A_HEAD_DIM = 128
A_WIDTH = A_HEADS * A_HEAD_DIM
ROT_DIM = 32
ROPE_THETA = 500000.0
A_QKV_COLS = 9 * A_WIDTH
GLA_HEADS = 4
GLA_DK = 256
GLA_DV = 512
GLA_LR = 16
GLA_TAU = 16.0
GLA_CHUNK = 64
GLA_QK = GLA_HEADS * GLA_DK
GLA_V = GLA_HEADS * GLA_DV
MAIN_COLS = A_QKV_COLS + 2 * GLA_QK + 2 * GLA_V
MOE_GROUPS = 4
MOE_PER_GROUP = 8
N_EXPERTS = 32
D_EXPERT = 1024
PLE_DIM = 256
LN_EPS = 1e-5
ALPHA = 2.0 ** 0.25

LANES = 128
VMEM_LIMIT = 56 * 1024 * 1024
ROW_TILE_PROJ = 1664
ROW_TILE = 832
MERGE_ROWS = 416
COL_TILE = 512
LN_ROWS = 128
MOE_BLK = 256
MOE_NBLK = (2 * M_ROWS) // MOE_BLK + N_EXPERTS
MOE_ROWS = MOE_NBLK * MOE_BLK
ATT_SB = 2048
GLA_ROWS = 512
ATT_VMEM_LIMIT = 60 * 1024 * 1024
SHIFT_SLOTS = 4
SHIFT_CHUNK = 511
ATT_UNROLL = 4
ATT_RESIDUE_MAJOR_D = 16
ATT_RESIDUE_PITCH = 264

def _cp(sem, vmem=VMEM_LIMIT):
    return pltpu.CompilerParams(dimension_semantics=sem, vmem_limit_bytes=vmem)


def _sigmoid(x):
    return 1.0 / (1.0 + jnp.exp(-x))


def _dot(a, b):
    return jnp.dot(a, b, preferred_element_type=F32)


def _dot_nt(a, b):
    return lax.dot_general(a, b, (((1,), (1,)), ((), ())), preferred_element_type=F32)


def _split3(x):
    h = x.astype(BF16)
    r = x - h.astype(F32)
    m = r.astype(BF16)
    l = (r - m.astype(F32)).astype(BF16)
    return h, m, l


def _layer_norm_rows(x, g, b):
    mu = jnp.mean(x, axis=-1, keepdims=True)
    xc = x - mu
    var = jnp.mean(xc * xc, axis=-1, keepdims=True)
    return xc * lax.rsqrt(var + LN_EPS) * g + b


def _ln_emb_kernel(xp_ref, xs_ref, g_ref, b_ref, of_ref, ob_ref):
    i = pl.program_id(0)
    x = jnp.where(i < SEQ // LN_ROWS, xp_ref[...], xs_ref[...])
    y = _layer_norm_rows(x, g_ref[...], b_ref[...])
    of_ref[...] = y
    ob_ref[...] = y.astype(BF16)


def _ln_emb(xp, xs, g, b):
    npb = SEQ // LN_ROWS
    return pl.pallas_call(
        _ln_emb_kernel,
        grid=(npb + 1,),
        in_specs=[
            pl.BlockSpec((LN_ROWS, D_MODEL), lambda i: (jnp.minimum(i, npb - 1), 0)),
            pl.BlockSpec((N_SAMPLE, D_MODEL), lambda i: (0, 0)),
            pl.BlockSpec((1, D_MODEL), lambda i: (0, 0)),
            pl.BlockSpec((1, D_MODEL), lambda i: (0, 0)),
        ],
        out_specs=[
            pl.BlockSpec((LN_ROWS, D_MODEL), lambda i: (i, 0)),
            pl.BlockSpec((LN_ROWS, D_MODEL), lambda i: (i, 0)),
        ],
        out_shape=[jax.ShapeDtypeStruct((M_ROWS, D_MODEL), F32),
                   jax.ShapeDtypeStruct((M_ROWS, D_MODEL), BF16)],
        compiler_params=_cp(("arbitrary",)),
        name="ln_emb",
    )(xp, xs, g.reshape(1, D_MODEL), b.reshape(1, D_MODEL))


def _proj_kernel(x_ref, w_ref, c_ref, s1_ref, s2_ref, o_ref, *, n_rope_blocks):
    j = pl.program_id(1)
    acc = _dot_nt(x_ref[...], w_ref[...].astype(BF16))

    @pl.when(j >= n_rope_blocks)
    def _():
        o_ref[...] = acc.astype(o_ref.dtype)

    @pl.when(j < n_rope_blocks)
    def _():
        c, s1, s2 = c_ref[...], s1_ref[...], s2_ref[...]
        for t in range(COL_TILE // LANES):
            a = acc[:, t * LANES:(t + 1) * LANES]
            o_ref[:, t * LANES:(t + 1) * LANES] = (
                a * c + pltpu.roll(a, LANES - ROT_DIM // 2, 1) * s1 + pltpu.roll(a, ROT_DIM // 2, 1) * s2
            ).astype(o_ref.dtype)


def _project(xb, w_t, row0, n_cols, n_rope_blocks, rope, out_dtype):
    c, s1, s2 = rope
    tm, tn = ROW_TILE_PROJ, COL_TILE
    jb0 = row0 // tn
    return pl.pallas_call(
        functools.partial(_proj_kernel, n_rope_blocks=n_rope_blocks),
        grid=(M_ROWS // tm, n_cols // tn),
        in_specs=[
            pl.BlockSpec((tm, D_MODEL), lambda i, j: (i, 0)),
            pl.BlockSpec((tn, D_MODEL), lambda i, j: (jb0 + j, 0)),
            pl.BlockSpec((tm, LANES), lambda i, j: (i, 0)),
            pl.BlockSpec((tm, LANES), lambda i, j: (i, 0)),
            pl.BlockSpec((tm, LANES), lambda i, j: (i, 0)),
        ],
        out_specs=pl.BlockSpec((tm, tn), lambda i, j: (i, j)),
        out_shape=jax.ShapeDtypeStruct((M_ROWS, n_cols), out_dtype),
        compiler_params=_cp(("arbitrary", "arbitrary")),
        name="in_proj",
    )(xb, w_t, c, s1, s2)


def _rope_tables():
    half = ROT_DIM // 2
    inv = np.power(np.float32(ROPE_THETA), -np.arange(half, dtype=np.float32) * np.float32(2.0 / ROT_DIM))
    pos = np.concatenate([np.arange(SEQ), np.tile(PAST_LEN + np.arange(DEC_SEQ), DEC_BATCH)]).astype(np.float32)
    ang = (pos[:, None] * inv[None, :].astype(np.float32)).astype(np.float32)
    cos, sin = np.cos(ang).astype(np.float32), np.sin(ang).astype(np.float32)
    ones = np.ones((M_ROWS, LANES - ROT_DIM), np.float32)
    zeros = np.zeros((M_ROWS, LANES - ROT_DIM), np.float32)
    zh = np.zeros((M_ROWS, half), np.float32)
    c = np.concatenate([cos, cos, ones], axis=1)
    s1 = np.concatenate([-sin, zh, zeros], axis=1)
    s2 = np.concatenate([zh, sin, zeros], axis=1)
    return jnp.asarray(c), jnp.asarray(s1), jnp.asarray(s2)


def _gk_kernel(x_ref, wlr_ref, wgk_ref, b_ref, o_ref):
    blr = _dot_nt(x_ref[...], wlr_ref[...])
    z = _dot(blr.astype(BF16), wgk_ref[...]) + b_ref[...]
    o_ref[...] = (jnp.minimum(z, 0.0) - jnp.log1p(jnp.exp(-jnp.abs(z)))) * (1.0 / GLA_TAU)


def _gla_gate(xb, w_lr_t, w_gk2, b_gk):
    wlr = jnp.zeros((LANES, D_MODEL), BF16).at[:GLA_LR].set(w_lr_t.astype(BF16))
    wgk = jnp.zeros((LANES, GLA_QK), BF16).at[:GLA_LR].set(w_gk2.astype(BF16))
    tm = ROW_TILE
    return pl.pallas_call(
        _gk_kernel,
        grid=(M_ROWS // tm,),
        in_specs=[
            pl.BlockSpec((tm, D_MODEL), lambda i: (i, 0)),
            pl.BlockSpec((LANES, D_MODEL), lambda i: (0, 0)),
            pl.BlockSpec((LANES, GLA_QK), lambda i: (0, 0)),
            pl.BlockSpec((1, GLA_QK), lambda i: (0, 0)),
        ],
        out_specs=pl.BlockSpec((tm, GLA_QK), lambda i: (i, 0)),
        out_shape=jax.ShapeDtypeStruct((M_ROWS, GLA_QK), F32),
        compiler_params=_cp(("arbitrary",)),
        name="gla_gate",
    )(xb, wlr, wgk, b_gk.reshape(1, GLA_QK))


class _CacheShiftInBackground:
    def __init__(self, step, n_steps, pieces, news, buf, rsem, wsem, nsem):
        self.step, self.n_steps, self.pieces, self.news = step, n_steps, pieces, news
        self.buf, self.rsem, self.wsem, self.nsem = buf, rsem, wsem, nsem

    def _copies(self, step):
        out = []
        for slot, (cache, dst, b, pos0, n) in enumerate(self.pieces(step)):
            stage = self.buf.at[slot, pl.ds(0, n)]
            rd = pltpu.make_async_copy(cache.at[0, b, pl.ds(DEC_SEQ + pos0, n)], stage, self.rsem.at[slot])
            wr = pltpu.make_async_copy(stage, dst.at[0, b, pl.ds(pos0, n)], self.wsem.at[slot])
            out.append((rd, wr))
        return out

    def _new_rows(self):
        return [pltpu.make_async_copy(new, dst.at[0, :, pl.ds(w - DEC_SEQ, DEC_SEQ)], self.nsem.at[i])
                for i, (new, dst, w) in enumerate(self.news)]

    def top_of_step(self):
        @pl.when(self.step == 0)
        def _():
            for cp in self._new_rows():
                cp.start()

        @pl.when(self.step > 0)
        def _():
            for _, wr in self._copies(self.step - 1):
                wr.wait()

        for rd, _ in self._copies(self.step):
            rd.start()

    def turn_reads_into_writes(self):
        for rd, wr in self._copies(self.step):
            rd.wait()
            wr.start()

    def end_of_step(self):
        @pl.when(self.step == self.n_steps - 1)
        def _():
            for _, wr in self._copies(self.step):
                wr.wait()
            for cp in self._new_rows():
                cp.wait()


def _gla_chunks(heads, gn, causal, tri_b, eye):
    n = heads[0][3].shape[0]
    rid = lax.broadcasted_iota(jnp.int32, (n + GLA_TPAD, GLA_DK), 0)
    bs = []
    for _, _, _, g, _, _ in heads:
        g1, g2, g3 = _split3(g)
        bs.append(_dot(tri_b, g1) + _dot(tri_b, g2) + _dot(tri_b, g3))
    ops = []
    for (q, k, v, _, _, _), b in zip(heads, bs):
        q, k = q.astype(F32), k.astype(F32)
        b_last = b[n - 1:n, :]
        q_dec = (q * ((GLA_DK ** -0.5) * jnp.exp(b))).astype(BF16)
        k_inv = (k * jnp.exp(-b)).astype(BF16)
        k_dec = k * jnp.exp(b_last - b)
        d1, d2, d3 = _split3(jnp.exp(b_last))
        rows = jnp.concatenate([k_dec, jnp.zeros((GLA_TPAD, GLA_DK), F32)], axis=0)
        rows = jnp.where(rid == n, d1.astype(F32), rows)
        rows = jnp.where(rid == n + 1, d2.astype(F32), rows)
        rows = jnp.where(rid == n + 2, d3.astype(F32), rows)
        ops.append((q_dec, k_inv, rows.astype(BF16), v.astype(BF16)))
    mids = []
    for q_dec, k_inv, rows, _ in ops:
        mids.append((jnp.where(causal, _dot_nt(q_dec, k_inv), 0.0).astype(BF16), _dot_nt(eye, rows)))
    outs = []
    for (q_dec, _, _, vb), (att, cols), (_, _, _, _, br, s0) in zip(ops, mids, heads):
        o = _dot(att, vb) + _dot(q_dec, s0.astype(BF16))
        dcol = cols[:, n:n + 1] + cols[:, n + 1:n + 2] + cols[:, n + 2:n + 3]
        s1 = s0 * dcol + _dot(cols[:, :n].astype(BF16), vb)
        br = br.astype(F32)
        ms = jnp.mean(o * o, axis=-1, keepdims=True)
        outs.append((o * lax.rsqrt(ms + LN_EPS) * gn * (br * _sigmoid(br)), s1))
    return outs


GLA_TPAD = 16


def _gla_consts(n):
    r = lax.broadcasted_iota(jnp.int32, (n, n), 0)
    c = lax.broadcasted_iota(jnp.int32, (n, n), 1)
    causal = r >= c
    er = lax.broadcasted_iota(jnp.int32, (GLA_DK, GLA_DK), 0)
    ec = lax.broadcasted_iota(jnp.int32, (GLA_DK, GLA_DK), 1)
    return causal, jnp.where(causal, 1.0, 0.0).astype(BF16), jnp.where(er == ec, 1.0, 0.0).astype(BF16)


def _gla_prompt_kernel(q_ref, k_ref, v_ref, br_ref, g_ref, gn_ref, n1_ref, n2_ref, c1_hbm, c2_hbm,
                       o_ref, st_ref, kv1_hbm, kv2_hbm, cbuf, rsem, wsem, nsem):
    step = pl.program_id(0)
    n_steps = SEQ // GLA_ROWS
    per_step = DEC_BATCH // n_steps

    def pieces(s):
        out = []
        for cache, dst, w in ((c1_hbm, kv1_hbm, A_WINDOWS[0]), (c2_hbm, kv2_hbm, A_WINDOWS[1])):
            out += [(cache, dst, s * per_step + j, 0, w - DEC_SEQ) for j in range(per_step)]
        return out

    shifter = _CacheShiftInBackground(
        step, n_steps, pieces, [(n1_ref, kv1_hbm, A_WINDOWS[0]), (n2_ref, kv2_hbm, A_WINDOWS[1])],
        cbuf, rsem, wsem, nsem)
    shifter.top_of_step()

    @pl.when(step == 0)
    def _():
        st_ref[...] = jnp.zeros_like(st_ref)

    causal, tri_b, eye = _gla_consts(GLA_CHUNK)
    gn = gn_ref[...]

    def body(cc, carry):
        rows = pl.ds(pl.multiple_of(cc * GLA_CHUNK, GLA_CHUNK), GLA_CHUNK)
        heads = []
        for h in range(GLA_HEADS):
            ks = slice(h * GLA_DK, (h + 1) * GLA_DK)
            vs = slice(h * GLA_DV, (h + 1) * GLA_DV)
            heads.append((q_ref[rows, ks], k_ref[rows, ks], v_ref[rows, vs], g_ref[rows, ks],
                          br_ref[rows, vs], st_ref[h]))
        for h, (o, s1) in enumerate(_gla_chunks(heads, gn, causal, tri_b, eye)):
            st_ref[h] = s1
            o_ref[rows, h * GLA_DV:(h + 1) * GLA_DV] = o.astype(BF16)
        return carry

    half = GLA_ROWS // GLA_CHUNK // 2
    lax.fori_loop(0, half, body, 0)
    shifter.turn_reads_into_writes()
    lax.fori_loop(half, 2 * half, body, 0)
    shifter.end_of_step()


def _gla_prompt(u, gk, gnorm, news, caches):
    n_steps = SEQ // GLA_ROWS
    n_slots = 2 * (DEC_BATCH // n_steps)
    any_spec = pl.BlockSpec(memory_space=pl.ANY)
    return pl.pallas_call(
        _gla_prompt_kernel,
        grid=(n_steps,),
        in_specs=[
            pl.BlockSpec((GLA_ROWS, GLA_QK), lambda c: (c, 0)),
            pl.BlockSpec((GLA_ROWS, GLA_QK), lambda c: (c, 1)),
            pl.BlockSpec((GLA_ROWS, GLA_V), lambda c: (c, 1)),
            pl.BlockSpec((GLA_ROWS, GLA_V), lambda c: (c, 2)),
            pl.BlockSpec((GLA_ROWS, GLA_QK), lambda c: (c, 0)),
            pl.BlockSpec((1, GLA_DV), lambda c: (0, 0)),
            pl.BlockSpec(news[0].shape, lambda c: (0, 0, 0, 0, 0)),
            pl.BlockSpec(news[1].shape, lambda c: (0, 0, 0, 0, 0)),
            any_spec, any_spec,
        ],
        out_specs=[
            pl.BlockSpec((GLA_ROWS, GLA_V), lambda c: (c, 0)),
            pl.BlockSpec((GLA_HEADS, GLA_DK, GLA_DV), lambda c: (0, 0, 0)),
            any_spec, any_spec,
        ],
        out_shape=[jax.ShapeDtypeStruct((SEQ, GLA_V), BF16),
                   jax.ShapeDtypeStruct((GLA_HEADS, GLA_DK, GLA_DV), F32),
                   jax.ShapeDtypeStruct(caches[0].shape, caches[0].dtype),
                   jax.ShapeDtypeStruct(caches[1].shape, caches[1].dtype)],
        scratch_shapes=[pltpu.VMEM((n_slots, A_WINDOWS[1] - DEC_SEQ, 2, A_HEADS, A_HEAD_DIM), F32),
                        pltpu.SemaphoreType.DMA((n_slots,)), pltpu.SemaphoreType.DMA((n_slots,)),
                        pltpu.SemaphoreType.DMA((2,))],
        compiler_params=_cp(("arbitrary",)),
        name="gla_prompt",
    )(u, u, u, u, gk, gnorm.reshape(1, GLA_DV), news[0], news[1], caches[0], caches[1])


GLA_SPAD = 16


def _gla_sample_kernel(q_ref, k_ref, v_ref, br_ref, g_ref, gn_ref, s0_ref, o_ref, s1_ref):
    causal, tri_b, eye = _gla_consts(GLA_SPAD)
    gn = gn_ref[...]
    heads = []
    for h in range(GLA_HEADS):
        ks = slice(h * GLA_DK, (h + 1) * GLA_DK)
        vs = slice(h * GLA_DV, (h + 1) * GLA_DV)
        heads.append((q_ref[0, :, ks], k_ref[0, :, ks], v_ref[0, :, vs], g_ref[0, :, ks],
                      br_ref[0, :, vs], s0_ref[0, h]))
    for h, (o, s1) in enumerate(_gla_chunks(heads, gn, causal, tri_b, eye)):
        s1_ref[0, h] = s1
        o_ref[0, :, h * GLA_DV:(h + 1) * GLA_DV] = o


def _gla_sample(q, k, v, br, g, gnorm, s0):
    def spec(c):
        return pl.BlockSpec((1, GLA_SPAD, c), lambda b: (b, 0, 0))
    st_spec = pl.BlockSpec((1, GLA_HEADS, GLA_DK, GLA_DV), lambda b: (b, 0, 0, 0))
    return pl.pallas_call(
        _gla_sample_kernel,
        grid=(DEC_BATCH,),
        in_specs=[spec(GLA_QK), spec(GLA_QK), spec(GLA_V), spec(GLA_V), spec(GLA_QK),
                  pl.BlockSpec((1, GLA_DV), lambda b: (0, 0)), st_spec],
        out_specs=[spec(GLA_V), st_spec],
        out_shape=[jax.ShapeDtypeStruct((DEC_BATCH, GLA_SPAD, GLA_V), F32),
                   jax.ShapeDtypeStruct((DEC_BATCH, GLA_HEADS, GLA_DK, GLA_DV), F32)],
        compiler_params=_cp(("arbitrary",)),
        name="gla_sample",
    )(q, k, v, br, g, gnorm.reshape(1, GLA_DV), s0)


def _attn_prompt_kernel(q0, q1, q2, kc0, kp0, vc0, vp0, kc1, kp1, vc1, vp1, kc2, kp2, vc2, vp2,
                        new3_ref, cache3_hbm, o_ref, kv3_hbm,
                        kb0, vb0, kb1, vb1, kb2, vb2, acc_ref, m_ref, l_ref, cbuf, rsem, wsem, nsem):
    sb = pl.program_id(0)
    step = sb * pl.num_programs(1) + pl.program_id(1)

    def pieces(s):
        return [(cache3_hbm, kv3_hbm, s, k * SHIFT_CHUNK, SHIFT_CHUNK) for k in range(SHIFT_SLOTS)]

    shifter = _CacheShiftInBackground(step, DEC_BATCH, pieces, [(new3_ref, kv3_hbm, A_WINDOWS[2])],
                                      cbuf, rsem, wsem, nsem)
    shifter.top_of_step()
    groups = ((q0, kc0, kp0, vc0, vp0, kb0, vb0, 1),
              (q1, kc1, kp1, vc1, vp1, kb1, vb1, 4),
              (q2, kc2, kp2, vc2, vp2, kb2, vb2, 16))
    scale = A_HEAD_DIM ** -0.5
    n_win = 2 * A_REACH
    row = lax.broadcasted_iota(jnp.int32, (A_REACH, n_win), 0)
    col = lax.broadcasted_iota(jnp.int32, (A_REACH, n_win), 1)
    in_reach = jnp.logical_and(col >= row, col <= row + A_REACH)
    in_prev = col < A_REACH
    ones = jnp.ones((n_win, LANES), BF16)

    for gi, (q_ref, kc, kp, vc, vp, kb, vb, d) in enumerate(groups):
        win = A_REACH * d
        by_residue = d == ATT_RESIDUE_MAJOR_D
        if by_residue:
            def stage(jj, carry, kp=kp, kc=kc, vp=vp, vc=vc, kb=kb, vb=vb, d=d):
                src = pl.ds(pl.multiple_of(jj * d, d), d)
                for buf, prev, cur in ((kb, kp, kc), (vb, vp, vc)):
                    buf[pl.ds(jj, d, stride=ATT_RESIDUE_PITCH), :] = prev[src, :]
                    buf[pl.ds(A_REACH + jj, d, stride=ATT_RESIDUE_PITCH), :] = cur[src, :]
                return carry
            lax.fori_loop(0, A_REACH, stage, 0, unroll=8)
        else:
            kb[0:win, :] = kp[...]
            kb[win:win + ATT_SB, :] = kc[...]
            vb[0:win, :] = vp[...]
            vb[win:win + ATT_SB, :] = vc[...]
        shift = {1: 0, 4: 2, 16: 4}[d]

        def tiles(it, carry, q_ref=q_ref, kb=kb, vb=vb, d=d, win=win, shift=shift, gi=gi, by_residue=by_residue):
            rows, scores = [], []
            for j in range(ATT_UNROLL):
                idx = it * ATT_UNROLL + j
                r = idx & (d - 1)
                nb = idx >> shift
                start = r + nb * win
                if d == 1:
                    start = pl.multiple_of(start, A_REACH)
                    qrows = pl.ds(start, A_REACH)
                    wrows = pl.ds(start, n_win)
                elif by_residue:
                    qrows = pl.ds(start, A_REACH, stride=d)
                    wrows = pl.ds(pl.multiple_of(r * ATT_RESIDUE_PITCH, 8), n_win)
                else:
                    qrows = pl.ds(start, A_REACH, stride=d)
                    wrows = pl.ds(start, n_win, stride=d)
                prev_bias = jnp.where(jnp.logical_or(sb > 0, nb > 0), 0.0, -jnp.inf)
                rows.append((qrows, wrows, prev_bias))
                scores.append(_dot_nt(q_ref[qrows, :].astype(BF16), kb[wrows, :].astype(BF16)))
            probs = []
            for s, (_, _, prev_bias) in zip(scores, rows):
                s = jnp.where(in_reach, s * scale, -jnp.inf) + jnp.where(in_prev, prev_bias, 0.0)
                m_t = jnp.max(s, axis=1, keepdims=True)
                probs.append((jnp.exp(s - m_t).astype(BF16), jnp.broadcast_to(m_t, (A_REACH, LANES))))
            sums = []
            for (p, _), (_, wrows, _) in zip(probs, rows):
                sums.append((_dot(p, ones), _dot(p, vb[wrows, :].astype(BF16))))
            for (l_t, num), (_, m_t), (qrows, _, _) in zip(sums, probs, rows):
                if gi == 0:
                    acc_ref[qrows, :] = num
                    m_ref[qrows, :] = m_t
                    l_ref[qrows, :] = l_t
                else:
                    m_o = m_ref[qrows, :]
                    m_n = jnp.maximum(m_o, m_t)
                    a = jnp.exp(m_o - m_n)
                    b = jnp.exp(m_t - m_n)
                    acc_ref[qrows, :] = a * acc_ref[qrows, :] + b * num
                    l_ref[qrows, :] = a * l_ref[qrows, :] + b * l_t
                    m_ref[qrows, :] = m_n
            return carry

        lax.fori_loop(0, ATT_SB // A_REACH // ATT_UNROLL, tiles, 0)
        if gi == 1:
            shifter.turn_reads_into_writes()

    o_ref[...] = (acc_ref[...] / l_ref[...]).astype(BF16)
    shifter.end_of_step()


def _attn_prompt(u, new3, cache3):
    nh = A_HEADS
    assert (SEQ // ATT_SB) * nh == DEC_BATCH and SHIFT_SLOTS * SHIFT_CHUNK == A_WINDOWS[2] - DEC_SEQ
    in_specs, scratch = [], []
    for g in range(3):
        in_specs.append(pl.BlockSpec((ATT_SB, LANES), lambda sb, h, g=g: (sb, g * nh + h)))
    for g, d in enumerate(A_DILATIONS):
        win = A_REACH * d
        per = ATT_SB // win
        for part in (3, 6):
            cb = part * nh + g * nh
            in_specs.append(pl.BlockSpec((ATT_SB, LANES), lambda sb, h, cb=cb: (sb, cb + h)))
            in_specs.append(pl.BlockSpec(
                (win, LANES), lambda sb, h, cb=cb, per=per: (jnp.maximum(sb * per - 1, 0), cb + h)))
        buf_rows = d * ATT_RESIDUE_PITCH if d == ATT_RESIDUE_MAJOR_D else win + ATT_SB
        scratch += [pltpu.VMEM((buf_rows, LANES), F32), pltpu.VMEM((buf_rows, LANES), F32)]
    scratch += [pltpu.VMEM((ATT_SB, LANES), F32)] * 3
    scratch += [pltpu.VMEM((SHIFT_SLOTS, SHIFT_CHUNK + 1, 2, A_HEADS, A_HEAD_DIM), F32),
                pltpu.SemaphoreType.DMA((SHIFT_SLOTS,)), pltpu.SemaphoreType.DMA((SHIFT_SLOTS,)),
                pltpu.SemaphoreType.DMA((1,))]
    in_specs += [pl.BlockSpec(new3.shape, lambda sb, h: (0, 0, 0, 0, 0)), pl.BlockSpec(memory_space=pl.ANY)]
    return pl.pallas_call(
        _attn_prompt_kernel,
        grid=(SEQ // ATT_SB, nh),
        in_specs=in_specs,
        out_specs=[pl.BlockSpec((ATT_SB, LANES), lambda sb, h: (sb, h)), pl.BlockSpec(memory_space=pl.ANY)],
        out_shape=[jax.ShapeDtypeStruct((SEQ, A_WIDTH), BF16), jax.ShapeDtypeStruct(cache3.shape, cache3.dtype)],
        scratch_shapes=scratch,
        compiler_params=_cp(("arbitrary", "arbitrary"), vmem=ATT_VMEM_LIMIT),
        name="attn_prompt",
    )(*([u] * 15), new3, cache3)


def _attn_sample_kernel(qkv_ref, c1_ref, c2_ref, c3_ref, o_ref):
    scale = A_HEAD_DIM ** -0.5
    pos = lax.broadcasted_iota(jnp.int32, (A_REACH, 1, 1), 0)

    def merge(state, m_t, l_t, num):
        if state is None:
            return m_t, l_t, num
        m_o, l_o, n_o = state
        m_n = jnp.maximum(m_o, m_t)
        a, b = jnp.exp(m_o - m_n), jnp.exp(m_t - m_n)
        return m_n, a * l_o + b * l_t, a * n_o + b * num

    for s in range(DEC_SEQ):
        state = None
        for g in range(3):
            q = qkv_ref[0, s, g]
            if g == 0:
                kt, vt = c1_ref[0, :, 0], c1_ref[0, :, 1]
                valid = pos >= s
                new_rows = range(s + 1)
            else:
                c_ref = c2_ref if g == 1 else c3_ref
                kt, vt = c_ref[0, :, 2 * s], c_ref[0, :, 2 * s + 1]
                valid = None
                new_rows = (s,)
            sc = jnp.sum(kt * q[None], axis=-1, keepdims=True) * scale
            if valid is not None:
                sc = jnp.where(valid, sc, -jnp.inf)
            s_new = [jnp.sum(qkv_ref[0, t, 3 + g] * q, axis=-1, keepdims=True) * scale for t in new_rows]
            m_t = jnp.max(sc, axis=0)
            for sn in s_new:
                m_t = jnp.maximum(m_t, sn)
            p = jnp.exp(sc - m_t[None])
            l_t = jnp.sum(p, axis=0)
            num = jnp.sum(p * vt, axis=0)
            for t, sn in zip(new_rows, s_new):
                pn = jnp.exp(sn - m_t)
                l_t = l_t + pn
                num = num + pn * qkv_ref[0, t, 6 + g]
            state = merge(state, m_t, l_t, num)
        o_ref[0, s] = state[2] / state[1]


def _attn_sample(qkv_s, c1, c2, c3):
    tile = (A_HEADS, A_HEAD_DIM)
    return pl.pallas_call(
        _attn_sample_kernel,
        grid=(DEC_BATCH,),
        in_specs=[
            pl.BlockSpec((1, DEC_SEQ, 9) + tile, lambda b: (b, 0, 0, 0, 0)),
            pl.BlockSpec((1, A_REACH, 2) + tile, lambda b: (b, 0, 0, 0, 0)),
            pl.BlockSpec((1, A_REACH, 2 * DEC_SEQ) + tile, lambda b: (b, 0, 0, 0, 0)),
            pl.BlockSpec((1, A_REACH, 2 * DEC_SEQ) + tile, lambda b: (b, 0, 0, 0, 0)),
        ],
        out_specs=pl.BlockSpec((1, DEC_SEQ) + tile, lambda b: (b, 0, 0, 0)),
        out_shape=jax.ShapeDtypeStruct((DEC_BATCH, DEC_SEQ) + tile, F32),
        compiler_params=_cp(("arbitrary",)),
        name="attn_sample",
    )(qkv_s, c1, c2, c3)


def _load_resident(step, pairs):
    @pl.when(step == 0)
    def _():
        for src, dst in pairs:
            pltpu.sync_copy(src, dst)


def _merge_ln1_kernel(oa_ref, og_ref, gt_ref, xn_ref, g_ref, b_ref, wa_hbm, wb_hbm, wo_hbm,
                      x1_ref, wa, wb, wo, mixed, resid):
    _load_resident(pl.program_id(0), ((wa_hbm, wa), (wb_hbm, wb), (wo_hbm, wo)))
    nj = D_MODEL // COL_TILE
    oa, og = oa_ref[...], og_ref[...]
    for t in range(nj):
        cs = slice(t * COL_TILE, (t + 1) * COL_TILE)
        gs = slice(D_MODEL + t * COL_TILE, D_MODEL + (t + 1) * COL_TILE)
        ya = _dot(oa, wa[:, cs])
        yb = _dot(og, wb[:, cs])
        ga, gb = gt_ref[:, cs].astype(F32), gt_ref[:, gs].astype(F32)
        mixed[:, cs] = (_sigmoid(ga) * ya + _sigmoid(gb) * yb).astype(BF16)
    mx = mixed[...]
    for t in range(nj):
        cs = slice(t * COL_TILE, (t + 1) * COL_TILE)
        resid[:, cs] = ALPHA * xn_ref[:, cs] + _dot(mx, wo[:, cs])
    x1_ref[...] = _layer_norm_rows(resid[...], g_ref[...], b_ref[...])


def _merge_ln1(o_a, o_g, gates, xn, g, b, wa, wb, wo):
    tm = MERGE_ROWS
    any_spec = pl.BlockSpec(memory_space=pl.ANY)
    return pl.pallas_call(
        _merge_ln1_kernel,
        grid=(M_ROWS // tm,),
        in_specs=[
            pl.BlockSpec((tm, A_WIDTH), lambda i: (i, 0)),
            pl.BlockSpec((tm, GLA_V), lambda i: (i, 0)),
            pl.BlockSpec((tm, 2 * D_MODEL), lambda i: (i, 0)),
            pl.BlockSpec((tm, D_MODEL), lambda i: (i, 0)),
            pl.BlockSpec((1, D_MODEL), lambda i: (0, 0)),
            pl.BlockSpec((1, D_MODEL), lambda i: (0, 0)),
            any_spec, any_spec, any_spec,
        ],
        out_specs=pl.BlockSpec((tm, D_MODEL), lambda i: (i, 0)),
        out_shape=jax.ShapeDtypeStruct((M_ROWS, D_MODEL), F32),
        scratch_shapes=[pltpu.VMEM((A_WIDTH, D_MODEL), BF16), pltpu.VMEM((GLA_V, D_MODEL), BF16),
                        pltpu.VMEM((D_MODEL, D_MODEL), BF16),
                        pltpu.VMEM((tm, D_MODEL), BF16), pltpu.VMEM((tm, D_MODEL), F32)],
        compiler_params=_cp(("arbitrary",)),
        name="merge_ln1",
    )(o_a, o_g, gates, xn, g.reshape(1, D_MODEL), b.reshape(1, D_MODEL), wa, wb, wo)


def _ple_kernel(x_ref, p_ref, wg_hbm, wp_hbm, o_ref, wg, wp):
    _load_resident(pl.program_id(0), ((wg_hbm, wg), (wp_hbm, wp)))
    xb = x_ref[...].astype(BF16)
    p = p_ref[...]
    for t in range(D_MODEL // COL_TILE):
        cs = slice(t * COL_TILE, (t + 1) * COL_TILE)
        gate = _sigmoid(_dot(xb, wg[:, cs]))
        o_ref[:, cs] = ALPHA * x_ref[:, cs] + gate * _dot(p, wp[:, cs])


def _ple(x1, p, wpg, wple):
    tm = ROW_TILE
    any_spec = pl.BlockSpec(memory_space=pl.ANY)
    return pl.pallas_call(
        _ple_kernel,
        grid=(M_ROWS // tm,),
        in_specs=[
            pl.BlockSpec((tm, D_MODEL), lambda i: (i, 0)),
            pl.BlockSpec((tm, PLE_DIM), lambda i: (i, 0)),
            any_spec, any_spec,
        ],
        out_specs=pl.BlockSpec((tm, D_MODEL), lambda i: (i, 0)),
        out_shape=jax.ShapeDtypeStruct((M_ROWS, D_MODEL), F32),
        scratch_shapes=[pltpu.VMEM((D_MODEL, D_MODEL), BF16), pltpu.VMEM((PLE_DIM, D_MODEL), BF16)],
        compiler_params=_cp(("arbitrary",)),
        name="ple_residual",
    )(x1, p, wpg, wple)


def _router_kernel(x_ref, w_ref, idx_ref, wt_ref):
    xh, xm, _ = _split3(x_ref[...])
    wh, wm, _ = _split3(w_ref[...])
    logits = _dot(xh, wh) + _dot(xh, wm) + _dot(xm, wh)
    lane_i = lax.broadcasted_iota(jnp.int32, (x_ref.shape[0], LANES), 1)
    lane = lane_i.astype(F32)
    lane_group = (lane_i >> 3).astype(F32)
    neg = -jnp.inf

    def top1(v):
        mx = jnp.max(v, axis=1, keepdims=True)
        ix = jnp.min(jnp.where(v == mx, lane, float(LANES)), axis=1, keepdims=True)
        return mx, ix

    gl = jnp.where(lane_i < MOE_GROUPS, logits[:, :LANES], neg)
    g_max, g_idx = top1(gl)
    g_p = 1.0 / jnp.sum(jnp.exp(gl - g_max), axis=1, keepdims=True)
    el = logits[:, LANES:]
    in_group = jnp.logical_and(lane_i < N_EXPERTS, lane_group == g_idx)
    e1v = jnp.where(in_group, el, neg)
    m1, i1 = top1(e1v)
    m2, i2 = top1(jnp.where(lane == i1, neg, e1v))
    t = jnp.exp(m2 - m1)
    w1 = g_p / (1.0 + t)
    w2 = g_p * t / (1.0 + t)
    idx_ref[...] = jnp.where(lane_i == 0, i1, jnp.where(lane_i == 1, i2, 0.0)).astype(jnp.int32)
    wt_ref[...] = jnp.where(lane_i == 0, w1, jnp.where(lane_i == 1, w2, 0.0))


def _router(x1, w_r1, w_r2):
    w = jnp.zeros((D_MODEL, 2 * LANES), F32)
    w = w.at[:, :MOE_GROUPS].set(w_r1).at[:, LANES:LANES + N_EXPERTS].set(w_r2)
    tm = ROW_TILE
    return pl.pallas_call(
        _router_kernel,
        grid=(M_ROWS // tm,),
        in_specs=[pl.BlockSpec((tm, D_MODEL), lambda i: (i, 0)),
                  pl.BlockSpec((D_MODEL, 2 * LANES), lambda i: (0, 0))],
        out_specs=[pl.BlockSpec((tm, LANES), lambda i: (i, 0)),
                   pl.BlockSpec((tm, LANES), lambda i: (i, 0))],
        out_shape=[jax.ShapeDtypeStruct((M_ROWS, LANES), jnp.int32),
                   jax.ShapeDtypeStruct((M_ROWS, LANES), F32)],
        compiler_params=_cp(("arbitrary",)),
        name="router",
    )(x1, w)


WEIGHT_DMA_PRIORITY = 1


def _moe_kernel(be_ref, nx_ref, nv_ref, tok_ref, x_hbm, wg_hbm, wu_hbm, wd_hbm, y_ref,
                sg, su, sd, wgb, wub, wdb, xbuf, wsem, xsem):
    i = pl.program_id(0)
    nv = nv_ref[0]
    e = be_ref[i]

    def weight_copies(ex):
        return (pltpu.make_async_copy(wg_hbm.at[ex], sg, wsem.at[0]),
                pltpu.make_async_copy(wu_hbm.at[ex], su, wsem.at[1]),
                pltpu.make_async_copy(wd_hbm.at[ex], sd, wsem.at[2]))

    def for_rows(blk, slot, action):
        def body(r, carry):
            src = tok_ref[blk * MOE_BLK + r]
            action(pltpu.make_async_copy(x_hbm.at[pl.ds(src, 1)], xbuf.at[slot, pl.ds(r, 1)], xsem.at[slot]))
            return carry
        lax.fori_loop(0, MOE_BLK, body, 0, unroll=8)

    @pl.when(i == 0)
    def _():
        for cp in weight_copies(e):
            cp.start(priority=WEIGHT_DMA_PRIORITY)
        for_rows(0, 0, lambda cp: cp.start())

    @pl.when(i + 1 < nv)
    def _():
        for_rows(i + 1, (i + 1) & 1, lambda cp: cp.start())

    first_of_expert = jnp.logical_or(i == 0, e != be_ref[jnp.maximum(i - 1, 0)])

    @pl.when(jnp.logical_and(i < nv, first_of_expert))
    def _():
        for cp in weight_copies(e):
            cp.wait()
        for r in range(0, D_MODEL, MOE_BLK):
            wgb[r:r + MOE_BLK, :] = sg[r:r + MOE_BLK, :].astype(BF16)
            wub[r:r + MOE_BLK, :] = su[r:r + MOE_BLK, :].astype(BF16)
        for r in range(0, D_EXPERT, MOE_BLK):
            wdb[r:r + MOE_BLK, :] = sd[r:r + MOE_BLK, :].astype(BF16)

        @pl.when(nx_ref[i] >= 0)
        def _():
            for cp in weight_copies(nx_ref[i]):
                cp.start(priority=WEIGHT_DMA_PRIORITY)

    @pl.when(i < nv)
    def _():
        slot = i & 1
        for_rows(i, slot, lambda cp: cp.wait())
        x = xbuf[slot].astype(BF16)
        a = _dot(x, wgb[...])
        h = (a * _sigmoid(a) * _dot(x, wub[...])).astype(BF16)
        y_ref[...] = _dot(h, wdb[...])

    @pl.when(i >= nv)
    def _():
        y_ref[...] = jnp.zeros_like(y_ref)


def _moe_experts(x1, row_tok, blk_e, blk_next, n_valid, w_gate, w_up, w_down):
    any_spec = pl.BlockSpec(memory_space=pl.ANY)
    return pl.pallas_call(
        _moe_kernel,
        grid_spec=pltpu.PrefetchScalarGridSpec(
            num_scalar_prefetch=4,
            grid=(MOE_NBLK,),
            in_specs=[any_spec, any_spec, any_spec, any_spec],
            out_specs=pl.BlockSpec((MOE_BLK, D_MODEL), lambda i, be, nx, nv, tok: (i, 0)),
            scratch_shapes=[
                pltpu.VMEM((D_MODEL, D_EXPERT), F32), pltpu.VMEM((D_MODEL, D_EXPERT), F32),
                pltpu.VMEM((D_EXPERT, D_MODEL), F32),
                pltpu.VMEM((D_MODEL, D_EXPERT), BF16), pltpu.VMEM((D_MODEL, D_EXPERT), BF16),
                pltpu.VMEM((D_EXPERT, D_MODEL), BF16),
                pltpu.VMEM((2, MOE_BLK, D_MODEL), F32),
                pltpu.SemaphoreType.DMA((3,)), pltpu.SemaphoreType.DMA((2,)),
            ],
        ),
        out_shape=jax.ShapeDtypeStruct((MOE_ROWS, D_MODEL), F32),
        compiler_params=_cp(("arbitrary",)),
        name="moe_experts",
    )(blk_e, blk_next, n_valid, row_tok, x1, w_gate, w_up, w_down)


def _moe_plan(e_idx):
    flat_e = e_idx.reshape(-1)
    n_asg = flat_e.shape[0]
    order = jnp.argsort(flat_e)
    se = flat_e[order]
    counts = jnp.zeros((N_EXPERTS,), jnp.int32).at[flat_e].add(1)
    starts = jnp.cumsum(counts) - counts
    pcounts = (counts + MOE_BLK - 1) // MOE_BLK * MOE_BLK
    pends = jnp.cumsum(pcounts)
    pstarts = pends - pcounts
    dest_sorted = pstarts[se] + jnp.arange(n_asg, dtype=jnp.int32) - starts[se]
    dest = jnp.zeros((n_asg,), jnp.int32).at[order].set(dest_sorted)
    row_tok = jnp.zeros((MOE_ROWS,), jnp.int32).at[dest_sorted].set((order // 2).astype(jnp.int32))
    n_valid = (pends[-1] // MOE_BLK).astype(jnp.int32)
    blk_start = jnp.arange(MOE_NBLK, dtype=jnp.int32) * MOE_BLK
    blk_e = jnp.searchsorted(pends, jnp.minimum(blk_start, pends[-1] - 1), side='right')
    blk_e = jnp.clip(blk_e, 0, N_EXPERTS - 1).astype(jnp.int32)
    ids = jnp.arange(N_EXPERTS, dtype=jnp.int32)
    later = jnp.where((counts[None, :] > 0) & (ids[None, :] > ids[:, None]), ids[None, :], N_EXPERTS)
    next_e = jnp.min(later, axis=1)
    next_e = jnp.where(next_e == N_EXPERTS, -1, next_e).astype(jnp.int32)
    return dest.reshape(-1, 2), row_tok, blk_e, next_e[blk_e], n_valid.reshape(1)


def _final_row_copy(dest_ref, y_hbm, ybuf, sem, blk, slot, k, t):
    src = dest_ref[k * M_ROWS + blk * LN_ROWS + t]
    return pltpu.make_async_copy(y_hbm.at[pl.ds(src, 1)], ybuf.at[slot, pl.ds(k * LN_ROWS + t, 1)], sem.at[slot])


def _final_kernel(dest_ref, base_ref, wt_ref, g_ref, b_ref, y_hbm, op_ref, os_ref, ybuf, sem):
    i = pl.program_id(0)
    n_rows = 2 * LN_ROWS

    def for_rows(blk, slot, action):
        for k in range(2):
            def body(t, carry, k=k):
                action(_final_row_copy(dest_ref, y_hbm, ybuf, sem, blk, slot, k, t))
                return carry
            lax.fori_loop(0, LN_ROWS, body, 0, unroll=8)

    @pl.when(i == 0)
    def _():
        for_rows(0, 0, lambda cp: cp.start())

    @pl.when(i + 1 < pl.num_programs(0))
    def _():
        for_rows(i + 1, (i + 1) & 1, lambda cp: cp.start())

    slot = i & 1
    for_rows(i, slot, lambda cp: cp.wait())

    wt = wt_ref[...]
    r = base_ref[...] + wt[:, 0:1] * ybuf[slot, 0:LN_ROWS, :] + wt[:, 1:2] * ybuf[slot, LN_ROWS:n_rows, :]
    y = _layer_norm_rows(r, g_ref[...], b_ref[...])

    @pl.when(i < SEQ // LN_ROWS)
    def _():
        op_ref[...] = y

    @pl.when(i >= SEQ // LN_ROWS)
    def _():
        os_ref[...] = y


def _final(dest_flat, base, y_rows, wt, g, b):
    npb = SEQ // LN_ROWS
    return pl.pallas_call(
        _final_kernel,
        grid_spec=pltpu.PrefetchScalarGridSpec(
            num_scalar_prefetch=1,
            grid=(npb + 1,),
            in_specs=[
                pl.BlockSpec((LN_ROWS, D_MODEL), lambda i, d: (i, 0)),
                pl.BlockSpec((LN_ROWS, LANES), lambda i, d: (i, 0)),
                pl.BlockSpec((1, D_MODEL), lambda i, d: (0, 0)),
                pl.BlockSpec((1, D_MODEL), lambda i, d: (0, 0)),
                pl.BlockSpec(memory_space=pl.ANY),
            ],
            out_specs=[pl.BlockSpec((LN_ROWS, D_MODEL), lambda i, d: (jnp.minimum(i, npb - 1), 0)),
                       pl.BlockSpec((N_SAMPLE, D_MODEL), lambda i, d: (0, 0))],
            scratch_shapes=[pltpu.VMEM((2, 2 * LN_ROWS, D_MODEL), F32), pltpu.SemaphoreType.DMA((2,))],
        ),
        out_shape=[jax.ShapeDtypeStruct((SEQ, D_MODEL), F32),
                   jax.ShapeDtypeStruct((N_SAMPLE, D_MODEL), F32)],
        compiler_params=_cp(("arbitrary",)),
        name="combine_ln2",
    )(dest_flat, base, wt, g.reshape(1, D_MODEL), b.reshape(1, D_MODEL), y_rows)


def kernel(x_prompt, x_sample, p_prompt, p_sample, cache_kv_a1, cache_kv_a2, cache_kv_a3, state_gla,
           ln_emb_g, ln_emb_b, ln1_g, ln1_b, ln2_g, ln2_b, w_in, w_gk2, b_gk, gla_norm_g, w_br_a, w_br_b,
           w_out, w_router_group, w_router_expert, w_gate, w_up, w_down, w_ple_gate, w_ple):
    xp = x_prompt.reshape(SEQ, D_MODEL)
    xs = x_sample.reshape(N_SAMPLE, D_MODEL)
    xn, xb = _ln_emb(xp, xs, ln_emb_g, ln_emb_b)

    w_t = jnp.transpose(w_in[0])
    rope = _rope_tables()
    u = _project(xb, w_t, 0, A_QKV_COLS, 2 * 3 * A_WIDTH // COL_TILE, rope, F32)
    u_gla = _project(xb, w_t, A_QKV_COLS, MAIN_COLS - A_QKV_COLS, 0, rope, BF16)
    gates = _project(xb, w_t[MAIN_COLS + GLA_LR:], 0, 2 * D_MODEL, 0, rope, BF16)
    gk = _gla_gate(xb, w_t[MAIN_COLS:MAIN_COLS + GLA_LR], w_gk2[0], b_gk[0])

    qkv_s = u[SEQ:].reshape(DEC_BATCH, DEC_SEQ, 9, A_HEADS, A_HEAD_DIM)
    kv_new = [jnp.stack([qkv_s[:, :, 3 + g], qkv_s[:, :, 6 + g]], axis=2) for g in range(3)]
    o_a, kv_s3 = _attn_prompt(u, kv_new[2], cache_kv_a3)
    c1 = cache_kv_a1.reshape(DEC_BATCH, A_REACH, 2, A_HEADS, A_HEAD_DIM)
    c2 = cache_kv_a2.reshape(DEC_BATCH, A_REACH, 4 * 2, A_HEADS, A_HEAD_DIM)
    c3 = cache_kv_a3.reshape(DEC_BATCH, A_REACH, 16 * 2, A_HEADS, A_HEAD_DIM)
    o_as = _attn_sample(qkv_s, c1, c2, c3)
    o_a = jnp.concatenate([o_a, o_as.reshape(N_SAMPLE, A_WIDTH).astype(BF16)], axis=0)

    o_g, st_t, kv_s1, kv_s2 = _gla_prompt(u_gla, gk, gla_norm_g[0], kv_new[:2], (cache_kv_a1, cache_kv_a2))

    def spad(a):
        a = a.astype(F32).reshape(DEC_BATCH, DEC_SEQ, a.shape[-1])
        return jnp.pad(a, ((0, 0), (0, GLA_SPAD - DEC_SEQ), (0, 0)))

    ug_s = u_gla[SEQ:]
    o_gs, st_s = _gla_sample(
        spad(ug_s[:, :GLA_QK]), spad(ug_s[:, GLA_QK:2 * GLA_QK]),
        spad(ug_s[:, 2 * GLA_QK:2 * GLA_QK + GLA_V]), spad(ug_s[:, 2 * GLA_QK + GLA_V:]),
        spad(gk[SEQ:]), gla_norm_g[0], state_gla[0])
    o_gs = o_gs[:, :DEC_SEQ].reshape(N_SAMPLE, GLA_V).astype(BF16)
    o_g = jnp.concatenate([o_g, o_gs], axis=0)

    x1 = _merge_ln1(o_a, o_g, gates, xn, ln1_g[0], ln1_b[0],
                    w_br_a[0].astype(BF16), w_br_b[0].astype(BF16), w_out[0].astype(BF16))

    p = jnp.concatenate([p_prompt[0].reshape(SEQ, PLE_DIM), p_sample[0].reshape(N_SAMPLE, PLE_DIM)], axis=0)
    base = _ple(x1, p.astype(BF16), w_ple_gate[0].astype(BF16), w_ple[0].astype(BF16))
    ridx, rwt = _router(x1, w_router_group[0], w_router_expert[0])
    dest, row_tok, blk_e, blk_next, n_valid = _moe_plan(ridx[:, :2])
    y_rows =_moe_experts(x1, row_tok, blk_e, blk_next, n_valid, w_gate[0], w_up[0], w_down[0])
    y_p, y_s = _final(dest.T.reshape(-1), base, y_rows, rwt, ln2_g[0], ln2_b[0])

    def kv_prompt(g):
        w = A_WINDOWS[g]
        k = u[SEQ - w:SEQ, (3 + g) * A_WIDTH:(4 + g) * A_WIDTH].reshape(w, A_HEADS, A_HEAD_DIM)
        v = u[SEQ - w:SEQ, (6 + g) * A_WIDTH:(7 + g) * A_WIDTH].reshape(w, A_HEADS, A_HEAD_DIM)
        return jnp.stack([k, v], axis=1)[None, None]


    gla_state_prompt = st_t[None, None]
    return (y_p.reshape(1, SEQ, D_MODEL), y_s.reshape(DEC_BATCH, DEC_SEQ, D_MODEL),
            kv_prompt(0), kv_prompt(1), kv_prompt(2), gla_state_prompt,
            kv_s1, kv_s2, kv_s3, st_s[None])
```

```python
import functools

import jax
import jax.numpy as jnp
import numpy as np
from jax import lax
from jax.experimental import pallas as pl
from jax.experimental.pallas import tpu as pltpu

F32 = jnp.float32
BF16 = jnp.bfloat16

D_MODEL = 2048
SEQ = 8192
DEC_BATCH = 32
DEC_SEQ = 4
PAST_LEN = 16384
N_SAMPLE = DEC_BATCH * DEC_SEQ
M_ROWS = SEQ + N_SAMPLE
A_WINDOWS = (128, 512, 2048)
A_DILATIONS = (1, 4, 16)
A_REACH = 128
A_HEADS = 8
A_HEAD_DIM = 128
A_WIDTH = A_HEADS * A_HEAD_DIM
ROT_DIM = 32
ROPE_THETA = 500000.0
A_QKV_COLS = 9 * A_WIDTH
GLA_HEADS = 4
GLA_DK = 256
GLA_DV = 512
GLA_LR = 16
GLA_TAU = 16.0
GLA_CHUNK = 64
GLA_QK = GLA_HEADS * GLA_DK
GLA_V = GLA_HEADS * GLA_DV
MAIN_COLS = A_QKV_COLS + 2 * GLA_QK + 2 * GLA_V
MOE_GROUPS = 4
MOE_PER_GROUP = 8
N_EXPERTS = 32
D_EXPERT = 1024
PLE_DIM = 256
LN_EPS = 1e-5
ALPHA = 2.0 ** 0.25

LANES = 128
VMEM_LIMIT = 56 * 1024 * 1024
ROW_TILE_PROJ = 1664
ROW_TILE = 832
MERGE_ROWS = 416
COL_TILE = 512
LN_ROWS = 128
MOE_BLK = 256
MOE_NBLK = (2 * M_ROWS) // MOE_BLK + N_EXPERTS
MOE_ROWS = MOE_NBLK * MOE_BLK
ATT_SB = 2048
GLA_ROWS = 512
ATT_VMEM_LIMIT = 60 * 1024 * 1024
SHIFT_SLOTS = 4
SHIFT_CHUNK = 511
ATT_UNROLL = 4
ATT_RESIDUE_MAJOR_D = 16
ATT_RESIDUE_PITCH = 264

def _cp(sem, vmem=VMEM_LIMIT):
    return pltpu.CompilerParams(dimension_semantics=sem, vmem_limit_bytes=vmem)


def _sigmoid(x):
    return 1.0 / (1.0 + jnp.exp(-x))


def _dot(a, b):
    return jnp.dot(a, b, preferred_element_type=F32)


def _dot_nt(a, b):
    return lax.dot_general(a, b, (((1,), (1,)), ((), ())), preferred_element_type=F32)


def _split3(x):
    h = x.astype(BF16)
    r = x - h.astype(F32)
    m = r.astype(BF16)
    l = (r - m.astype(F32)).astype(BF16)
    return h, m, l


def _layer_norm_rows(x, g, b):
    mu = jnp.mean(x, axis=-1, keepdims=True)
    xc = x - mu
    var = jnp.mean(xc * xc, axis=-1, keepdims=True)
    return xc * lax.rsqrt(var + LN_EPS) * g + b


def _ln_emb_kernel(xp_ref, xs_ref, g_ref, b_ref, of_ref, ob_ref):
    i = pl.program_id(0)
    x = jnp.where(i < SEQ // LN_ROWS, xp_ref[...], xs_ref[...])
    y = _layer_norm_rows(x, g_ref[...], b_ref[...])
    of_ref[...] = y
    ob_ref[...] = y.astype(BF16)


def _ln_emb(xp, xs, g, b):
    npb = SEQ // LN_ROWS
    return pl.pallas_call(
        _ln_emb_kernel,
        grid=(npb + 1,),
        in_specs=[
            pl.BlockSpec((LN_ROWS, D_MODEL), lambda i: (jnp.minimum(i, npb - 1), 0)),
            pl.BlockSpec((N_SAMPLE, D_MODEL), lambda i: (0, 0)),
            pl.BlockSpec((1, D_MODEL), lambda i: (0, 0)),
            pl.BlockSpec((1, D_MODEL), lambda i: (0, 0)),
        ],
        out_specs=[
            pl.BlockSpec((LN_ROWS, D_MODEL), lambda i: (i, 0)),
            pl.BlockSpec((LN_ROWS, D_MODEL), lambda i: (i, 0)),
        ],
        out_shape=[jax.ShapeDtypeStruct((M_ROWS, D_MODEL), F32),
                   jax.ShapeDtypeStruct((M_ROWS, D_MODEL), BF16)],
        compiler_params=_cp(("arbitrary",)),
        name="ln_emb",
    )(xp, xs, g.reshape(1, D_MODEL), b.reshape(1, D_MODEL))


def _proj_kernel(x_ref, w_ref, c_ref, s1_ref, s2_ref, o_ref, *, n_rope_blocks):
    j = pl.program_id(1)
    acc = _dot_nt(x_ref[...], w_ref[...].astype(BF16))

    @pl.when(j >= n_rope_blocks)
    def _():
        o_ref[...] = acc.astype(o_ref.dtype)

    @pl.when(j < n_rope_blocks)
    def _():
        c, s1, s2 = c_ref[...], s1_ref[...], s2_ref[...]
        for t in range(COL_TILE // LANES):
            a = acc[:, t * LANES:(t + 1) * LANES]
            o_ref[:, t * LANES:(t + 1) * LANES] = (
                a * c + pltpu.roll(a, LANES - ROT_DIM // 2, 1) * s1 + pltpu.roll(a, ROT_DIM // 2, 1) * s2
            ).astype(o_ref.dtype)


def _project(xb, w_t, row0, n_cols, n_rope_blocks, rope, out_dtype):
    c, s1, s2 = rope
    tm, tn = ROW_TILE_PROJ, COL_TILE
    jb0 = row0 // tn
    return pl.pallas_call(
        functools.partial(_proj_kernel, n_rope_blocks=n_rope_blocks),
        grid=(M_ROWS // tm, n_cols // tn),
        in_specs=[
            pl.BlockSpec((tm, D_MODEL), lambda i, j: (i, 0)),
            pl.BlockSpec((tn, D_MODEL), lambda i, j: (jb0 + j, 0)),
            pl.BlockSpec((tm, LANES), lambda i, j: (i, 0)),
            pl.BlockSpec((tm, LANES), lambda i, j: (i, 0)),
            pl.BlockSpec((tm, LANES), lambda i, j: (i, 0)),
        ],
        out_specs=pl.BlockSpec((tm, tn), lambda i, j: (i, j)),
        out_shape=jax.ShapeDtypeStruct((M_ROWS, n_cols), out_dtype),
        compiler_params=_cp(("arbitrary", "arbitrary")),
        name="in_proj",
    )(xb, w_t, c, s1, s2)


def _rope_tables():
    half = ROT_DIM // 2
    inv = np.power(np.float32(ROPE_THETA), -np.arange(half, dtype=np.float32) * np.float32(2.0 / ROT_DIM))
    pos = np.concatenate([np.arange(SEQ), np.tile(PAST_LEN + np.arange(DEC_SEQ), DEC_BATCH)]).astype(np.float32)
    ang = (pos[:, None] * inv[None, :].astype(np.float32)).astype(np.float32)
    cos, sin = np.cos(ang).astype(np.float32), np.sin(ang).astype(np.float32)
    ones = np.ones((M_ROWS, LANES - ROT_DIM), np.float32)
    zeros = np.zeros((M_ROWS, LANES - ROT_DIM), np.float32)
    zh = np.zeros((M_ROWS, half), np.float32)
    c = np.concatenate([cos, cos, ones], axis=1)
    s1 = np.concatenate([-sin, zh, zeros], axis=1)
    s2 = np.concatenate([zh, sin, zeros], axis=1)
    return jnp.asarray(c), jnp.asarray(s1), jnp.asarray(s2)


def _gk_kernel(x_ref, wlr_ref, wgk_ref, b_ref, o_ref):
    blr = _dot_nt(x_ref[...], wlr_ref[...])
    z = _dot(blr.astype(BF16), wgk_ref[...]) + b_ref[...]
    o_ref[...] = (jnp.minimum(z, 0.0) - jnp.log1p(jnp.exp(-jnp.abs(z)))) * (1.0 / GLA_TAU)


def _gla_gate(xb, w_lr_t, w_gk2, b_gk):
    wlr = jnp.zeros((LANES, D_MODEL), BF16).at[:GLA_LR].set(w_lr_t.astype(BF16))
    wgk = jnp.zeros((LANES, GLA_QK), BF16).at[:GLA_LR].set(w_gk2.astype(BF16))
    tm = ROW_TILE
    return pl.pallas_call(
        _gk_kernel,
        grid=(M_ROWS // tm,),
        in_specs=[
            pl.BlockSpec((tm, D_MODEL), lambda i: (i, 0)),
            pl.BlockSpec((LANES, D_MODEL), lambda i: (0, 0)),
            pl.BlockSpec((LANES, GLA_QK), lambda i: (0, 0)),
            pl.BlockSpec((1, GLA_QK), lambda i: (0, 0)),
        ],
        out_specs=pl.BlockSpec((tm, GLA_QK), lambda i: (i, 0)),
        out_shape=jax.ShapeDtypeStruct((M_ROWS, GLA_QK), F32),
        compiler_params=_cp(("arbitrary",)),
        name="gla_gate",
    )(xb, wlr, wgk, b_gk.reshape(1, GLA_QK))


class _CacheShiftInBackground:
    def __init__(self, step, n_steps, pieces, news, buf, rsem, wsem, nsem):
        self.step, self.n_steps, self.pieces, self.news = step, n_steps, pieces, news
        self.buf, self.rsem, self.wsem, self.nsem = buf, rsem, wsem, nsem

    def _copies(self, step):
        out = []
        for slot, (cache, dst, b, pos0, n) in enumerate(self.pieces(step)):
            stage = self.buf.at[slot, pl.ds(0, n)]
            rd = pltpu.make_async_copy(cache.at[0, b, pl.ds(DEC_SEQ + pos0, n)], stage, self.rsem.at[slot])
            wr = pltpu.make_async_copy(stage, dst.at[0, b, pl.ds(pos0, n)], self.wsem.at[slot])
            out.append((rd, wr))
        return out

    def _new_rows(self):
        return [pltpu.make_async_copy(new, dst.at[0, :, pl.ds(w - DEC_SEQ, DEC_SEQ)], self.nsem.at[i])
                for i, (new, dst, w) in enumerate(self.news)]

    def top_of_step(self):
        @pl.when(self.step == 0)
        def _():
            for cp in self._new_rows():
                cp.start()

        @pl.when(self.step > 0)
        def _():
            for _, wr in self._copies(self.step - 1):
                wr.wait()

        for rd, _ in self._copies(self.step):
            rd.start()

    def turn_reads_into_writes(self):
        for rd, wr in self._copies(self.step):
            rd.wait()
            wr.start()

    def end_of_step(self):
        @pl.when(self.step == self.n_steps - 1)
        def _():
            for _, wr in self._copies(self.step):
                wr.wait()
            for cp in self._new_rows():
                cp.wait()


def _gla_chunks(heads, gn, causal, tri_b, eye):
    n = heads[0][3].shape[0]
    rid = lax.broadcasted_iota(jnp.int32, (n + GLA_TPAD, GLA_DK), 0)
    bs = []
    for _, _, _, g, _, _ in heads:
        g1, g2, g3 = _split3(g)
        bs.append(_dot(tri_b, g1) + _dot(tri_b, g2) + _dot(tri_b, g3))
    ops = []
    for (q, k, v, _, _, _), b in zip(heads, bs):
        q, k = q.astype(F32), k.astype(F32)
        b_last = b[n - 1:n, :]
        q_dec = (q * ((GLA_DK ** -0.5) * jnp.exp(b))).astype(BF16)
        k_inv = (k * jnp.exp(-b)).astype(BF16)
        k_dec = k * jnp.exp(b_last - b)
        d1, d2, d3 = _split3(jnp.exp(b_last))
        rows = jnp.concatenate([k_dec, jnp.zeros((GLA_TPAD, GLA_DK), F32)], axis=0)
        rows = jnp.where(rid == n, d1.astype(F32), rows)
        rows = jnp.where(rid == n + 1, d2.astype(F32), rows)
        rows = jnp.where(rid == n + 2, d3.astype(F32), rows)
        ops.append((q_dec, k_inv, rows.astype(BF16), v.astype(BF16)))
    mids = []
    for q_dec, k_inv, rows, _ in ops:
        mids.append((jnp.where(causal, _dot_nt(q_dec, k_inv), 0.0).astype(BF16), _dot_nt(eye, rows)))
    outs = []
    for (q_dec, _, _, vb), (att, cols), (_, _, _, _, br, s0) in zip(ops, mids, heads):
        o = _dot(att, vb) + _dot(q_dec, s0.astype(BF16))
        dcol = cols[:, n:n + 1] + cols[:, n + 1:n + 2] + cols[:, n + 2:n + 3]
        s1 = s0 * dcol + _dot(cols[:, :n].astype(BF16), vb)
        br = br.astype(F32)
        ms = jnp.mean(o * o, axis=-1, keepdims=True)
        outs.append((o * lax.rsqrt(ms + LN_EPS) * gn * (br * _sigmoid(br)), s1))
    return outs


GLA_TPAD = 16


def _gla_consts(n):
    r = lax.broadcasted_iota(jnp.int32, (n, n), 0)
    c = lax.broadcasted_iota(jnp.int32, (n, n), 1)
    causal = r >= c
    er = lax.broadcasted_iota(jnp.int32, (GLA_DK, GLA_DK), 0)
    ec = lax.broadcasted_iota(jnp.int32, (GLA_DK, GLA_DK), 1)
    return causal, jnp.where(causal, 1.0, 0.0).astype(BF16), jnp.where(er == ec, 1.0, 0.0).astype(BF16)


def _gla_prompt_kernel(q_ref, k_ref, v_ref, br_ref, g_ref, gn_ref, n1_ref, n2_ref, c1_hbm, c2_hbm,
                       o_ref, st_ref, kv1_hbm, kv2_hbm, cbuf, rsem, wsem, nsem):
    step = pl.program_id(0)
    n_steps = SEQ // GLA_ROWS
    per_step = DEC_BATCH // n_steps

    def pieces(s):
        out = []
        for cache, dst, w in ((c1_hbm, kv1_hbm, A_WINDOWS[0]), (c2_hbm, kv2_hbm, A_WINDOWS[1])):
            out += [(cache, dst, s * per_step + j, 0, w - DEC_SEQ) for j in range(per_step)]
        return out

    shifter = _CacheShiftInBackground(
        step, n_steps, pieces, [(n1_ref, kv1_hbm, A_WINDOWS[0]), (n2_ref, kv2_hbm, A_WINDOWS[1])],
        cbuf, rsem, wsem, nsem)
    shifter.top_of_step()

    @pl.when(step == 0)
    def _():
        st_ref[...] = jnp.zeros_like(st_ref)

    causal, tri_b, eye = _gla_consts(GLA_CHUNK)
    gn = gn_ref[...]

    def body(cc, carry):
        rows = pl.ds(pl.multiple_of(cc * GLA_CHUNK, GLA_CHUNK), GLA_CHUNK)
        heads = []
        for h in range(GLA_HEADS):
            ks = slice(h * GLA_DK, (h + 1) * GLA_DK)
            vs = slice(h * GLA_DV, (h + 1) * GLA_DV)
            heads.append((q_ref[rows, ks], k_ref[rows, ks], v_ref[rows, vs], g_ref[rows, ks],
                          br_ref[rows, vs], st_ref[h]))
        for h, (o, s1) in enumerate(_gla_chunks(heads, gn, causal, tri_b, eye)):
            st_ref[h] = s1
            o_ref[rows, h * GLA_DV:(h + 1) * GLA_DV] = o.astype(BF16)
        return carry

    half = GLA_ROWS // GLA_CHUNK // 2
    lax.fori_loop(0, half, body, 0)
    shifter.turn_reads_into_writes()
    lax.fori_loop(half, 2 * half, body, 0)
    shifter.end_of_step()


def _gla_prompt(u, gk, gnorm, news, caches):
    n_steps = SEQ // GLA_ROWS
    n_slots = 2 * (DEC_BATCH // n_steps)
    any_spec = pl.BlockSpec(memory_space=pl.ANY)
    return pl.pallas_call(
        _gla_prompt_kernel,
        grid=(n_steps,),
        in_specs=[
            pl.BlockSpec((GLA_ROWS, GLA_QK), lambda c: (c, 0)),
            pl.BlockSpec((GLA_ROWS, GLA_QK), lambda c: (c, 1)),
            pl.BlockSpec((GLA_ROWS, GLA_V), lambda c: (c, 1)),
            pl.BlockSpec((GLA_ROWS, GLA_V), lambda c: (c, 2)),
            pl.BlockSpec((GLA_ROWS, GLA_QK), lambda c: (c, 0)),
            pl.BlockSpec((1, GLA_DV), lambda c: (0, 0)),
            pl.BlockSpec(news[0].shape, lambda c: (0, 0, 0, 0, 0)),
            pl.BlockSpec(news[1].shape, lambda c: (0, 0, 0, 0, 0)),
            any_spec, any_spec,
        ],
        out_specs=[
            pl.BlockSpec((GLA_ROWS, GLA_V), lambda c: (c, 0)),
            pl.BlockSpec((GLA_HEADS, GLA_DK, GLA_DV), lambda c: (0, 0, 0)),
            any_spec, any_spec,
        ],
        out_shape=[jax.ShapeDtypeStruct((SEQ, GLA_V), BF16),
                   jax.ShapeDtypeStruct((GLA_HEADS, GLA_DK, GLA_DV), F32),
                   jax.ShapeDtypeStruct(caches[0].shape, caches[0].dtype),
                   jax.ShapeDtypeStruct(caches[1].shape, caches[1].dtype)],
        scratch_shapes=[pltpu.VMEM((n_slots, A_WINDOWS[1] - DEC_SEQ, 2, A_HEADS, A_HEAD_DIM), F32),
                        pltpu.SemaphoreType.DMA((n_slots,)), pltpu.SemaphoreType.DMA((n_slots,)),
                        pltpu.SemaphoreType.DMA((2,))],
        compiler_params=_cp(("arbitrary",)),
        name="gla_prompt",
    )(u, u, u, u, gk, gnorm.reshape(1, GLA_DV), news[0], news[1], caches[0], caches[1])


GLA_SPAD = 16


def _gla_sample_kernel(q_ref, k_ref, v_ref, br_ref, g_ref, gn_ref, s0_ref, o_ref, s1_ref):
    causal, tri_b, eye = _gla_consts(GLA_SPAD)
    gn = gn_ref[...]
    heads = []
    for h in range(GLA_HEADS):
        ks = slice(h * GLA_DK, (h + 1) * GLA_DK)
        vs = slice(h * GLA_DV, (h + 1) * GLA_DV)
        heads.append((q_ref[0, :, ks], k_ref[0, :, ks], v_ref[0, :, vs], g_ref[0, :, ks],
                      br_ref[0, :, vs], s0_ref[0, h]))
    for h, (o, s1) in enumerate(_gla_chunks(heads, gn, causal, tri_b, eye)):
        s1_ref[0, h] = s1
        o_ref[0, :, h * GLA_DV:(h + 1) * GLA_DV] = o


def _gla_sample(q, k, v, br, g, gnorm, s0):
    def spec(c):
        return pl.BlockSpec((1, GLA_SPAD, c), lambda b: (b, 0, 0))
    st_spec = pl.BlockSpec((1, GLA_HEADS, GLA_DK, GLA_DV), lambda b: (b, 0, 0, 0))
    return pl.pallas_call(
        _gla_sample_kernel,
        grid=(DEC_BATCH,),
        in_specs=[spec(GLA_QK), spec(GLA_QK), spec(GLA_V), spec(GLA_V), spec(GLA_QK),
                  pl.BlockSpec((1, GLA_DV), lambda b: (0, 0)), st_spec],
        out_specs=[spec(GLA_V), st_spec],
        out_shape=[jax.ShapeDtypeStruct((DEC_BATCH, GLA_SPAD, GLA_V), F32),
                   jax.ShapeDtypeStruct((DEC_BATCH, GLA_HEADS, GLA_DK, GLA_DV), F32)],
        compiler_params=_cp(("arbitrary",)),
        name="gla_sample",
    )(q, k, v, br, g, gnorm.reshape(1, GLA_DV), s0)


def _attn_prompt_kernel(q0, q1, q2, kc0, kp0, vc0, vp0, kc1, kp1, vc1, vp1, kc2, kp2, vc2, vp2,
                        new3_ref, cache3_hbm, o_ref, kv3_hbm,
                        kb0, vb0, kb1, vb1, kb2, vb2, acc_ref, m_ref, l_ref, cbuf, rsem, wsem, nsem):
    sb = pl.program_id(0)
    step = sb * pl.num_programs(1) + pl.program_id(1)

    def pieces(s):
        return [(cache3_hbm, kv3_hbm, s, k * SHIFT_CHUNK, SHIFT_CHUNK) for k in range(SHIFT_SLOTS)]

    shifter = _CacheShiftInBackground(step, DEC_BATCH, pieces, [(new3_ref, kv3_hbm, A_WINDOWS[2])],
                                      cbuf, rsem, wsem, nsem)
    shifter.top_of_step()
    groups = ((q0, kc0, kp0, vc0, vp0, kb0, vb0, 1),
              (q1, kc1, kp1, vc1, vp1, kb1, vb1, 4),
              (q2, kc2, kp2, vc2, vp2, kb2, vb2, 16))
    scale = A_HEAD_DIM ** -0.5
    n_win = 2 * A_REACH
    row = lax.broadcasted_iota(jnp.int32, (A_REACH, n_win), 0)
    col = lax.broadcasted_iota(jnp.int32, (A_REACH, n_win), 1)
    in_reach = jnp.logical_and(col >= row, col <= row + A_REACH)
    in_prev = col < A_REACH
    ones = jnp.ones((n_win, LANES), BF16)

    for gi, (q_ref, kc, kp, vc, vp, kb, vb, d) in enumerate(groups):
        win = A_REACH * d
        by_residue = d == ATT_RESIDUE_MAJOR_D
        if by_residue:
            def stage(jj, carry, kp=kp, kc=kc, vp=vp, vc=vc, kb=kb, vb=vb, d=d):
                src = pl.ds(pl.multiple_of(jj * d, d), d)
                for buf, prev, cur in ((kb, kp, kc), (vb, vp, vc)):
                    buf[pl.ds(jj, d, stride=ATT_RESIDUE_PITCH), :] = prev[src, :]
                    buf[pl.ds(A_REACH + jj, d, stride=ATT_RESIDUE_PITCH), :] = cur[src, :]
                return carry
            lax.fori_loop(0, A_REACH, stage, 0, unroll=8)
        else:
            kb[0:win, :] = kp[...]
            kb[win:win + ATT_SB, :] = kc[...]
            vb[0:win, :] = vp[...]
            vb[win:win + ATT_SB, :] = vc[...]
        shift = {1: 0, 4: 2, 16: 4}[d]

        def tiles(it, carry, q_ref=q_ref, kb=kb, vb=vb, d=d, win=win, shift=shift, gi=gi, by_residue=by_residue):
            rows, scores = [], []
            for j in range(ATT_UNROLL):
                idx = it * ATT_UNROLL + j
                r = idx & (d - 1)
                nb = idx >> shift
                start = r + nb * win
                if d == 1:
                    start = pl.multiple_of(start, A_REACH)
                    qrows = pl.ds(start, A_REACH)
                    wrows = pl.ds(start, n_win)
                elif by_residue:
                    qrows = pl.ds(start, A_REACH, stride=d)
                    wrows = pl.ds(pl.multiple_of(r * ATT_RESIDUE_PITCH, 8), n_win)
                else:
                    qrows = pl.ds(start, A_REACH, stride=d)
                    wrows = pl.ds(start, n_win, stride=d)
                prev_bias = jnp.where(jnp.logical_or(sb > 0, nb > 0), 0.0, -jnp.inf)
                rows.append((qrows, wrows, prev_bias))
                scores.append(_dot_nt(q_ref[qrows, :].astype(BF16), kb[wrows, :].astype(BF16)))
            probs = []
            for s, (_, _, prev_bias) in zip(scores, rows):
                s = jnp.where(in_reach, s * scale, -jnp.inf) + jnp.where(in_prev, prev_bias, 0.0)
                m_t = jnp.max(s, axis=1, keepdims=True)
                probs.append((jnp.exp(s - m_t).astype(BF16), jnp.broadcast_to(m_t, (A_REACH, LANES))))
            sums = []
            for (p, _), (_, wrows, _) in zip(probs, rows):
                sums.append((_dot(p, ones), _dot(p, vb[wrows, :].astype(BF16))))
            for (l_t, num), (_, m_t), (qrows, _, _) in zip(sums, probs, rows):
                if gi == 0:
                    acc_ref[qrows, :] = num
                    m_ref[qrows, :] = m_t
                    l_ref[qrows, :] = l_t
                else:
                    m_o = m_ref[qrows, :]
                    m_n = jnp.maximum(m_o, m_t)
                    a = jnp.exp(m_o - m_n)
                    b = jnp.exp(m_t - m_n)
                    acc_ref[qrows, :] = a * acc_ref[qrows, :] + b * num
                    l_ref[qrows, :] = a * l_ref[qrows, :] + b * l_t
                    m_ref[qrows, :] = m_n
            return carry

        lax.fori_loop(0, ATT_SB // A_REACH // ATT_UNROLL, tiles, 0)
        if gi == 1:
            shifter.turn_reads_into_writes()

    o_ref[...] = (acc_ref[...] / l_ref[...]).astype(BF16)
    shifter.end_of_step()


def _attn_prompt(u, new3, cache3):
    nh = A_HEADS
    assert (SEQ // ATT_SB) * nh == DEC_BATCH and SHIFT_SLOTS * SHIFT_CHUNK == A_WINDOWS[2] - DEC_SEQ
    in_specs, scratch = [], []
    for g in range(3):
        in_specs.append(pl.BlockSpec((ATT_SB, LANES), lambda sb, h, g=g: (sb, g * nh + h)))
    for g, d in enumerate(A_DILATIONS):
        win = A_REACH * d
        per = ATT_SB // win
        for part in (3, 6):
            cb = part * nh + g * nh
            in_specs.append(pl.BlockSpec((ATT_SB, LANES), lambda sb, h, cb=cb: (sb, cb + h)))
            in_specs.append(pl.BlockSpec(
                (win, LANES), lambda sb, h, cb=cb, per=per: (jnp.maximum(sb * per - 1, 0), cb + h)))
        buf_rows = d * ATT_RESIDUE_PITCH if d == ATT_RESIDUE_MAJOR_D else win + ATT_SB
        scratch += [pltpu.VMEM((buf_rows, LANES), F32), pltpu.VMEM((buf_rows, LANES), F32)]
    scratch += [pltpu.VMEM((ATT_SB, LANES), F32)] * 3
    scratch += [pltpu.VMEM((SHIFT_SLOTS, SHIFT_CHUNK + 1, 2, A_HEADS, A_HEAD_DIM), F32),
                pltpu.SemaphoreType.DMA((SHIFT_SLOTS,)), pltpu.SemaphoreType.DMA((SHIFT_SLOTS,)),
                pltpu.SemaphoreType.DMA((1,))]
    in_specs += [pl.BlockSpec(new3.shape, lambda sb, h: (0, 0, 0, 0, 0)), pl.BlockSpec(memory_space=pl.ANY)]
    return pl.pallas_call(
        _attn_prompt_kernel,
        grid=(SEQ // ATT_SB, nh),
        in_specs=in_specs,
        out_specs=[pl.BlockSpec((ATT_SB, LANES), lambda sb, h: (sb, h)), pl.BlockSpec(memory_space=pl.ANY)],
        out_shape=[jax.ShapeDtypeStruct((SEQ, A_WIDTH), BF16), jax.ShapeDtypeStruct(cache3.shape, cache3.dtype)],
        scratch_shapes=scratch,
        compiler_params=_cp(("arbitrary", "arbitrary"), vmem=ATT_VMEM_LIMIT),
        name="attn_prompt",
    )(*([u] * 15), new3, cache3)


def _attn_sample_kernel(qkv_ref, c1_ref, c2_ref, c3_ref, o_ref):
    scale = A_HEAD_DIM ** -0.5
    pos = lax.broadcasted_iota(jnp.int32, (A_REACH, 1, 1), 0)

    def merge(state, m_t, l_t, num):
        if state is None:
            return m_t, l_t, num
        m_o, l_o, n_o = state
        m_n = jnp.maximum(m_o, m_t)
        a, b = jnp.exp(m_o - m_n), jnp.exp(m_t - m_n)
        return m_n, a * l_o + b * l_t, a * n_o + b * num

    for s in range(DEC_SEQ):
        state = None
        for g in range(3):
            q = qkv_ref[0, s, g]
            if g == 0:
                kt, vt = c1_ref[0, :, 0], c1_ref[0, :, 1]
                valid = pos >= s
                new_rows = range(s + 1)
            else:
                c_ref = c2_ref if g == 1 else c3_ref
                kt, vt = c_ref[0, :, 2 * s], c_ref[0, :, 2 * s + 1]
                valid = None
                new_rows = (s,)
            sc = jnp.sum(kt * q[None], axis=-1, keepdims=True) * scale
            if valid is not None:
                sc = jnp.where(valid, sc, -jnp.inf)
            s_new = [jnp.sum(qkv_ref[0, t, 3 + g] * q, axis=-1, keepdims=True) * scale for t in new_rows]
            m_t = jnp.max(sc, axis=0)
            for sn in s_new:
                m_t = jnp.maximum(m_t, sn)
            p = jnp.exp(sc - m_t[None])
            l_t = jnp.sum(p, axis=0)
            num = jnp.sum(p * vt, axis=0)
            for t, sn in zip(new_rows, s_new):
                pn = jnp.exp(sn - m_t)
                l_t = l_t + pn
                num = num + pn * qkv_ref[0, t, 6 + g]
            state = merge(state, m_t, l_t, num)
        o_ref[0, s] = state[2] / state[1]


def _attn_sample(qkv_s, c1, c2, c3):
    tile = (A_HEADS, A_HEAD_DIM)
    return pl.pallas_call(
        _attn_sample_kernel,
        grid=(DEC_BATCH,),
        in_specs=[
            pl.BlockSpec((1, DEC_SEQ, 9) + tile, lambda b: (b, 0, 0, 0, 0)),
            pl.BlockSpec((1, A_REACH, 2) + tile, lambda b: (b, 0, 0, 0, 0)),
            pl.BlockSpec((1, A_REACH, 2 * DEC_SEQ) + tile, lambda b: (b, 0, 0, 0, 0)),
            pl.BlockSpec((1, A_REACH, 2 * DEC_SEQ) + tile, lambda b: (b, 0, 0, 0, 0)),
        ],
        out_specs=pl.BlockSpec((1, DEC_SEQ) + tile, lambda b: (b, 0, 0, 0)),
        out_shape=jax.ShapeDtypeStruct((DEC_BATCH, DEC_SEQ) + tile, F32),
        compiler_params=_cp(("arbitrary",)),
        name="attn_sample",
    )(qkv_s, c1, c2, c3)


def _load_resident(step, pairs):
    @pl.when(step == 0)
    def _():
        for src, dst in pairs:
            pltpu.sync_copy(src, dst)


def _merge_ln1_kernel(oa_ref, og_ref, gt_ref, xn_ref, g_ref, b_ref, wa_hbm, wb_hbm, wo_hbm,
                      x1_ref, wa, wb, wo, mixed, resid):
    _load_resident(pl.program_id(0), ((wa_hbm, wa), (wb_hbm, wb), (wo_hbm, wo)))
    nj = D_MODEL // COL_TILE
    oa, og = oa_ref[...], og_ref[...]
    for t in range(nj):
        cs = slice(t * COL_TILE, (t + 1) * COL_TILE)
        gs = slice(D_MODEL + t * COL_TILE, D_MODEL + (t + 1) * COL_TILE)
        ya = _dot(oa, wa[:, cs])
        yb = _dot(og, wb[:, cs])
        ga, gb = gt_ref[:, cs].astype(F32), gt_ref[:, gs].astype(F32)
        mixed[:, cs] = (_sigmoid(ga) * ya + _sigmoid(gb) * yb).astype(BF16)
    mx = mixed[...]
    for t in range(nj):
        cs = slice(t * COL_TILE, (t + 1) * COL_TILE)
        resid[:, cs] = ALPHA * xn_ref[:, cs] + _dot(mx, wo[:, cs])
    x1_ref[...] = _layer_norm_rows(resid[...], g_ref[...], b_ref[...])


def _merge_ln1(o_a, o_g, gates, xn, g, b, wa, wb, wo):
    tm = MERGE_ROWS
    any_spec = pl.BlockSpec(memory_space=pl.ANY)
    return pl.pallas_call(
        _merge_ln1_kernel,
        grid=(M_ROWS // tm,),
        in_specs=[
            pl.BlockSpec((tm, A_WIDTH), lambda i: (i, 0)),
            pl.BlockSpec((tm, GLA_V), lambda i: (i, 0)),
            pl.BlockSpec((tm, 2 * D_MODEL), lambda i: (i, 0)),
            pl.BlockSpec((tm, D_MODEL), lambda i: (i, 0)),
            pl.BlockSpec((1, D_MODEL), lambda i: (0, 0)),
            pl.BlockSpec((1, D_MODEL), lambda i: (0, 0)),
            any_spec, any_spec, any_spec,
        ],
        out_specs=pl.BlockSpec((tm, D_MODEL), lambda i: (i, 0)),
        out_shape=jax.ShapeDtypeStruct((M_ROWS, D_MODEL), F32),
        scratch_shapes=[pltpu.VMEM((A_WIDTH, D_MODEL), BF16), pltpu.VMEM((GLA_V, D_MODEL), BF16),
                        pltpu.VMEM((D_MODEL, D_MODEL), BF16),
                        pltpu.VMEM((tm, D_MODEL), BF16), pltpu.VMEM((tm, D_MODEL), F32)],
        compiler_params=_cp(("arbitrary",)),
        name="merge_ln1",
    )(o_a, o_g, gates, xn, g.reshape(1, D_MODEL), b.reshape(1, D_MODEL), wa, wb, wo)


def _ple_kernel(x_ref, p_ref, wg_hbm, wp_hbm, o_ref, wg, wp):
    _load_resident(pl.program_id(0), ((wg_hbm, wg), (wp_hbm, wp)))
    xb = x_ref[...].astype(BF16)
    p = p_ref[...]
    for t in range(D_MODEL // COL_TILE):
        cs = slice(t * COL_TILE, (t + 1) * COL_TILE)
        gate = _sigmoid(_dot(xb, wg[:, cs]))
        o_ref[:, cs] = ALPHA * x_ref[:, cs] + gate * _dot(p, wp[:, cs])


def _ple(x1, p, wpg, wple):
    tm = ROW_TILE
    any_spec = pl.BlockSpec(memory_space=pl.ANY)
    return pl.pallas_call(
        _ple_kernel,
        grid=(M_ROWS // tm,),
        in_specs=[
            pl.BlockSpec((tm, D_MODEL), lambda i: (i, 0)),
            pl.BlockSpec((tm, PLE_DIM), lambda i: (i, 0)),
            any_spec, any_spec,
        ],
        out_specs=pl.BlockSpec((tm, D_MODEL), lambda i: (i, 0)),
        out_shape=jax.ShapeDtypeStruct((M_ROWS, D_MODEL), F32),
        scratch_shapes=[pltpu.VMEM((D_MODEL, D_MODEL), BF16), pltpu.VMEM((PLE_DIM, D_MODEL), BF16)],
        compiler_params=_cp(("arbitrary",)),
        name="ple_residual",
    )(x1, p, wpg, wple)


def _router_kernel(x_ref, w_ref, idx_ref, wt_ref, cnt_ref):
    xh, xm, _ = _split3(x_ref[...])
    wh, wm, _ = _split3(w_ref[...])
    logits = _dot(xh, wh) + _dot(xh, wm) + _dot(xm, wh)
    lane_i = lax.broadcasted_iota(jnp.int32, (x_ref.shape[0], LANES), 1)
    lane = lane_i.astype(F32)
    lane_group = (lane_i >> 3).astype(F32)
    neg = -jnp.inf

    def top1(v):
        mx = jnp.max(v, axis=1, keepdims=True)
        ix = jnp.min(jnp.where(v == mx, lane, float(LANES)), axis=1, keepdims=True)
        return mx, ix

    gl = jnp.where(lane_i < MOE_GROUPS, logits[:, :LANES], neg)
    g_max, g_idx = top1(gl)
    g_p = 1.0 / jnp.sum(jnp.exp(gl - g_max), axis=1, keepdims=True)
    el = logits[:, LANES:]
    in_group = jnp.logical_and(lane_i < N_EXPERTS, lane_group == g_idx)
    e1v = jnp.where(in_group, el, neg)
    m1, i1 = top1(e1v)
    m2, i2 = top1(jnp.where(lane == i1, neg, e1v))
    t = jnp.exp(m2 - m1)
    w1 = g_p / (1.0 + t)
    w2 = g_p * t / (1.0 + t)
    wt_ref[...] = jnp.where(lane_i == 0, w1, jnp.where(lane_i == 1, w2, 0.0))

    @pl.when(pl.program_id(0) == 0)
    def _():
        cnt_ref[...] = jnp.zeros_like(cnt_ref)

    tm = x_ref.shape[0]
    is1, is2 = lane == i1, lane == i2
    onehot = jnp.where(jnp.logical_or(is1, is2), 1.0, 0.0)
    r = lax.broadcasted_iota(jnp.int32, (tm, tm), 0)
    c = lax.broadcasted_iota(jnp.int32, (tm, tm), 1)
    before = _dot(jnp.where(r > c, 1.0, 0.0).astype(BF16), onehot.astype(BF16)) + cnt_ref[...]
    rank1 = jnp.sum(jnp.where(is1, before, 0.0), axis=1, keepdims=True)
    rank2 = jnp.sum(jnp.where(is2, before, 0.0), axis=1, keepdims=True)
    cnt_ref[...] += jnp.sum(onehot, axis=0, keepdims=True)
    idx_ref[...] = jnp.where(lane_i == 0, i1, jnp.where(lane_i == 1, i2, jnp.where(
        lane_i == 2, rank1, jnp.where(lane_i == 3, rank2, 0.0)))).astype(jnp.int32)


def _router(x1, w_r1, w_r2):
    w = jnp.zeros((D_MODEL, 2 * LANES), F32)
    w = w.at[:, :MOE_GROUPS].set(w_r1).at[:, LANES:LANES + N_EXPERTS].set(w_r2)
    tm = ROW_TILE
    return pl.pallas_call(
        _router_kernel,
        grid=(M_ROWS // tm,),
        in_specs=[pl.BlockSpec((tm, D_MODEL), lambda i: (i, 0)),
                  pl.BlockSpec((D_MODEL, 2 * LANES), lambda i: (0, 0))],
        out_specs=[pl.BlockSpec((tm, LANES), lambda i: (i, 0)),
                   pl.BlockSpec((tm, LANES), lambda i: (i, 0)),
                   pl.BlockSpec((1, LANES), lambda i: (0, 0))],
        out_shape=[jax.ShapeDtypeStruct((M_ROWS, LANES), jnp.int32),
                   jax.ShapeDtypeStruct((M_ROWS, LANES), F32),
                   jax.ShapeDtypeStruct((1, LANES), F32)],
        compiler_params=_cp(("arbitrary",)),
        name="router",
    )(x1, w)


WEIGHT_DMA_PRIORITY = 1


def _moe_kernel(be_ref, nx_ref, nv_ref, tok_ref, x_hbm, wg_hbm, wu_hbm, wd_hbm, y_ref,
                sg, su, sd, wgb, wub, wdb, xbuf, wsem, xsem):
    i = pl.program_id(0)
    nv = nv_ref[0]
    e = be_ref[i]

    def weight_copies(ex):
        return (pltpu.make_async_copy(wg_hbm.at[ex], sg, wsem.at[0]),
                pltpu.make_async_copy(wu_hbm.at[ex], su, wsem.at[1]),
                pltpu.make_async_copy(wd_hbm.at[ex], sd, wsem.at[2]))

    def for_rows(blk, slot, action):
        def body(r, carry):
            src = tok_ref[blk * MOE_BLK + r]
            action(pltpu.make_async_copy(x_hbm.at[pl.ds(src, 1)], xbuf.at[slot, pl.ds(r, 1)], xsem.at[slot]))
            return carry
        lax.fori_loop(0, MOE_BLK, body, 0, unroll=8)

    @pl.when(i == 0)
    def _():
        for cp in weight_copies(e):
            cp.start(priority=WEIGHT_DMA_PRIORITY)
        for_rows(0, 0, lambda cp: cp.start())

    @pl.when(i + 1 < nv)
    def _():
        for_rows(i + 1, (i + 1) & 1, lambda cp: cp.start())

    first_of_expert = jnp.logical_or(i == 0, e != be_ref[jnp.maximum(i - 1, 0)])

    @pl.when(jnp.logical_and(i < nv, first_of_expert))
    def _():
        for cp in weight_copies(e):
            cp.wait()
        for r in range(0, D_MODEL, MOE_BLK):
            wgb[r:r + MOE_BLK, :] = sg[r:r + MOE_BLK, :].astype(BF16)
            wub[r:r + MOE_BLK, :] = su[r:r + MOE_BLK, :].astype(BF16)
        for r in range(0, D_EXPERT, MOE_BLK):
            wdb[r:r + MOE_BLK, :] = sd[r:r + MOE_BLK, :].astype(BF16)

        @pl.when(nx_ref[i] >= 0)
        def _():
            for cp in weight_copies(nx_ref[i]):
                cp.start(priority=WEIGHT_DMA_PRIORITY)

    @pl.when(i < nv)
    def _():
        slot = i & 1
        for_rows(i, slot, lambda cp: cp.wait())
        x = xbuf[slot].astype(BF16)
        a = _dot(x, wgb[...])
        h = (a * _sigmoid(a) * _dot(x, wub[...])).astype(BF16)
        y_ref[...] = _dot(h, wdb[...])

    @pl.when(i >= nv)
    def _():
        y_ref[...] = jnp.zeros_like(y_ref)


def _moe_experts(x1, row_tok, blk_e, blk_next, n_valid, w_gate, w_up, w_down):
    any_spec = pl.BlockSpec(memory_space=pl.ANY)
    return pl.pallas_call(
        _moe_kernel,
        grid_spec=pltpu.PrefetchScalarGridSpec(
            num_scalar_prefetch=4,
            grid=(MOE_NBLK,),
            in_specs=[any_spec, any_spec, any_spec, any_spec],
            out_specs=pl.BlockSpec((MOE_BLK, D_MODEL), lambda i, be, nx, nv, tok: (i, 0)),
            scratch_shapes=[
                pltpu.VMEM((D_MODEL, D_EXPERT), F32), pltpu.VMEM((D_MODEL, D_EXPERT), F32),
                pltpu.VMEM((D_EXPERT, D_MODEL), F32),
                pltpu.VMEM((D_MODEL, D_EXPERT), BF16), pltpu.VMEM((D_MODEL, D_EXPERT), BF16),
                pltpu.VMEM((D_EXPERT, D_MODEL), BF16),
                pltpu.VMEM((2, MOE_BLK, D_MODEL), F32),
                pltpu.SemaphoreType.DMA((3,)), pltpu.SemaphoreType.DMA((2,)),
            ],
        ),
        out_shape=jax.ShapeDtypeStruct((MOE_ROWS, D_MODEL), F32),
        compiler_params=_cp(("arbitrary",)),
        name="moe_experts",
    )(blk_e, blk_next, n_valid, row_tok, x1, w_gate, w_up, w_down)


def _moe_plan(e_idx, rank, counts):
    ids = jnp.arange(N_EXPERTS, dtype=jnp.int32)
    pcounts = (counts + MOE_BLK - 1) // MOE_BLK * MOE_BLK
    pends = jnp.cumsum(pcounts)
    pstarts = pends - pcounts
    dest = jnp.sum(jnp.where(e_idx[..., None] == ids, pstarts, 0), axis=-1) + rank
    tok = jnp.broadcast_to(jnp.arange(M_ROWS, dtype=jnp.int32)[:, None], dest.shape)
    row_tok = jnp.zeros((MOE_ROWS,), jnp.int32).at[dest.reshape(-1)].set(tok.reshape(-1))
    n_valid = (pends[-1] // MOE_BLK).astype(jnp.int32)
    blk_start = jnp.minimum(jnp.arange(MOE_NBLK, dtype=jnp.int32) * MOE_BLK, pends[-1] - 1)
    blk_e = jnp.sum((pends[None, :] <= blk_start[:, None]).astype(jnp.int32), axis=1)
    blk_e = jnp.clip(blk_e, 0, N_EXPERTS - 1).astype(jnp.int32)
    later = jnp.where((counts[None, :] > 0) & (ids[None, :] > ids[:, None]), ids[None, :], N_EXPERTS)
    next_e = jnp.min(later, axis=1)
    next_e = jnp.where(next_e == N_EXPERTS, -1, next_e).astype(jnp.int32)
    blk_next = jnp.sum(jnp.where(blk_e[:, None] == ids, next_e, 0), axis=1).astype(jnp.int32)
    return dest, row_tok, blk_e, blk_next, n_valid.reshape(1)


def _final_row_copy(dest_ref, y_hbm, ybuf, sem, blk, slot, k, t):
    src = dest_ref[k * M_ROWS + blk * LN_ROWS + t]
    return pltpu.make_async_copy(y_hbm.at[pl.ds(src, 1)], ybuf.at[slot, pl.ds(k * LN_ROWS + t, 1)], sem.at[slot])


def _final_kernel(dest_ref, base_ref, wt_ref, g_ref, b_ref, y_hbm, op_ref, os_ref, ybuf, sem):
    i = pl.program_id(0)
    n_rows = 2 * LN_ROWS

    def for_rows(blk, slot, action):
        for k in range(2):
            def body(t, carry, k=k):
                action(_final_row_copy(dest_ref, y_hbm, ybuf, sem, blk, slot, k, t))
                return carry
            lax.fori_loop(0, LN_ROWS, body, 0, unroll=8)

    @pl.when(i == 0)
    def _():
        for_rows(0, 0, lambda cp: cp.start())

    @pl.when(i + 1 < pl.num_programs(0))
    def _():
        for_rows(i + 1, (i + 1) & 1, lambda cp: cp.start())

    slot = i & 1
    for_rows(i, slot, lambda cp: cp.wait())

    wt = wt_ref[...]
    r = base_ref[...] + wt[:, 0:1] * ybuf[slot, 0:LN_ROWS, :] + wt[:, 1:2] * ybuf[slot, LN_ROWS:n_rows, :]
    y = _layer_norm_rows(r, g_ref[...], b_ref[...])

    @pl.when(i < SEQ // LN_ROWS)
    def _():
        op_ref[...] = y

    @pl.when(i >= SEQ // LN_ROWS)
    def _():
        os_ref[...] = y


def _final(dest_flat, base, y_rows, wt, g, b):
    npb = SEQ // LN_ROWS
    return pl.pallas_call(
        _final_kernel,
        grid_spec=pltpu.PrefetchScalarGridSpec(
            num_scalar_prefetch=1,
            grid=(npb + 1,),
            in_specs=[
                pl.BlockSpec((LN_ROWS, D_MODEL), lambda i, d: (i, 0)),
                pl.BlockSpec((LN_ROWS, LANES), lambda i, d: (i, 0)),
                pl.BlockSpec((1, D_MODEL), lambda i, d: (0, 0)),
                pl.BlockSpec((1, D_MODEL), lambda i, d: (0, 0)),
                pl.BlockSpec(memory_space=pl.ANY),
            ],
            out_specs=[pl.BlockSpec((LN_ROWS, D_MODEL), lambda i, d: (jnp.minimum(i, npb - 1), 0)),
                       pl.BlockSpec((N_SAMPLE, D_MODEL), lambda i, d: (0, 0))],
            scratch_shapes=[pltpu.VMEM((2, 2 * LN_ROWS, D_MODEL), F32), pltpu.SemaphoreType.DMA((2,))],
        ),
        out_shape=[jax.ShapeDtypeStruct((SEQ, D_MODEL), F32),
                   jax.ShapeDtypeStruct((N_SAMPLE, D_MODEL), F32)],
        compiler_params=_cp(("arbitrary",)),
        name="combine_ln2",
    )(dest_flat, base, wt, g.reshape(1, D_MODEL), b.reshape(1, D_MODEL), y_rows)


def kernel(x_prompt, x_sample, p_prompt, p_sample, cache_kv_a1, cache_kv_a2, cache_kv_a3, state_gla,
           ln_emb_g, ln_emb_b, ln1_g, ln1_b, ln2_g, ln2_b, w_in, w_gk2, b_gk, gla_norm_g, w_br_a, w_br_b,
           w_out, w_router_group, w_router_expert, w_gate, w_up, w_down, w_ple_gate, w_ple):
    xp = x_prompt.reshape(SEQ, D_MODEL)
    xs = x_sample.reshape(N_SAMPLE, D_MODEL)
    xn, xb = _ln_emb(xp, xs, ln_emb_g, ln_emb_b)

    w_t = jnp.transpose(w_in[0])
    rope = _rope_tables()
    u = _project(xb, w_t, 0, A_QKV_COLS, 2 * 3 * A_WIDTH // COL_TILE, rope, F32)
    u_gla = _project(xb, w_t, A_QKV_COLS, MAIN_COLS - A_QKV_COLS, 0, rope, BF16)
    gates = _project(xb, w_t[MAIN_COLS + GLA_LR:], 0, 2 * D_MODEL, 0, rope, BF16)
    gk = _gla_gate(xb, w_t[MAIN_COLS:MAIN_COLS + GLA_LR], w_gk2[0], b_gk[0])

    qkv_s = u[SEQ:].reshape(DEC_BATCH, DEC_SEQ, 9, A_HEADS, A_HEAD_DIM)
    kv_new = [jnp.stack([qkv_s[:, :, 3 + g], qkv_s[:, :, 6 + g]], axis=2) for g in range(3)]
    o_a, kv_s3 = _attn_prompt(u, kv_new[2], cache_kv_a3)
    c1 = cache_kv_a1.reshape(DEC_BATCH, A_REACH, 2, A_HEADS, A_HEAD_DIM)
    c2 = cache_kv_a2.reshape(DEC_BATCH, A_REACH, 4 * 2, A_HEADS, A_HEAD_DIM)
    c3 = cache_kv_a3.reshape(DEC_BATCH, A_REACH, 16 * 2, A_HEADS, A_HEAD_DIM)
    o_as = _attn_sample(qkv_s, c1, c2, c3)
    o_a = jnp.concatenate([o_a, o_as.reshape(N_SAMPLE, A_WIDTH).astype(BF16)], axis=0)

    o_g, st_t, kv_s1, kv_s2 = _gla_prompt(u_gla, gk, gla_norm_g[0], kv_new[:2], (cache_kv_a1, cache_kv_a2))

    def spad(a):
        a = a.astype(F32).reshape(DEC_BATCH, DEC_SEQ, a.shape[-1])
        return jnp.pad(a, ((0, 0), (0, GLA_SPAD - DEC_SEQ), (0, 0)))

    ug_s = u_gla[SEQ:]
    o_gs, st_s = _gla_sample(
        spad(ug_s[:, :GLA_QK]), spad(ug_s[:, GLA_QK:2 * GLA_QK]),
        spad(ug_s[:, 2 * GLA_QK:2 * GLA_QK + GLA_V]), spad(ug_s[:, 2 * GLA_QK + GLA_V:]),
        spad(gk[SEQ:]), gla_norm_g[0], state_gla[0])
    o_gs = o_gs[:, :DEC_SEQ].reshape(N_SAMPLE, GLA_V).astype(BF16)
    o_g = jnp.concatenate([o_g, o_gs], axis=0)

    x1 = _merge_ln1(o_a, o_g, gates, xn, ln1_g[0], ln1_b[0],
                    w_br_a[0].astype(BF16), w_br_b[0].astype(BF16), w_out[0].astype(BF16))

    p = jnp.concatenate([p_prompt[0].reshape(SEQ, PLE_DIM), p_sample[0].reshape(N_SAMPLE, PLE_DIM)], axis=0)
    base = _ple(x1, p.astype(BF16), w_ple_gate[0].astype(BF16), w_ple[0].astype(BF16))
    ridx, rwt, rcnt = _router(x1, w_router_group[0], w_router_expert[0])
    dest, row_tok, blk_e, blk_next, n_valid = _moe_plan(
        ridx[:, :2], ridx[:, 2:4], rcnt[0, :N_EXPERTS].astype(jnp.int32))
    y_rows =_moe_experts(x1, row_tok, blk_e, blk_next, n_valid, w_gate[0], w_up[0], w_down[0])
    y_p, y_s = _final(dest.T.reshape(-1), base, y_rows, rwt, ln2_g[0], ln2_b[0])

    def kv_prompt(g):
        w = A_WINDOWS[g]
        k = u[SEQ - w:SEQ, (3 + g) * A_WIDTH:(4 + g) * A_WIDTH].reshape(w, A_HEADS, A_HEAD_DIM)
        v = u[SEQ - w:SEQ, (6 + g) * A_WIDTH:(7 + g) * A_WIDTH].reshape(w, A_HEADS, A_HEAD_DIM)
        return jnp.stack([k, v], axis=1)[None, None]


    gla_state_prompt = st_t[None, None]
    return (y_p.reshape(1, SEQ, D_MODEL), y_s.reshape(DEC_BATCH, DEC_SEQ, D_MODEL),
            kv_prompt(0), kv_prompt(1), kv_prompt(2), gla_state_prompt,
            kv_s1, kv_s2, kv_s3, st_s[None])
```

```python
import functools

import jax
import jax.numpy as jnp
import numpy as np
from jax import lax
from jax.experimental import pallas as pl
from jax.experimental.pallas import tpu as pltpu

F32 = jnp.float32
BF16 = jnp.bfloat16

D_MODEL = 2048
SEQ = 8192
DEC_BATCH = 32
DEC_SEQ = 4
PAST_LEN = 16384
N_SAMPLE = DEC_BATCH * DEC_SEQ
M_ROWS = SEQ + N_SAMPLE
A_WINDOWS = (128, 512, 2048)
A_DILATIONS = (1, 4, 16)
A_REACH = 128
A_HEADS = 8
A_HEAD_DIM = 128
A_WIDTH = A_HEADS * A_HEAD_DIM
ROT_DIM = 32
ROPE_THETA = 500000.0
A_QKV_COLS = 9 * A_WIDTH
GLA_HEADS = 4
GLA_DK = 256
GLA_DV = 512
GLA_LR = 16
GLA_TAU = 16.0
GLA_CHUNK = 64
GLA_QK = GLA_HEADS * GLA_DK
GLA_V = GLA_HEADS * GLA_DV
MAIN_COLS = A_QKV_COLS + 2 * GLA_QK + 2 * GLA_V
MOE_GROUPS = 4
MOE_PER_GROUP = 8
N_EXPERTS = 32
D_EXPERT = 1024
PLE_DIM = 256
LN_EPS = 1e-5
ALPHA = 2.0 ** 0.25

LANES = 128
VMEM_LIMIT = 56 * 1024 * 1024
ROW_TILE_PROJ = 1664
ROW_TILE = 832
MERGE_ROWS = 416
COL_TILE = 512
LN_ROWS = 128
MOE_BLK = 256
MOE_NBLK = (2 * M_ROWS) // MOE_BLK + N_EXPERTS
MOE_ROWS = MOE_NBLK * MOE_BLK
ATT_SB = 2048
GLA_ROWS = 512
ATT_VMEM_LIMIT = 60 * 1024 * 1024
SHIFT_SLOTS = 4
SHIFT_CHUNK = 511
ATT_UNROLL = 4
ATT_RESIDUE_MAJOR_D = 16
ATT_RESIDUE_PITCH = 264

def _cp(sem, vmem=VMEM_LIMIT):
    return pltpu.CompilerParams(dimension_semantics=sem, vmem_limit_bytes=vmem)


def _sigmoid(x):
    return 1.0 / (1.0 + jnp.exp(-x))


def _dot(a, b):
    return jnp.dot(a, b, preferred_element_type=F32)


def _dot_nt(a, b):
    return lax.dot_general(a, b, (((1,), (1,)), ((), ())), preferred_element_type=F32)


def _split3(x):
    h = x.astype(BF16)
    r = x - h.astype(F32)
    m = r.astype(BF16)
    l = (r - m.astype(F32)).astype(BF16)
    return h, m, l


def _layer_norm_rows(x, g, b):
    mu = jnp.mean(x, axis=-1, keepdims=True)
    xc = x - mu
    var = jnp.mean(xc * xc, axis=-1, keepdims=True)
    return xc * lax.rsqrt(var + LN_EPS) * g + b


LN_GROUP = 4


def _ln_emb_kernel(xp_ref, xs_ref, g_ref, b_ref, of_ref, ob_ref):
    is_prompt = pl.program_id(0) < SEQ // (LN_ROWS * LN_GROUP)
    x = jnp.where(is_prompt, xp_ref[...], xs_ref[...][None])
    y = _layer_norm_rows(x, g_ref[...], b_ref[...])
    of_ref[...] = y
    ob_ref[...] = y.astype(BF16)


def _ln_emb(xp, xs, g, b):
    n_groups = M_ROWS // LN_ROWS
    npb = SEQ // (LN_ROWS * LN_GROUP)
    blk = (LN_GROUP, LN_ROWS, D_MODEL)
    xn, xb = pl.pallas_call(
        _ln_emb_kernel,
        grid=(npb + 1,),
        in_specs=[
            pl.BlockSpec(blk, lambda i: (jnp.minimum(i, npb - 1), 0, 0)),
            pl.BlockSpec((N_SAMPLE, D_MODEL), lambda i: (0, 0)),
            pl.BlockSpec((1, D_MODEL), lambda i: (0, 0)),
            pl.BlockSpec((1, D_MODEL), lambda i: (0, 0)),
        ],
        out_specs=[pl.BlockSpec(blk, lambda i: (i, 0, 0)), pl.BlockSpec(blk, lambda i: (i, 0, 0))],
        out_shape=[jax.ShapeDtypeStruct((n_groups, LN_ROWS, D_MODEL), F32),
                   jax.ShapeDtypeStruct((n_groups, LN_ROWS, D_MODEL), BF16)],
        compiler_params=_cp(("arbitrary",)),
        name="ln_emb",
    )(xp.reshape(SEQ // LN_ROWS, LN_ROWS, D_MODEL), xs, g.reshape(1, D_MODEL), b.reshape(1, D_MODEL))
    return xn.reshape(M_ROWS, D_MODEL), xb.reshape(M_ROWS, D_MODEL)


def _proj_kernel(x_ref, w_ref, c_ref, s1_ref, s2_ref, o_ref, *, n_rope_blocks):
    j = pl.program_id(1)
    acc = _dot_nt(x_ref[...], w_ref[...].astype(BF16))

    @pl.when(j >= n_rope_blocks)
    def _():
        o_ref[...] = acc.astype(o_ref.dtype)

    @pl.when(j < n_rope_blocks)
    def _():
        c, s1, s2 = c_ref[...], s1_ref[...], s2_ref[...]
        for t in range(COL_TILE // LANES):
            a = acc[:, t * LANES:(t + 1) * LANES]
            o_ref[:, t * LANES:(t + 1) * LANES] = (
                a * c + pltpu.roll(a, LANES - ROT_DIM // 2, 1) * s1 + pltpu.roll(a, ROT_DIM // 2, 1) * s2
            ).astype(o_ref.dtype)


def _project(xb, w_t, row0, n_cols, n_rope_blocks, rope, out_dtype):
    c, s1, s2 = rope
    tm, tn = ROW_TILE_PROJ, COL_TILE
    jb0 = row0 // tn
    return pl.pallas_call(
        functools.partial(_proj_kernel, n_rope_blocks=n_rope_blocks),
        grid=(M_ROWS // tm, n_cols // tn),
        in_specs=[
            pl.BlockSpec((tm, D_MODEL), lambda i, j: (i, 0)),
            pl.BlockSpec((tn, D_MODEL), lambda i, j: (jb0 + j, 0)),
            pl.BlockSpec((tm, LANES), lambda i, j: (i, 0)),
            pl.BlockSpec((tm, LANES), lambda i, j: (i, 0)),
            pl.BlockSpec((tm, LANES), lambda i, j: (i, 0)),
        ],
        out_specs=pl.BlockSpec((tm, tn), lambda i, j: (i, j)),
        out_shape=jax.ShapeDtypeStruct((M_ROWS, n_cols), out_dtype),
        compiler_params=_cp(("arbitrary", "arbitrary")),
        name="in_proj",
    )(xb, w_t, c, s1, s2)


def _rope_tables():
    half = ROT_DIM // 2
    inv = np.power(np.float32(ROPE_THETA), -np.arange(half, dtype=np.float32) * np.float32(2.0 / ROT_DIM))
    pos = np.concatenate([np.arange(SEQ), np.tile(PAST_LEN + np.arange(DEC_SEQ), DEC_BATCH)]).astype(np.float32)
    ang = (pos[:, None] * inv[None, :].astype(np.float32)).astype(np.float32)
    cos, sin = np.cos(ang).astype(np.float32), np.sin(ang).astype(np.float32)
    ones = np.ones((M_ROWS, LANES - ROT_DIM), np.float32)
    zeros = np.zeros((M_ROWS, LANES - ROT_DIM), np.float32)
    zh = np.zeros((M_ROWS, half), np.float32)
    c = np.concatenate([cos, cos, ones], axis=1)
    s1 = np.concatenate([-sin, zh, zeros], axis=1)
    s2 = np.concatenate([zh, sin, zeros], axis=1)
    return jnp.asarray(c), jnp.asarray(s1), jnp.asarray(s2)


def _gk_kernel(x_ref, wlr_ref, wgk_ref, b_ref, o_ref):
    blr = _dot_nt(x_ref[...], wlr_ref[...])
    z = _dot(blr.astype(BF16), wgk_ref[...]) + b_ref[...]
    o_ref[...] = (jnp.minimum(z, 0.0) - jnp.log1p(jnp.exp(-jnp.abs(z)))) * (1.0 / GLA_TAU)


def _gla_gate(xb, w_lr_t, w_gk2, b_gk):
    wlr = jnp.zeros((LANES, D_MODEL), BF16).at[:GLA_LR].set(w_lr_t.astype(BF16))
    wgk = jnp.zeros((LANES, GLA_QK), BF16).at[:GLA_LR].set(w_gk2.astype(BF16))
    tm = ROW_TILE
    return pl.pallas_call(
        _gk_kernel,
        grid=(M_ROWS // tm,),
        in_specs=[
            pl.BlockSpec((tm, D_MODEL), lambda i: (i, 0)),
            pl.BlockSpec((LANES, D_MODEL), lambda i: (0, 0)),
            pl.BlockSpec((LANES, GLA_QK), lambda i: (0, 0)),
            pl.BlockSpec((1, GLA_QK), lambda i: (0, 0)),
        ],
        out_specs=pl.BlockSpec((tm, GLA_QK), lambda i: (i, 0)),
        out_shape=jax.ShapeDtypeStruct((M_ROWS, GLA_QK), F32),
        compiler_params=_cp(("arbitrary",)),
        name="gla_gate",
    )(xb, wlr, wgk, b_gk.reshape(1, GLA_QK))


class _CacheShiftInBackground:
    def __init__(self, step, n_steps, pieces, news, buf, rsem, wsem, nsem):
        self.step, self.n_steps, self.pieces, self.news = step, n_steps, pieces, news
        self.buf, self.rsem, self.wsem, self.nsem = buf, rsem, wsem, nsem

    def _copies(self, step):
        out = []
        for slot, (cache, dst, b, pos0, n) in enumerate(self.pieces(step)):
            stage = self.buf.at[slot, pl.ds(0, n)]
            rd = pltpu.make_async_copy(cache.at[0, b, pl.ds(DEC_SEQ + pos0, n)], stage, self.rsem.at[slot])
            wr = pltpu.make_async_copy(stage, dst.at[0, b, pl.ds(pos0, n)], self.wsem.at[slot])
            out.append((rd, wr))
        return out

    def _new_rows(self):
        return [pltpu.make_async_copy(new, dst.at[0, :, pl.ds(w - DEC_SEQ, DEC_SEQ)], self.nsem.at[i])
                for i, (new, dst, w) in enumerate(self.news)]

    def top_of_step(self):
        @pl.when(self.step == 0)
        def _():
            for cp in self._new_rows():
                cp.start()

        @pl.when(self.step > 0)
        def _():
            for _, wr in self._copies(self.step - 1):
                wr.wait()

        for rd, _ in self._copies(self.step):
            rd.start()

    def turn_reads_into_writes(self):
        for rd, wr in self._copies(self.step):
            rd.wait()
            wr.start()

    def end_of_step(self):
        @pl.when(self.step == self.n_steps - 1)
        def _():
            for _, wr in self._copies(self.step):
                wr.wait()
            for cp in self._new_rows():
                cp.wait()


def _gla_chunks(heads, gn, causal, tri_b, eye):
    n = heads[0][3].shape[0]
    rid = lax.broadcasted_iota(jnp.int32, (n + GLA_TPAD, GLA_DK), 0)
    bs = []
    for _, _, _, g, _, _ in heads:
        g1, g2, g3 = _split3(g)
        bs.append(_dot(tri_b, g1) + _dot(tri_b, g2) + _dot(tri_b, g3))
    ops = []
    for (q, k, v, _, _, _), b in zip(heads, bs):
        q, k = q.astype(F32), k.astype(F32)
        b_last = b[n - 1:n, :]
        q_dec = (q * ((GLA_DK ** -0.5) * jnp.exp(b))).astype(BF16)
        k_inv = (k * jnp.exp(-b)).astype(BF16)
        k_dec = k * jnp.exp(b_last - b)
        d1, d2, d3 = _split3(jnp.exp(b_last))
        rows = jnp.concatenate([k_dec, jnp.zeros((GLA_TPAD, GLA_DK), F32)], axis=0)
        rows = jnp.where(rid == n, d1.astype(F32), rows)
        rows = jnp.where(rid == n + 1, d2.astype(F32), rows)
        rows = jnp.where(rid == n + 2, d3.astype(F32), rows)
        ops.append((q_dec, k_inv, rows.astype(BF16), v.astype(BF16)))
    mids = []
    for q_dec, k_inv, rows, _ in ops:
        mids.append((jnp.where(causal, _dot_nt(q_dec, k_inv), 0.0).astype(BF16), _dot_nt(eye, rows)))
    outs = []
    for (q_dec, _, _, vb), (att, cols), (_, _, _, _, br, s0) in zip(ops, mids, heads):
        o = _dot(att, vb) + _dot(q_dec, s0.astype(BF16))
        dcol = cols[:, n:n + 1] + cols[:, n + 1:n + 2] + cols[:, n + 2:n + 3]
        s1 = s0 * dcol + _dot(cols[:, :n].astype(BF16), vb)
        br = br.astype(F32)
        ms = jnp.mean(o * o, axis=-1, keepdims=True)
        outs.append((o * lax.rsqrt(ms + LN_EPS) * gn * (br * _sigmoid(br)), s1))
    return outs


GLA_TPAD = 16


def _gla_consts(n):
    r = lax.broadcasted_iota(jnp.int32, (n, n), 0)
    c = lax.broadcasted_iota(jnp.int32, (n, n), 1)
    causal = r >= c
    er = lax.broadcasted_iota(jnp.int32, (GLA_DK, GLA_DK), 0)
    ec = lax.broadcasted_iota(jnp.int32, (GLA_DK, GLA_DK), 1)
    return causal, jnp.where(causal, 1.0, 0.0).astype(BF16), jnp.where(er == ec, 1.0, 0.0).astype(BF16)


def _gla_prompt_kernel(q_ref, k_ref, v_ref, br_ref, g_ref, gn_ref, n1_ref, n2_ref, c1_hbm, c2_hbm,
                       o_ref, st_ref, kv1_hbm, kv2_hbm, cbuf, rsem, wsem, nsem):
    step = pl.program_id(0)
    n_steps = SEQ // GLA_ROWS
    per_step = DEC_BATCH // n_steps

    def pieces(s):
        out = []
        for cache, dst, w in ((c1_hbm, kv1_hbm, A_WINDOWS[0]), (c2_hbm, kv2_hbm, A_WINDOWS[1])):
            out += [(cache, dst, s * per_step + j, 0, w - DEC_SEQ) for j in range(per_step)]
        return out

    shifter = _CacheShiftInBackground(
        step, n_steps, pieces, [(n1_ref, kv1_hbm, A_WINDOWS[0]), (n2_ref, kv2_hbm, A_WINDOWS[1])],
        cbuf, rsem, wsem, nsem)
    shifter.top_of_step()

    @pl.when(step == 0)
    def _():
        st_ref[...] = jnp.zeros_like(st_ref)

    causal, tri_b, eye = _gla_consts(GLA_CHUNK)
    gn = gn_ref[...]

    def body(cc, carry):
        rows = pl.ds(pl.multiple_of(cc * GLA_CHUNK, GLA_CHUNK), GLA_CHUNK)
        heads = []
        for h in range(GLA_HEADS):
            ks = slice(h * GLA_DK, (h + 1) * GLA_DK)
            vs = slice(h * GLA_DV, (h + 1) * GLA_DV)
            heads.append((q_ref[rows, ks], k_ref[rows, ks], v_ref[rows, vs], g_ref[rows, ks],
                          br_ref[rows, vs], st_ref[h]))
        for h, (o, s1) in enumerate(_gla_chunks(heads, gn, causal, tri_b, eye)):
            st_ref[h] = s1
            o_ref[rows, h * GLA_DV:(h + 1) * GLA_DV] = o.astype(BF16)
        return carry

    half = GLA_ROWS // GLA_CHUNK // 2
    lax.fori_loop(0, half, body, 0)
    shifter.turn_reads_into_writes()
    lax.fori_loop(half, 2 * half, body, 0)
    shifter.end_of_step()


def _gla_prompt(u, gk, gnorm, news, caches):
    n_steps = SEQ // GLA_ROWS
    n_slots = 2 * (DEC_BATCH // n_steps)
    any_spec = pl.BlockSpec(memory_space=pl.ANY)
    return pl.pallas_call(
        _gla_prompt_kernel,
        grid=(n_steps,),
        in_specs=[
            pl.BlockSpec((GLA_ROWS, GLA_QK), lambda c: (c, 0)),
            pl.BlockSpec((GLA_ROWS, GLA_QK), lambda c: (c, 1)),
            pl.BlockSpec((GLA_ROWS, GLA_V), lambda c: (c, 1)),
            pl.BlockSpec((GLA_ROWS, GLA_V), lambda c: (c, 2)),
            pl.BlockSpec((GLA_ROWS, GLA_QK), lambda c: (c, 0)),
            pl.BlockSpec((1, GLA_DV), lambda c: (0, 0)),
            pl.BlockSpec(news[0].shape, lambda c: (0, 0, 0, 0, 0)),
            pl.BlockSpec(news[1].shape, lambda c: (0, 0, 0, 0, 0)),
            any_spec, any_spec,
        ],
        out_specs=[
            pl.BlockSpec((GLA_ROWS, GLA_V), lambda c: (c, 0)),
            pl.BlockSpec((GLA_HEADS, GLA_DK, GLA_DV), lambda c: (0, 0, 0)),
            any_spec, any_spec,
        ],
        out_shape=[jax.ShapeDtypeStruct((SEQ, GLA_V), BF16),
                   jax.ShapeDtypeStruct((GLA_HEADS, GLA_DK, GLA_DV), F32),
                   jax.ShapeDtypeStruct(caches[0].shape, caches[0].dtype),
                   jax.ShapeDtypeStruct(caches[1].shape, caches[1].dtype)],
        scratch_shapes=[pltpu.VMEM((n_slots, A_WINDOWS[1] - DEC_SEQ, 2, A_HEADS, A_HEAD_DIM), F32),
                        pltpu.SemaphoreType.DMA((n_slots,)), pltpu.SemaphoreType.DMA((n_slots,)),
                        pltpu.SemaphoreType.DMA((2,))],
        compiler_params=_cp(("arbitrary",)),
        name="gla_prompt",
    )(u, u, u, u, gk, gnorm.reshape(1, GLA_DV), news[0], news[1], caches[0], caches[1])


GLA_SPAD = 16


def _gla_sample_kernel(q_ref, k_ref, v_ref, br_ref, g_ref, gn_ref, s0_ref, o_ref, s1_ref):
    causal, tri_b, eye = _gla_consts(GLA_SPAD)
    gn = gn_ref[...]
    heads = []
    for h in range(GLA_HEADS):
        ks = slice(h * GLA_DK, (h + 1) * GLA_DK)
        vs = slice(h * GLA_DV, (h + 1) * GLA_DV)
        heads.append((q_ref[0, :, ks], k_ref[0, :, ks], v_ref[0, :, vs], g_ref[0, :, ks],
                      br_ref[0, :, vs], s0_ref[0, h]))
    for h, (o, s1) in enumerate(_gla_chunks(heads, gn, causal, tri_b, eye)):
        s1_ref[0, h] = s1
        o_ref[0, :, h * GLA_DV:(h + 1) * GLA_DV] = o


def _gla_sample(q, k, v, br, g, gnorm, s0):
    def spec(c):
        return pl.BlockSpec((1, GLA_SPAD, c), lambda b: (b, 0, 0))
    st_spec = pl.BlockSpec((1, GLA_HEADS, GLA_DK, GLA_DV), lambda b: (b, 0, 0, 0))
    return pl.pallas_call(
        _gla_sample_kernel,
        grid=(DEC_BATCH,),
        in_specs=[spec(GLA_QK), spec(GLA_QK), spec(GLA_V), spec(GLA_V), spec(GLA_QK),
                  pl.BlockSpec((1, GLA_DV), lambda b: (0, 0)), st_spec],
        out_specs=[spec(GLA_V), st_spec],
        out_shape=[jax.ShapeDtypeStruct((DEC_BATCH, GLA_SPAD, GLA_V), F32),
                   jax.ShapeDtypeStruct((DEC_BATCH, GLA_HEADS, GLA_DK, GLA_DV), F32)],
        compiler_params=_cp(("arbitrary",)),
        name="gla_sample",
    )(q, k, v, br, g, gnorm.reshape(1, GLA_DV), s0)


def _attn_prompt_kernel(q0, q1, q2, kc0, kp0, vc0, vp0, kc1, kp1, vc1, vp1, kc2, kp2, vc2, vp2,
                        new3_ref, cache3_hbm, o_ref, kv3_hbm,
                        kb0, vb0, kb1, vb1, kb2, vb2, acc_ref, m_ref, l_ref, cbuf, rsem, wsem, nsem):
    sb = pl.program_id(0)
    step = sb * pl.num_programs(1) + pl.program_id(1)

    def pieces(s):
        return [(cache3_hbm, kv3_hbm, s, k * SHIFT_CHUNK, SHIFT_CHUNK) for k in range(SHIFT_SLOTS)]

    shifter = _CacheShiftInBackground(step, DEC_BATCH, pieces, [(new3_ref, kv3_hbm, A_WINDOWS[2])],
                                      cbuf, rsem, wsem, nsem)
    shifter.top_of_step()
    groups = ((q0, kc0, kp0, vc0, vp0, kb0, vb0, 1),
              (q1, kc1, kp1, vc1, vp1, kb1, vb1, 4),
              (q2, kc2, kp2, vc2, vp2, kb2, vb2, 16))
    scale = A_HEAD_DIM ** -0.5
    n_win = 2 * A_REACH
    row = lax.broadcasted_iota(jnp.int32, (A_REACH, n_win), 0)
    col = lax.broadcasted_iota(jnp.int32, (A_REACH, n_win), 1)
    in_reach = jnp.logical_and(col >= row, col <= row + A_REACH)
    in_prev = col < A_REACH
    ones = jnp.ones((n_win, LANES), BF16)

    for gi, (q_ref, kc, kp, vc, vp, kb, vb, d) in enumerate(groups):
        win = A_REACH * d
        by_residue = d == ATT_RESIDUE_MAJOR_D
        if by_residue:
            def stage(jj, carry, kp=kp, kc=kc, vp=vp, vc=vc, kb=kb, vb=vb, d=d):
                src = pl.ds(pl.multiple_of(jj * d, d), d)
                for buf, prev, cur in ((kb, kp, kc), (vb, vp, vc)):
                    buf[pl.ds(jj, d, stride=ATT_RESIDUE_PITCH), :] = prev[src, :]
                    buf[pl.ds(A_REACH + jj, d, stride=ATT_RESIDUE_PITCH), :] = cur[src, :]
                return carry
            lax.fori_loop(0, A_REACH, stage, 0, unroll=8)
        else:
            kb[0:win, :] = kp[...]
            kb[win:win + ATT_SB, :] = kc[...]
            vb[0:win, :] = vp[...]
            vb[win:win + ATT_SB, :] = vc[...]
        shift = {1: 0, 4: 2, 16: 4}[d]

        def tiles(it, carry, q_ref=q_ref, kb=kb, vb=vb, d=d, win=win, shift=shift, gi=gi, by_residue=by_residue):
            rows, scores = [], []
            for j in range(ATT_UNROLL):
                idx = it * ATT_UNROLL + j
                r = idx & (d - 1)
                nb = idx >> shift
                start = r + nb * win
                if d == 1:
                    start = pl.multiple_of(start, A_REACH)
                    qrows = pl.ds(start, A_REACH)
                    wrows = pl.ds(start, n_win)
                elif by_residue:
                    qrows = pl.ds(start, A_REACH, stride=d)
                    wrows = pl.ds(pl.multiple_of(r * ATT_RESIDUE_PITCH, 8), n_win)
                else:
                    qrows = pl.ds(start, A_REACH, stride=d)
                    wrows = pl.ds(start, n_win, stride=d)
                prev_bias = jnp.where(jnp.logical_or(sb > 0, nb > 0), 0.0, -jnp.inf)
                rows.append((qrows, wrows, prev_bias))
                scores.append(_dot_nt(q_ref[qrows, :].astype(BF16), kb[wrows, :].astype(BF16)))
            probs = []
            for s, (_, _, prev_bias) in zip(scores, rows):
                s = jnp.where(in_reach, s * scale, -jnp.inf) + jnp.where(in_prev, prev_bias, 0.0)
                m_t = jnp.max(s, axis=1, keepdims=True)
                probs.append((jnp.exp(s - m_t).astype(BF16), jnp.broadcast_to(m_t, (A_REACH, LANES))))
            sums = []
            for (p, _), (_, wrows, _) in zip(probs, rows):
                sums.append((_dot(p, ones), _dot(p, vb[wrows, :].astype(BF16))))
            for (l_t, num), (_, m_t), (qrows, _, _) in zip(sums, probs, rows):
                if gi == 0:
                    acc_ref[qrows, :] = num
                    m_ref[qrows, :] = m_t
                    l_ref[qrows, :] = l_t
                else:
                    m_o = m_ref[qrows, :]
                    m_n = jnp.maximum(m_o, m_t)
                    a = jnp.exp(m_o - m_n)
                    b = jnp.exp(m_t - m_n)
                    acc_ref[qrows, :] = a * acc_ref[qrows, :] + b * num
                    l_ref[qrows, :] = a * l_ref[qrows, :] + b * l_t
                    m_ref[qrows, :] = m_n
            return carry

        lax.fori_loop(0, ATT_SB // A_REACH // ATT_UNROLL, tiles, 0)
        if gi == 1:
            shifter.turn_reads_into_writes()

    o_ref[...] = (acc_ref[...] / l_ref[...]).astype(BF16)
    shifter.end_of_step()


def _attn_prompt(u, new3, cache3):
    nh = A_HEADS
    assert (SEQ // ATT_SB) * nh == DEC_BATCH and SHIFT_SLOTS * SHIFT_CHUNK == A_WINDOWS[2] - DEC_SEQ
    in_specs, scratch = [], []
    for g in range(3):
        in_specs.append(pl.BlockSpec((ATT_SB, LANES), lambda sb, h, g=g: (sb, g * nh + h)))
    for g, d in enumerate(A_DILATIONS):
        win = A_REACH * d
        per = ATT_SB // win
        for part in (3, 6):
            cb = part * nh + g * nh
            in_specs.append(pl.BlockSpec((ATT_SB, LANES), lambda sb, h, cb=cb: (sb, cb + h)))
            in_specs.append(pl.BlockSpec(
                (win, LANES), lambda sb, h, cb=cb, per=per: (jnp.maximum(sb * per - 1, 0), cb + h)))
        buf_rows = d * ATT_RESIDUE_PITCH if d == ATT_RESIDUE_MAJOR_D else win + ATT_SB
        scratch += [pltpu.VMEM((buf_rows, LANES), F32), pltpu.VMEM((buf_rows, LANES), F32)]
    scratch += [pltpu.VMEM((ATT_SB, LANES), F32)] * 3
    scratch += [pltpu.VMEM((SHIFT_SLOTS, SHIFT_CHUNK + 1, 2, A_HEADS, A_HEAD_DIM), F32),
                pltpu.SemaphoreType.DMA((SHIFT_SLOTS,)), pltpu.SemaphoreType.DMA((SHIFT_SLOTS,)),
                pltpu.SemaphoreType.DMA((1,))]
    in_specs += [pl.BlockSpec(new3.shape, lambda sb, h: (0, 0, 0, 0, 0)), pl.BlockSpec(memory_space=pl.ANY)]
    return pl.pallas_call(
        _attn_prompt_kernel,
        grid=(SEQ // ATT_SB, nh),
        in_specs=in_specs,
        out_specs=[pl.BlockSpec((ATT_SB, LANES), lambda sb, h: (sb, h)), pl.BlockSpec(memory_space=pl.ANY)],
        out_shape=[jax.ShapeDtypeStruct((SEQ, A_WIDTH), BF16), jax.ShapeDtypeStruct(cache3.shape, cache3.dtype)],
        scratch_shapes=scratch,
        compiler_params=_cp(("arbitrary", "arbitrary"), vmem=ATT_VMEM_LIMIT),
        name="attn_prompt",
    )(*([u] * 15), new3, cache3)


def _attn_sample_kernel(qkv_ref, c1_ref, c2_ref, c3_ref, o_ref):
    scale = A_HEAD_DIM ** -0.5
    pos = lax.broadcasted_iota(jnp.int32, (A_REACH, 1, 1), 0)

    def merge(state, m_t, l_t, num):
        if state is None:
            return m_t, l_t, num
        m_o, l_o, n_o = state
        m_n = jnp.maximum(m_o, m_t)
        a, b = jnp.exp(m_o - m_n), jnp.exp(m_t - m_n)
        return m_n, a * l_o + b * l_t, a * n_o + b * num

    for s in range(DEC_SEQ):
        state = None
        for g in range(3):
            q = qkv_ref[0, s, g]
            if g == 0:
                kt, vt = c1_ref[0, :, 0], c1_ref[0, :, 1]
                valid = pos >= s
                new_rows = range(s + 1)
            else:
                c_ref = c2_ref if g == 1 else c3_ref
                kt, vt = c_ref[0, :, 2 * s], c_ref[0, :, 2 * s + 1]
                valid = None
                new_rows = (s,)
            sc = jnp.sum(kt * q[None], axis=-1, keepdims=True) * scale
            if valid is not None:
                sc = jnp.where(valid, sc, -jnp.inf)
            s_new = [jnp.sum(qkv_ref[0, t, 3 + g] * q, axis=-1, keepdims=True) * scale for t in new_rows]
            m_t = jnp.max(sc, axis=0)
            for sn in s_new:
                m_t = jnp.maximum(m_t, sn)
            p = jnp.exp(sc - m_t[None])
            l_t = jnp.sum(p, axis=0)
            num = jnp.sum(p * vt, axis=0)
            for t, sn in zip(new_rows, s_new):
                pn = jnp.exp(sn - m_t)
                l_t = l_t + pn
                num = num + pn * qkv_ref[0, t, 6 + g]
            state = merge(state, m_t, l_t, num)
        o_ref[0, s] = state[2] / state[1]


def _attn_sample(qkv_s, c1, c2, c3):
    tile = (A_HEADS, A_HEAD_DIM)
    return pl.pallas_call(
        _attn_sample_kernel,
        grid=(DEC_BATCH,),
        in_specs=[
            pl.BlockSpec((1, DEC_SEQ, 9) + tile, lambda b: (b, 0, 0, 0, 0)),
            pl.BlockSpec((1, A_REACH, 2) + tile, lambda b: (b, 0, 0, 0, 0)),
            pl.BlockSpec((1, A_REACH, 2 * DEC_SEQ) + tile, lambda b: (b, 0, 0, 0, 0)),
            pl.BlockSpec((1, A_REACH, 2 * DEC_SEQ) + tile, lambda b: (b, 0, 0, 0, 0)),
        ],
        out_specs=pl.BlockSpec((1, DEC_SEQ) + tile, lambda b: (b, 0, 0, 0)),
        out_shape=jax.ShapeDtypeStruct((DEC_BATCH, DEC_SEQ) + tile, F32),
        compiler_params=_cp(("arbitrary",)),
        name="attn_sample",
    )(qkv_s, c1, c2, c3)


def _load_resident(step, pairs):
    @pl.when(step == 0)
    def _():
        for src, dst in pairs:
            pltpu.sync_copy(src, dst)


def _merge_ln1_kernel(oa_ref, og_ref, gt_ref, xn_ref, g_ref, b_ref, wa_hbm, wb_hbm, wo_hbm,
                      x1_ref, wa, wb, wo, mixed, resid):
    _load_resident(pl.program_id(0), ((wa_hbm, wa), (wb_hbm, wb), (wo_hbm, wo)))
    nj = D_MODEL // COL_TILE
    oa, og = oa_ref[...], og_ref[...]
    for t in range(nj):
        cs = slice(t * COL_TILE, (t + 1) * COL_TILE)
        gs = slice(D_MODEL + t * COL_TILE, D_MODEL + (t + 1) * COL_TILE)
        ya = _dot(oa, wa[:, cs])
        yb = _dot(og, wb[:, cs])
        ga, gb = gt_ref[:, cs].astype(F32), gt_ref[:, gs].astype(F32)
        mixed[:, cs] = (_sigmoid(ga) * ya + _sigmoid(gb) * yb).astype(BF16)
    mx = mixed[...]
    for t in range(nj):
        cs = slice(t * COL_TILE, (t + 1) * COL_TILE)
        resid[:, cs] = ALPHA * xn_ref[:, cs] + _dot(mx, wo[:, cs])
    x1_ref[...] = _layer_norm_rows(resid[...], g_ref[...], b_ref[...])


def _merge_ln1(o_a, o_g, gates, xn, g, b, wa, wb, wo):
    tm = MERGE_ROWS
    any_spec = pl.BlockSpec(memory_space=pl.ANY)
    return pl.pallas_call(
        _merge_ln1_kernel,
        grid=(M_ROWS // tm,),
        in_specs=[
            pl.BlockSpec((tm, A_WIDTH), lambda i: (i, 0)),
            pl.BlockSpec((tm, GLA_V), lambda i: (i, 0)),
            pl.BlockSpec((tm, 2 * D_MODEL), lambda i: (i, 0)),
            pl.BlockSpec((tm, D_MODEL), lambda i: (i, 0)),
            pl.BlockSpec((1, D_MODEL), lambda i: (0, 0)),
            pl.BlockSpec((1, D_MODEL), lambda i: (0, 0)),
            any_spec, any_spec, any_spec,
        ],
        out_specs=pl.BlockSpec((tm, D_MODEL), lambda i: (i, 0)),
        out_shape=jax.ShapeDtypeStruct((M_ROWS, D_MODEL), F32),
        scratch_shapes=[pltpu.VMEM((A_WIDTH, D_MODEL), BF16), pltpu.VMEM((GLA_V, D_MODEL), BF16),
                        pltpu.VMEM((D_MODEL, D_MODEL), BF16),
                        pltpu.VMEM((tm, D_MODEL), BF16), pltpu.VMEM((tm, D_MODEL), F32)],
        compiler_params=_cp(("arbitrary",)),
        name="merge_ln1",
    )(o_a, o_g, gates, xn, g.reshape(1, D_MODEL), b.reshape(1, D_MODEL), wa, wb, wo)


def _route_block(x, w, idx_ref, wt_ref, cnt_ref):
    tm = x.shape[0]
    xh, xm, _ = _split3(x)
    wh, wm, _ = _split3(w)
    logits = _dot(xh, wh) + _dot(xh, wm) + _dot(xm, wh)
    lane_i = lax.broadcasted_iota(jnp.int32, (tm, LANES), 1)
    lane = lane_i.astype(F32)
    lane_group = (lane_i >> 3).astype(F32)
    neg = -jnp.inf

    def top1(v):
        mx = jnp.max(v, axis=1, keepdims=True)
        ix = jnp.min(jnp.where(v == mx, lane, float(LANES)), axis=1, keepdims=True)
        return mx, ix

    gl = jnp.where(lane_i < MOE_GROUPS, logits[:, :LANES], neg)
    g_max, g_idx = top1(gl)
    g_p = 1.0 / jnp.sum(jnp.exp(gl - g_max), axis=1, keepdims=True)
    el = logits[:, LANES:]
    in_group = jnp.logical_and(lane_i < N_EXPERTS, lane_group == g_idx)
    e1v = jnp.where(in_group, el, neg)
    m1, i1 = top1(e1v)
    m2, i2 = top1(jnp.where(lane == i1, neg, e1v))
    t = jnp.exp(m2 - m1)
    w1 = g_p / (1.0 + t)
    w2 = g_p * t / (1.0 + t)
    wt_ref[...] = jnp.where(lane_i == 0, w1, jnp.where(lane_i == 1, w2, 0.0))

    @pl.when(pl.program_id(0) == 0)
    def _():
        cnt_ref[...] = jnp.zeros_like(cnt_ref)

    is1, is2 = lane == i1, lane == i2
    onehot = jnp.where(jnp.logical_or(is1, is2), 1.0, 0.0)
    r = lax.broadcasted_iota(jnp.int32, (tm, tm), 0)
    c = lax.broadcasted_iota(jnp.int32, (tm, tm), 1)
    before = _dot(jnp.where(r > c, 1.0, 0.0).astype(BF16), onehot.astype(BF16)) + cnt_ref[...]
    rank1 = jnp.sum(jnp.where(is1, before, 0.0), axis=1, keepdims=True)
    rank2 = jnp.sum(jnp.where(is2, before, 0.0), axis=1, keepdims=True)
    cnt_ref[...] += jnp.sum(onehot, axis=0, keepdims=True)
    idx_ref[...] = jnp.where(lane_i == 0, i1, jnp.where(lane_i == 1, i2, jnp.where(
        lane_i == 2, rank1, jnp.where(lane_i == 3, rank2, 0.0)))).astype(jnp.int32)


def _ple_router_kernel(x_ref, p_ref, wr_ref, wg_hbm, wp_hbm, o_ref, idx_ref, wt_ref, cnt_ref, wg, wp):
    _load_resident(pl.program_id(0), ((wg_hbm, wg), (wp_hbm, wp)))
    x = x_ref[...]
    xb = x.astype(BF16)
    p = p_ref[...]
    for t in range(D_MODEL // COL_TILE):
        cs = slice(t * COL_TILE, (t + 1) * COL_TILE)
        gate = _sigmoid(_dot(xb, wg[:, cs]))
        o_ref[:, cs] = ALPHA * x_ref[:, cs] + gate * _dot(p, wp[:, cs])
    _route_block(x, wr_ref[...], idx_ref, wt_ref, cnt_ref)


def _ple_router(x1, p, wpg, wple, w_r1, w_r2):
    wr = jnp.zeros((D_MODEL, 2 * LANES), F32)
    wr = wr.at[:, :MOE_GROUPS].set(w_r1).at[:, LANES:LANES + N_EXPERTS].set(w_r2)
    tm = ROW_TILE
    any_spec = pl.BlockSpec(memory_space=pl.ANY)
    return pl.pallas_call(
        _ple_router_kernel,
        grid=(M_ROWS // tm,),
        in_specs=[
            pl.BlockSpec((tm, D_MODEL), lambda i: (i, 0)),
            pl.BlockSpec((tm, PLE_DIM), lambda i: (i, 0)),
            pl.BlockSpec((D_MODEL, 2 * LANES), lambda i: (0, 0)),
            any_spec, any_spec,
        ],
        out_specs=[pl.BlockSpec((tm, D_MODEL), lambda i: (i, 0)),
                   pl.BlockSpec((tm, LANES), lambda i: (i, 0)),
                   pl.BlockSpec((tm, LANES), lambda i: (i, 0)),
                   pl.BlockSpec((1, LANES), lambda i: (0, 0))],
        out_shape=[jax.ShapeDtypeStruct((M_ROWS, D_MODEL), F32),
                   jax.ShapeDtypeStruct((M_ROWS, LANES), jnp.int32),
                   jax.ShapeDtypeStruct((M_ROWS, LANES), F32),
                   jax.ShapeDtypeStruct((1, LANES), F32)],
        scratch_shapes=[pltpu.VMEM((D_MODEL, D_MODEL), BF16), pltpu.VMEM((PLE_DIM, D_MODEL), BF16)],
        compiler_params=_cp(("arbitrary",)),
        name="ple_router",
    )(x1, p, wr, wpg, wple)


WEIGHT_DMA_PRIORITY = 1


def _moe_kernel(be_ref, nx_ref, nv_ref, tok_ref, x_hbm, wg_hbm, wu_hbm, wd_hbm, y_ref,
                sg, su, sd, wgb, wub, wdb, xbuf, wsem, xsem):
    i = pl.program_id(0)
    nv = nv_ref[0]
    e = be_ref[i]

    def weight_copies(ex):
        return (pltpu.make_async_copy(wg_hbm.at[ex], sg, wsem.at[0]),
                pltpu.make_async_copy(wu_hbm.at[ex], su, wsem.at[1]),
                pltpu.make_async_copy(wd_hbm.at[ex], sd, wsem.at[2]))

    def row_copy(blk, slot, r):
        src = tok_ref[blk * MOE_BLK + r]
        return pltpu.make_async_copy(x_hbm.at[pl.ds(src, 1)], xbuf.at[slot, pl.ds(r, 1)], xsem.at[slot])

    def for_rows(blk, slot, action):
        for r in range(MOE_BLK):
            action(row_copy(blk, slot, r))

    @pl.when(i == 0)
    def _():
        for cp in weight_copies(e):
            cp.start(priority=WEIGHT_DMA_PRIORITY)

        def body(r, carry):
            row_copy(0, 0, r).start()
            return carry
        lax.fori_loop(0, MOE_BLK, body, 0, unroll=8)

    @pl.when(i + 1 < nv)
    def _():
        for_rows(i + 1, (i + 1) & 1, lambda cp: cp.start())

    first_of_expert = jnp.logical_or(i == 0, e != be_ref[jnp.maximum(i - 1, 0)])

    @pl.when(jnp.logical_and(i < nv, first_of_expert))
    def _():
        for cp in weight_copies(e):
            cp.wait()
        for r in range(0, D_MODEL, MOE_BLK):
            wgb[r:r + MOE_BLK, :] = sg[r:r + MOE_BLK, :].astype(BF16)
            wub[r:r + MOE_BLK, :] = su[r:r + MOE_BLK, :].astype(BF16)
        for r in range(0, D_EXPERT, MOE_BLK):
            wdb[r:r + MOE_BLK, :] = sd[r:r + MOE_BLK, :].astype(BF16)

        @pl.when(nx_ref[i] >= 0)
        def _():
            for cp in weight_copies(nx_ref[i]):
                cp.start(priority=WEIGHT_DMA_PRIORITY)

    @pl.when(i < nv)
    def _():
        slot = i & 1
        for_rows(i, slot, lambda cp: cp.wait())
        x = xbuf[slot].astype(BF16)
        a = _dot(x, wgb[...])
        h = (a * _sigmoid(a) * _dot(x, wub[...])).astype(BF16)
        y_ref[...] = _dot(h, wdb[...])

    @pl.when(i >= nv)
    def _():
        y_ref[...] = jnp.zeros_like(y_ref)


def _moe_experts(x1, row_tok, blk_e, blk_next, n_valid, w_gate, w_up, w_down):
    any_spec = pl.BlockSpec(memory_space=pl.ANY)
    return pl.pallas_call(
        _moe_kernel,
        grid_spec=pltpu.PrefetchScalarGridSpec(
            num_scalar_prefetch=4,
            grid=(MOE_NBLK,),
            in_specs=[any_spec, any_spec, any_spec, any_spec],
            out_specs=pl.BlockSpec((MOE_BLK, D_MODEL), lambda i, be, nx, nv, tok: (i, 0)),
            scratch_shapes=[
                pltpu.VMEM((D_MODEL, D_EXPERT), F32), pltpu.VMEM((D_MODEL, D_EXPERT), F32),
                pltpu.VMEM((D_EXPERT, D_MODEL), F32),
                pltpu.VMEM((D_MODEL, D_EXPERT), BF16), pltpu.VMEM((D_MODEL, D_EXPERT), BF16),
                pltpu.VMEM((D_EXPERT, D_MODEL), BF16),
                pltpu.VMEM((2, MOE_BLK, D_MODEL), F32),
                pltpu.SemaphoreType.DMA((3,)), pltpu.SemaphoreType.DMA((2,)),
            ],
        ),
        out_shape=jax.ShapeDtypeStruct((MOE_ROWS, D_MODEL), F32),
        compiler_params=_cp(("arbitrary",)),
        name="moe_experts",
    )(blk_e, blk_next, n_valid, row_tok, x1, w_gate, w_up, w_down)


def _moe_plan(e_idx, rank, counts):
    ids = jnp.arange(N_EXPERTS, dtype=jnp.int32)
    pcounts = (counts + MOE_BLK - 1) // MOE_BLK * MOE_BLK
    pends = jnp.cumsum(pcounts)
    pstarts = pends - pcounts
    dest = jnp.sum(jnp.where(e_idx[..., None] == ids, pstarts, 0), axis=-1) + rank
    tok = jnp.broadcast_to(jnp.arange(M_ROWS, dtype=jnp.int32)[:, None], dest.shape)
    row_tok = jnp.zeros((MOE_ROWS,), jnp.int32).at[dest.reshape(-1)].set(tok.reshape(-1))
    n_valid = (pends[-1] // MOE_BLK).astype(jnp.int32)
    blk_start = jnp.minimum(jnp.arange(MOE_NBLK, dtype=jnp.int32) * MOE_BLK, pends[-1] - 1)
    blk_e = jnp.sum((pends[None, :] <= blk_start[:, None]).astype(jnp.int32), axis=1)
    blk_e = jnp.clip(blk_e, 0, N_EXPERTS - 1).astype(jnp.int32)
    later = jnp.where((counts[None, :] > 0) & (ids[None, :] > ids[:, None]), ids[None, :], N_EXPERTS)
    next_e = jnp.min(later, axis=1)
    next_e = jnp.where(next_e == N_EXPERTS, -1, next_e).astype(jnp.int32)
    blk_next = jnp.sum(jnp.where(blk_e[:, None] == ids, next_e, 0), axis=1).astype(jnp.int32)
    return dest, row_tok, blk_e, blk_next, n_valid.reshape(1)


def _final_row_copy(dest_ref, y_hbm, ybuf, sem, blk, slot, k, t):
    src = dest_ref[k * M_ROWS + blk * LN_ROWS + t]
    return pltpu.make_async_copy(y_hbm.at[pl.ds(src, 1)], ybuf.at[slot, pl.ds(k * LN_ROWS + t, 1)], sem.at[slot])


def _final_kernel(dest_ref, base_ref, wt_ref, g_ref, b_ref, y_hbm, op_ref, os_ref, ybuf, sem):
    i = pl.program_id(0)
    n_rows = 2 * LN_ROWS

    def for_rows(blk, slot, action):
        for k in range(2):
            for t in range(LN_ROWS):
                action(_final_row_copy(dest_ref, y_hbm, ybuf, sem, blk, slot, k, t))

    @pl.when(i == 0)
    def _():
        for k in range(2):
            def body(t, carry, k=k):
                _final_row_copy(dest_ref, y_hbm, ybuf, sem, 0, 0, k, t).start()
                return carry
            lax.fori_loop(0, LN_ROWS, body, 0, unroll=8)

    @pl.when(i + 1 < pl.num_programs(0))
    def _():
        for_rows(i + 1, (i + 1) & 1, lambda cp: cp.start())

    slot = i & 1
    for_rows(i, slot, lambda cp: cp.wait())

    wt = wt_ref[...]
    r = base_ref[...] + wt[:, 0:1] * ybuf[slot, 0:LN_ROWS, :] + wt[:, 1:2] * ybuf[slot, LN_ROWS:n_rows, :]
    y = _layer_norm_rows(r, g_ref[...], b_ref[...])

    @pl.when(i < SEQ // LN_ROWS)
    def _():
        op_ref[...] = y

    @pl.when(i >= SEQ // LN_ROWS)
    def _():
        os_ref[...] = y


def _final(dest_flat, base, y_rows, wt, g, b):
    npb = SEQ // LN_ROWS
    return pl.pallas_call(
        _final_kernel,
        grid_spec=pltpu.PrefetchScalarGridSpec(
            num_scalar_prefetch=1,
            grid=(npb + 1,),
            in_specs=[
                pl.BlockSpec((LN_ROWS, D_MODEL), lambda i, d: (i, 0)),
                pl.BlockSpec((LN_ROWS, LANES), lambda i, d: (i, 0)),
                pl.BlockSpec((1, D_MODEL), lambda i, d: (0, 0)),
                pl.BlockSpec((1, D_MODEL), lambda i, d: (0, 0)),
                pl.BlockSpec(memory_space=pl.ANY),
            ],
            out_specs=[pl.BlockSpec((LN_ROWS, D_MODEL), lambda i, d: (jnp.minimum(i, npb - 1), 0)),
                       pl.BlockSpec((N_SAMPLE, D_MODEL), lambda i, d: (0, 0))],
            scratch_shapes=[pltpu.VMEM((2, 2 * LN_ROWS, D_MODEL), F32), pltpu.SemaphoreType.DMA((2,))],
        ),
        out_shape=[jax.ShapeDtypeStruct((SEQ, D_MODEL), F32),
                   jax.ShapeDtypeStruct((N_SAMPLE, D_MODEL), F32)],
        compiler_params=_cp(("arbitrary",)),
        name="combine_ln2",
    )(dest_flat, base, wt, g.reshape(1, D_MODEL), b.reshape(1, D_MODEL), y_rows)


def kernel(x_prompt, x_sample, p_prompt, p_sample, cache_kv_a1, cache_kv_a2, cache_kv_a3, state_gla,
           ln_emb_g, ln_emb_b, ln1_g, ln1_b, ln2_g, ln2_b, w_in, w_gk2, b_gk, gla_norm_g, w_br_a, w_br_b,
           w_out, w_router_group, w_router_expert, w_gate, w_up, w_down, w_ple_gate, w_ple):
    xp = x_prompt.reshape(SEQ, D_MODEL)
    xs = x_sample.reshape(N_SAMPLE, D_MODEL)
    xn, xb = _ln_emb(xp, xs, ln_emb_g, ln_emb_b)

    w_t = jnp.transpose(w_in[0])
    rope = _rope_tables()
    u = _project(xb, w_t, 0, A_QKV_COLS, 2 * 3 * A_WIDTH // COL_TILE, rope, F32)
    u_gla = _project(xb, w_t, A_QKV_COLS, MAIN_COLS - A_QKV_COLS, 0, rope, BF16)
    gates = _project(xb, w_t[MAIN_COLS + GLA_LR:], 0, 2 * D_MODEL, 0, rope, BF16)
    gk = _gla_gate(xb, w_t[MAIN_COLS:MAIN_COLS + GLA_LR], w_gk2[0], b_gk[0])

    qkv_s = u[SEQ:].reshape(DEC_BATCH, DEC_SEQ, 9, A_HEADS, A_HEAD_DIM)
    kv_new = [jnp.stack([qkv_s[:, :, 3 + g], qkv_s[:, :, 6 + g]], axis=2) for g in range(3)]
    o_a, kv_s3 = _attn_prompt(u, kv_new[2], cache_kv_a3)
    c1 = cache_kv_a1.reshape(DEC_BATCH, A_REACH, 2, A_HEADS, A_HEAD_DIM)
    c2 = cache_kv_a2.reshape(DEC_BATCH, A_REACH, 4 * 2, A_HEADS, A_HEAD_DIM)
    c3 = cache_kv_a3.reshape(DEC_BATCH, A_REACH, 16 * 2, A_HEADS, A_HEAD_DIM)
    o_as = _attn_sample(qkv_s, c1, c2, c3)
    o_a = jnp.concatenate([o_a, o_as.reshape(N_SAMPLE, A_WIDTH).astype(BF16)], axis=0)

    o_g, st_t, kv_s1, kv_s2 = _gla_prompt(u_gla, gk, gla_norm_g[0], kv_new[:2], (cache_kv_a1, cache_kv_a2))

    def spad(a):
        a = a.astype(F32).reshape(DEC_BATCH, DEC_SEQ, a.shape[-1])
        return jnp.pad(a, ((0, 0), (0, GLA_SPAD - DEC_SEQ), (0, 0)))

    ug_s = u_gla[SEQ:]
    o_gs, st_s = _gla_sample(
        spad(ug_s[:, :GLA_QK]), spad(ug_s[:, GLA_QK:2 * GLA_QK]),
        spad(ug_s[:, 2 * GLA_QK:2 * GLA_QK + GLA_V]), spad(ug_s[:, 2 * GLA_QK + GLA_V:]),
        spad(gk[SEQ:]), gla_norm_g[0], state_gla[0])
    o_gs = o_gs[:, :DEC_SEQ].reshape(N_SAMPLE, GLA_V).astype(BF16)
    o_g = jnp.concatenate([o_g, o_gs], axis=0)

    x1 = _merge_ln1(o_a, o_g, gates, xn, ln1_g[0], ln1_b[0],
                    w_br_a[0].astype(BF16), w_br_b[0].astype(BF16), w_out[0].astype(BF16))

    p = jnp.concatenate([p_prompt[0].reshape(SEQ, PLE_DIM), p_sample[0].reshape(N_SAMPLE, PLE_DIM)], axis=0)
    base, ridx, rwt, rcnt = _ple_router(x1, p.astype(BF16), w_ple_gate[0].astype(BF16), w_ple[0].astype(BF16),
                                        w_router_group[0], w_router_expert[0])
    dest, row_tok, blk_e, blk_next, n_valid = _moe_plan(
        ridx[:, :2], ridx[:, 2:4], rcnt[0, :N_EXPERTS].astype(jnp.int32))
    y_rows =_moe_experts(x1, row_tok, blk_e, blk_next, n_valid, w_gate[0], w_up[0], w_down[0])
    y_p, y_s = _final(dest.T.reshape(-1), base, y_rows, rwt, ln2_g[0], ln2_b[0])

    def kv_prompt(g):
        w = A_WINDOWS[g]
        k = u[SEQ - w:SEQ, (3 + g) * A_WIDTH:(4 + g) * A_WIDTH].reshape(w, A_HEADS, A_HEAD_DIM)
        v = u[SEQ - w:SEQ, (6 + g) * A_WIDTH:(7 + g) * A_WIDTH].reshape(w, A_HEADS, A_HEAD_DIM)
        return jnp.stack([k, v], axis=1)[None, None]


    gla_state_prompt = st_t[None, None]
    return (y_p.reshape(1, SEQ, D_MODEL), y_s.reshape(DEC_BATCH, DEC_SEQ, D_MODEL),
            kv_prompt(0), kv_prompt(1), kv_prompt(2), gla_state_prompt,
            kv_s1, kv_s2, kv_s3, st_s[None])
```

```python
import functools

import jax
import jax.numpy as jnp
import numpy as np
from jax import lax
from jax.experimental import pallas as pl
from jax.experimental.pallas import tpu as pltpu

F32 = jnp.float32
BF16 = jnp.bfloat16

D_MODEL = 2048
SEQ = 8192
DEC_BATCH = 32
DEC_SEQ = 4
PAST_LEN = 16384
N_SAMPLE = DEC_BATCH * DEC_SEQ
M_ROWS = SEQ + N_SAMPLE
A_WINDOWS = (128, 512, 2048)
A_DILATIONS = (1, 4, 16)
A_REACH = 128
A_HEADS = 8
A_HEAD_DIM = 128
A_WIDTH = A_HEADS * A_HEAD_DIM
ROT_DIM = 32
ROPE_THETA = 500000.0
A_QKV_COLS = 9 * A_WIDTH
GLA_HEADS = 4
GLA_DK = 256
GLA_DV = 512
GLA_LR = 16
GLA_TAU = 16.0
GLA_CHUNK = 64
GLA_QK = GLA_HEADS * GLA_DK
GLA_V = GLA_HEADS * GLA_DV
MAIN_COLS = A_QKV_COLS + 2 * GLA_QK + 2 * GLA_V
MOE_GROUPS = 4
MOE_PER_GROUP = 8
N_EXPERTS = 32
D_EXPERT = 1024
PLE_DIM = 256
LN_EPS = 1e-5
ALPHA = 2.0 ** 0.25

LANES = 128
VMEM_LIMIT = 56 * 1024 * 1024
ROW_TILE_PROJ = 2080
ROW_TILE = 832
MERGE_ROWS = 416
COL_TILE = 512
LN_ROWS = 128
MOE_BLK = 256
MOE_NBLK = (2 * M_ROWS) // MOE_BLK + N_EXPERTS
MOE_ROWS = MOE_NBLK * MOE_BLK
ATT_SB = 2048
GLA_ROWS = 512
ATT_VMEM_LIMIT = 60 * 1024 * 1024
SHIFT_SLOTS = 4
SHIFT_CHUNK = 511
ATT_UNROLL = 4
ATT_RESIDUE_MAJOR_D = 16
ATT_RESIDUE_PITCH = 264

def _cp(sem, vmem=VMEM_LIMIT):
    return pltpu.CompilerParams(dimension_semantics=sem, vmem_limit_bytes=vmem)


def _sigmoid(x):
    return 1.0 / (1.0 + jnp.exp(-x))


def _dot(a, b):
    return jnp.dot(a, b, preferred_element_type=F32)


def _dot_nt(a, b):
    return lax.dot_general(a, b, (((1,), (1,)), ((), ())), preferred_element_type=F32)


def _split3(x):
    h = x.astype(BF16)
    r = x - h.astype(F32)
    m = r.astype(BF16)
    l = (r - m.astype(F32)).astype(BF16)
    return h, m, l


def _layer_norm_rows(x, g, b):
    mu = jnp.mean(x, axis=-1, keepdims=True)
    xc = x - mu
    var = jnp.mean(xc * xc, axis=-1, keepdims=True)
    return xc * lax.rsqrt(var + LN_EPS) * g + b


LN_GROUP = 4


def _ln_emb_kernel(xp_ref, xs_ref, g_ref, b_ref, of_ref, ob_ref):
    is_prompt = pl.program_id(0) < SEQ // (LN_ROWS * LN_GROUP)
    x = jnp.where(is_prompt, xp_ref[...], xs_ref[...][None])
    y = _layer_norm_rows(x, g_ref[...], b_ref[...])
    of_ref[...] = y
    ob_ref[...] = y.astype(BF16)


def _ln_emb(xp, xs, g, b):
    n_groups = M_ROWS // LN_ROWS
    npb = SEQ // (LN_ROWS * LN_GROUP)
    blk = (LN_GROUP, LN_ROWS, D_MODEL)
    xn, xb = pl.pallas_call(
        _ln_emb_kernel,
        grid=(npb + 1,),
        in_specs=[
            pl.BlockSpec(blk, lambda i: (jnp.minimum(i, npb - 1), 0, 0)),
            pl.BlockSpec((N_SAMPLE, D_MODEL), lambda i: (0, 0)),
            pl.BlockSpec((1, D_MODEL), lambda i: (0, 0)),
            pl.BlockSpec((1, D_MODEL), lambda i: (0, 0)),
        ],
        out_specs=[pl.BlockSpec(blk, lambda i: (i, 0, 0)), pl.BlockSpec(blk, lambda i: (i, 0, 0))],
        out_shape=[jax.ShapeDtypeStruct((n_groups, LN_ROWS, D_MODEL), F32),
                   jax.ShapeDtypeStruct((n_groups, LN_ROWS, D_MODEL), BF16)],
        compiler_params=_cp(("arbitrary",)),
        name="ln_emb",
    )(xp.reshape(SEQ // LN_ROWS, LN_ROWS, D_MODEL), xs, g.reshape(1, D_MODEL), b.reshape(1, D_MODEL))
    return xn.reshape(M_ROWS, D_MODEL), xb.reshape(M_ROWS, D_MODEL)


def _proj_kernel(x_ref, w_ref, c_ref, s1_ref, s2_ref, o_ref, *, n_rope_blocks):
    j = pl.program_id(1)
    acc = _dot_nt(x_ref[...], w_ref[...].astype(BF16))

    @pl.when(j >= n_rope_blocks)
    def _():
        o_ref[...] = acc.astype(o_ref.dtype)

    @pl.when(j < n_rope_blocks)
    def _():
        c, s1, s2 = c_ref[...], s1_ref[...], s2_ref[...]
        for t in range(COL_TILE // LANES):
            a = acc[:, t * LANES:(t + 1) * LANES]
            o_ref[:, t * LANES:(t + 1) * LANES] = (
                a * c + pltpu.roll(a, LANES - ROT_DIM // 2, 1) * s1 + pltpu.roll(a, ROT_DIM // 2, 1) * s2
            ).astype(o_ref.dtype)


def _project(xb, w_t, row0, n_cols, n_rope_blocks, rope, out_dtype):
    c, s1, s2 = rope
    tm, tn = ROW_TILE_PROJ, COL_TILE
    return pl.pallas_call(
        functools.partial(_proj_kernel, n_rope_blocks=n_rope_blocks),
        grid=(M_ROWS // tm, n_cols // tn),
        in_specs=[
            pl.BlockSpec((tm, D_MODEL), lambda i, j: (i, 0)),
            pl.BlockSpec((pl.Element(tn), pl.Element(D_MODEL)),
                         lambda i, j: (pl.multiple_of(row0 + j * tn, 8), 0)),
            pl.BlockSpec((tm, LANES), lambda i, j: (i, 0)),
            pl.BlockSpec((tm, LANES), lambda i, j: (i, 0)),
            pl.BlockSpec((tm, LANES), lambda i, j: (i, 0)),
        ],
        out_specs=pl.BlockSpec((tm, tn), lambda i, j: (i, j)),
        out_shape=jax.ShapeDtypeStruct((M_ROWS, n_cols), out_dtype),
        compiler_params=_cp(("arbitrary", "arbitrary")),
        name="in_proj",
    )(xb, w_t, c, s1, s2)


def _rope_tables():
    half = ROT_DIM // 2
    inv = np.power(np.float32(ROPE_THETA), -np.arange(half, dtype=np.float32) * np.float32(2.0 / ROT_DIM))
    pos = np.concatenate([np.arange(SEQ), np.tile(PAST_LEN + np.arange(DEC_SEQ), DEC_BATCH)]).astype(np.float32)
    ang = (pos[:, None] * inv[None, :].astype(np.float32)).astype(np.float32)
    cos, sin = np.cos(ang).astype(np.float32), np.sin(ang).astype(np.float32)
    ones = np.ones((M_ROWS, LANES - ROT_DIM), np.float32)
    zeros = np.zeros((M_ROWS, LANES - ROT_DIM), np.float32)
    zh = np.zeros((M_ROWS, half), np.float32)
    c = np.concatenate([cos, cos, ones], axis=1)
    s1 = np.concatenate([-sin, zh, zeros], axis=1)
    s2 = np.concatenate([zh, sin, zeros], axis=1)
    return jnp.asarray(c), jnp.asarray(s1), jnp.asarray(s2)


def _gk_kernel(x_ref, wlr_ref, wgk_ref, b_ref, o_ref):
    blr = _dot_nt(x_ref[...], wlr_ref[...])
    z = _dot(blr.astype(BF16), wgk_ref[...]) + b_ref[...]
    o_ref[...] = (jnp.minimum(z, 0.0) - jnp.log1p(jnp.exp(-jnp.abs(z)))) * (1.0 / GLA_TAU)


def _gla_gate(xb, w_lr_t, w_gk2, b_gk):
    wlr = jnp.zeros((LANES, D_MODEL), BF16).at[:GLA_LR].set(w_lr_t.astype(BF16))
    wgk = jnp.zeros((LANES, GLA_QK), BF16).at[:GLA_LR].set(w_gk2.astype(BF16))
    tm = ROW_TILE
    return pl.pallas_call(
        _gk_kernel,
        grid=(M_ROWS // tm,),
        in_specs=[
            pl.BlockSpec((tm, D_MODEL), lambda i: (i, 0)),
            pl.BlockSpec((LANES, D_MODEL), lambda i: (0, 0)),
            pl.BlockSpec((LANES, GLA_QK), lambda i: (0, 0)),
            pl.BlockSpec((1, GLA_QK), lambda i: (0, 0)),
        ],
        out_specs=pl.BlockSpec((tm, GLA_QK), lambda i: (i, 0)),
        out_shape=jax.ShapeDtypeStruct((M_ROWS, GLA_QK), F32),
        compiler_params=_cp(("arbitrary",)),
        name="gla_gate",
    )(xb, wlr, wgk, b_gk.reshape(1, GLA_QK))


class _CacheShiftInBackground:
    def __init__(self, step, n_steps, pieces, news, buf, rsem, wsem, nsem):
        self.step, self.n_steps, self.pieces, self.news = step, n_steps, pieces, news
        self.buf, self.rsem, self.wsem, self.nsem = buf, rsem, wsem, nsem

    def _copies(self, step):
        out = []
        for slot, (cache, dst, b, pos0, n) in enumerate(self.pieces(step)):
            stage = self.buf.at[slot, pl.ds(0, n)]
            rd = pltpu.make_async_copy(cache.at[0, b, pl.ds(DEC_SEQ + pos0, n)], stage, self.rsem.at[slot])
            wr = pltpu.make_async_copy(stage, dst.at[0, b, pl.ds(pos0, n)], self.wsem.at[slot])
            out.append((rd, wr))
        return out

    def _new_rows(self):
        return [pltpu.make_async_copy(new, dst.at[0, :, pl.ds(w - DEC_SEQ, DEC_SEQ)], self.nsem.at[i])
                for i, (new, dst, w) in enumerate(self.news)]

    def top_of_step(self):
        @pl.when(self.step == 0)
        def _():
            for cp in self._new_rows():
                cp.start()

        @pl.when(self.step > 0)
        def _():
            for _, wr in self._copies(self.step - 1):
                wr.wait()

        for rd, _ in self._copies(self.step):
            rd.start()

    def turn_reads_into_writes(self):
        for rd, wr in self._copies(self.step):
            rd.wait()
            wr.start()

    def end_of_step(self):
        @pl.when(self.step == self.n_steps - 1)
        def _():
            for _, wr in self._copies(self.step):
                wr.wait()
            for cp in self._new_rows():
                cp.wait()


def _gla_chunks(heads, gn, causal, tri_b, eye):
    n = heads[0][3].shape[0]
    rid = lax.broadcasted_iota(jnp.int32, (n + GLA_TPAD, GLA_DK), 0)
    bs = []
    for _, _, _, g, _, _ in heads:
        g1, g2, g3 = _split3(g)
        bs.append(_dot(tri_b, g1) + _dot(tri_b, g2) + _dot(tri_b, g3))
    ops = []
    for (q, k, v, _, _, _), b in zip(heads, bs):
        q, k = q.astype(F32), k.astype(F32)
        b_last = b[n - 1:n, :]
        q_dec = (q * ((GLA_DK ** -0.5) * jnp.exp(b))).astype(BF16)
        k_inv = (k * jnp.exp(-b)).astype(BF16)
        k_dec = k * jnp.exp(b_last - b)
        d1, d2, d3 = _split3(jnp.exp(b_last))
        rows = jnp.concatenate([k_dec, jnp.zeros((GLA_TPAD, GLA_DK), F32)], axis=0)
        rows = jnp.where(rid == n, d1.astype(F32), rows)
        rows = jnp.where(rid == n + 1, d2.astype(F32), rows)
        rows = jnp.where(rid == n + 2, d3.astype(F32), rows)
        ops.append((q_dec, k_inv, rows.astype(BF16), v.astype(BF16)))
    mids = []
    for q_dec, k_inv, rows, _ in ops:
        mids.append((jnp.where(causal, _dot_nt(q_dec, k_inv), 0.0).astype(BF16), _dot_nt(eye, rows)))
    outs = []
    for (q_dec, _, _, vb), (att, cols), (_, _, _, _, br, s0) in zip(ops, mids, heads):
        o = _dot(att, vb) + _dot(q_dec, s0.astype(BF16))
        dcol = cols[:, n:n + 1] + cols[:, n + 1:n + 2] + cols[:, n + 2:n + 3]
        s1 = s0 * dcol + _dot(cols[:, :n].astype(BF16), vb)
        br = br.astype(F32)
        ms = jnp.mean(o * o, axis=-1, keepdims=True)
        outs.append((o * lax.rsqrt(ms + LN_EPS) * gn * (br * _sigmoid(br)), s1))
    return outs


GLA_TPAD = 16


def _gla_consts(n):
    r = lax.broadcasted_iota(jnp.int32, (n, n), 0)
    c = lax.broadcasted_iota(jnp.int32, (n, n), 1)
    causal = r >= c
    er = lax.broadcasted_iota(jnp.int32, (GLA_DK, GLA_DK), 0)
    ec = lax.broadcasted_iota(jnp.int32, (GLA_DK, GLA_DK), 1)
    return causal, jnp.where(causal, 1.0, 0.0).astype(BF16), jnp.where(er == ec, 1.0, 0.0).astype(BF16)


def _gla_prompt_kernel(q_ref, k_ref, v_ref, br_ref, g_ref, gn_ref, n1_ref, n2_ref, c1_hbm, c2_hbm,
                       o_ref, st_ref, kv1_hbm, kv2_hbm, cbuf, rsem, wsem, nsem):
    step = pl.program_id(0)
    n_steps = SEQ // GLA_ROWS
    per_step = DEC_BATCH // n_steps

    def pieces(s):
        out = []
        for cache, dst, w in ((c1_hbm, kv1_hbm, A_WINDOWS[0]), (c2_hbm, kv2_hbm, A_WINDOWS[1])):
            out += [(cache, dst, s * per_step + j, 0, w - DEC_SEQ) for j in range(per_step)]
        return out

    shifter = _CacheShiftInBackground(
        step, n_steps, pieces, [(n1_ref, kv1_hbm, A_WINDOWS[0]), (n2_ref, kv2_hbm, A_WINDOWS[1])],
        cbuf, rsem, wsem, nsem)
    shifter.top_of_step()

    @pl.when(step == 0)
    def _():
        st_ref[...] = jnp.zeros_like(st_ref)

    causal, tri_b, eye = _gla_consts(GLA_CHUNK)
    gn = gn_ref[...]

    def body(cc, carry):
        rows = pl.ds(pl.multiple_of(cc * GLA_CHUNK, GLA_CHUNK), GLA_CHUNK)
        heads = []
        for h in range(GLA_HEADS):
            ks = slice(h * GLA_DK, (h + 1) * GLA_DK)
            vs = slice(h * GLA_DV, (h + 1) * GLA_DV)
            heads.append((q_ref[rows, ks], k_ref[rows, ks], v_ref[rows, vs], g_ref[rows, ks],
                          br_ref[rows, vs], st_ref[h]))
        for h, (o, s1) in enumerate(_gla_chunks(heads, gn, causal, tri_b, eye)):
            st_ref[h] = s1
            o_ref[rows, h * GLA_DV:(h + 1) * GLA_DV] = o.astype(BF16)
        return carry

    half = GLA_ROWS // GLA_CHUNK // 2
    lax.fori_loop(0, half, body, 0)
    shifter.turn_reads_into_writes()
    lax.fori_loop(half, 2 * half, body, 0)
    shifter.end_of_step()


def _gla_prompt(u, gk, gnorm, news, caches):
    n_steps = SEQ // GLA_ROWS
    n_slots = 2 * (DEC_BATCH // n_steps)
    any_spec = pl.BlockSpec(memory_space=pl.ANY)
    return pl.pallas_call(
        _gla_prompt_kernel,
        grid=(n_steps,),
        in_specs=[
            pl.BlockSpec((GLA_ROWS, GLA_QK), lambda c: (c, 0)),
            pl.BlockSpec((GLA_ROWS, GLA_QK), lambda c: (c, 1)),
            pl.BlockSpec((GLA_ROWS, GLA_V), lambda c: (c, 1)),
            pl.BlockSpec((GLA_ROWS, GLA_V), lambda c: (c, 2)),
            pl.BlockSpec((GLA_ROWS, GLA_QK), lambda c: (c, 0)),
            pl.BlockSpec((1, GLA_DV), lambda c: (0, 0)),
            pl.BlockSpec(news[0].shape, lambda c: (0, 0, 0, 0, 0)),
            pl.BlockSpec(news[1].shape, lambda c: (0, 0, 0, 0, 0)),
            any_spec, any_spec,
        ],
        out_specs=[
            pl.BlockSpec((GLA_ROWS, GLA_V), lambda c: (c, 0)),
            pl.BlockSpec((GLA_HEADS, GLA_DK, GLA_DV), lambda c: (0, 0, 0)),
            any_spec, any_spec,
        ],
        out_shape=[jax.ShapeDtypeStruct((SEQ, GLA_V), BF16),
                   jax.ShapeDtypeStruct((GLA_HEADS, GLA_DK, GLA_DV), F32),
                   jax.ShapeDtypeStruct(caches[0].shape, caches[0].dtype),
                   jax.ShapeDtypeStruct(caches[1].shape, caches[1].dtype)],
        scratch_shapes=[pltpu.VMEM((n_slots, A_WINDOWS[1] - DEC_SEQ, 2, A_HEADS, A_HEAD_DIM), F32),
                        pltpu.SemaphoreType.DMA((n_slots,)), pltpu.SemaphoreType.DMA((n_slots,)),
                        pltpu.SemaphoreType.DMA((2,))],
        compiler_params=_cp(("arbitrary",)),
        name="gla_prompt",
    )(u, u, u, u, gk, gnorm.reshape(1, GLA_DV), news[0], news[1], caches[0], caches[1])


GLA_SPAD = 16


def _gla_sample_kernel(q_ref, k_ref, v_ref, br_ref, g_ref, gn_ref, s0_ref, o_ref, s1_ref):
    causal, tri_b, eye = _gla_consts(GLA_SPAD)
    gn = gn_ref[...]
    heads = []
    for h in range(GLA_HEADS):
        ks = slice(h * GLA_DK, (h + 1) * GLA_DK)
        vs = slice(h * GLA_DV, (h + 1) * GLA_DV)
        heads.append((q_ref[0, :, ks], k_ref[0, :, ks], v_ref[0, :, vs], g_ref[0, :, ks],
                      br_ref[0, :, vs], s0_ref[0, h]))
    for h, (o, s1) in enumerate(_gla_chunks(heads, gn, causal, tri_b, eye)):
        s1_ref[0, h] = s1
        o_ref[0, :, h * GLA_DV:(h + 1) * GLA_DV] = o


def _gla_sample(q, k, v, br, g, gnorm, s0):
    def spec(c):
        return pl.BlockSpec((1, GLA_SPAD, c), lambda b: (b, 0, 0))
    st_spec = pl.BlockSpec((1, GLA_HEADS, GLA_DK, GLA_DV), lambda b: (b, 0, 0, 0))
    return pl.pallas_call(
        _gla_sample_kernel,
        grid=(DEC_BATCH,),
        in_specs=[spec(GLA_QK), spec(GLA_QK), spec(GLA_V), spec(GLA_V), spec(GLA_QK),
                  pl.BlockSpec((1, GLA_DV), lambda b: (0, 0)), st_spec],
        out_specs=[spec(GLA_V), st_spec],
        out_shape=[jax.ShapeDtypeStruct((DEC_BATCH, GLA_SPAD, GLA_V), F32),
                   jax.ShapeDtypeStruct((DEC_BATCH, GLA_HEADS, GLA_DK, GLA_DV), F32)],
        compiler_params=_cp(("arbitrary",)),
        name="gla_sample",
    )(q, k, v, br, g, gnorm.reshape(1, GLA_DV), s0)


def _attn_prompt_kernel(q0, q1, q2, kc0, kp0, vc0, vp0, kc1, kp1, vc1, vp1, kc2, kp2, vc2, vp2,
                        new3_ref, cache3_hbm, o_ref, kv3_hbm,
                        kb0, vb0, kb1, vb1, kb2, vb2, acc_ref, m_ref, l_ref, cbuf, rsem, wsem, nsem):
    sb = pl.program_id(0)
    step = sb * pl.num_programs(1) + pl.program_id(1)

    def pieces(s):
        return [(cache3_hbm, kv3_hbm, s, k * SHIFT_CHUNK, SHIFT_CHUNK) for k in range(SHIFT_SLOTS)]

    shifter = _CacheShiftInBackground(step, DEC_BATCH, pieces, [(new3_ref, kv3_hbm, A_WINDOWS[2])],
                                      cbuf, rsem, wsem, nsem)
    shifter.top_of_step()
    groups = ((q0, kc0, kp0, vc0, vp0, kb0, vb0, 1),
              (q1, kc1, kp1, vc1, vp1, kb1, vb1, 4),
              (q2, kc2, kp2, vc2, vp2, kb2, vb2, 16))
    scale = A_HEAD_DIM ** -0.5
    n_win = 2 * A_REACH
    row = lax.broadcasted_iota(jnp.int32, (A_REACH, n_win), 0)
    col = lax.broadcasted_iota(jnp.int32, (A_REACH, n_win), 1)
    in_reach = jnp.logical_and(col >= row, col <= row + A_REACH)
    in_prev = col < A_REACH
    ones = jnp.ones((n_win, LANES), BF16)

    for gi, (q_ref, kc, kp, vc, vp, kb, vb, d) in enumerate(groups):
        win = A_REACH * d
        by_residue = d == ATT_RESIDUE_MAJOR_D
        if by_residue:
            def stage(jj, carry, kp=kp, kc=kc, vp=vp, vc=vc, kb=kb, vb=vb, d=d):
                src = pl.ds(pl.multiple_of(jj * d, d), d)
                for buf, prev, cur in ((kb, kp, kc), (vb, vp, vc)):
                    buf[pl.ds(jj, d, stride=ATT_RESIDUE_PITCH), :] = prev[src, :]
                    buf[pl.ds(A_REACH + jj, d, stride=ATT_RESIDUE_PITCH), :] = cur[src, :]
                return carry
            lax.fori_loop(0, A_REACH, stage, 0, unroll=8)
        else:
            kb[0:win, :] = kp[...]
            kb[win:win + ATT_SB, :] = kc[...]
            vb[0:win, :] = vp[...]
            vb[win:win + ATT_SB, :] = vc[...]
        shift = {1: 0, 4: 2, 16: 4}[d]

        def tiles(it, carry, q_ref=q_ref, kb=kb, vb=vb, d=d, win=win, shift=shift, gi=gi, by_residue=by_residue):
            rows, scores = [], []
            for j in range(ATT_UNROLL):
                idx = it * ATT_UNROLL + j
                r = idx & (d - 1)
                nb = idx >> shift
                start = r + nb * win
                if d == 1:
                    start = pl.multiple_of(start, A_REACH)
                    qrows = pl.ds(start, A_REACH)
                    wrows = pl.ds(start, n_win)
                elif by_residue:
                    qrows = pl.ds(start, A_REACH, stride=d)
                    wrows = pl.ds(pl.multiple_of(r * ATT_RESIDUE_PITCH, 8), n_win)
                else:
                    qrows = pl.ds(start, A_REACH, stride=d)
                    wrows = pl.ds(start, n_win, stride=d)
                prev_bias = jnp.where(jnp.logical_or(sb > 0, nb > 0), 0.0, -jnp.inf)
                rows.append((qrows, wrows, prev_bias))
                scores.append(_dot_nt(q_ref[qrows, :].astype(BF16), kb[wrows, :].astype(BF16)))
            probs = []
            for s, (_, _, prev_bias) in zip(scores, rows):
                s = jnp.where(in_reach, s * scale, -jnp.inf) + jnp.where(in_prev, prev_bias, 0.0)
                m_t = jnp.max(s, axis=1, keepdims=True)
                probs.append((jnp.exp(s - m_t).astype(BF16), jnp.broadcast_to(m_t, (A_REACH, LANES))))
            sums = []
            for (p, _), (_, wrows, _) in zip(probs, rows):
                sums.append((_dot(p, ones), _dot(p, vb[wrows, :].astype(BF16))))
            for (l_t, num), (_, m_t), (qrows, _, _) in zip(sums, probs, rows):
                if gi == 0:
                    acc_ref[qrows, :] = num
                    m_ref[qrows, :] = m_t
                    l_ref[qrows, :] = l_t
                else:
                    m_o = m_ref[qrows, :]
                    m_n = jnp.maximum(m_o, m_t)
                    a = jnp.exp(m_o - m_n)
                    b = jnp.exp(m_t - m_n)
                    acc_ref[qrows, :] = a * acc_ref[qrows, :] + b * num
                    l_ref[qrows, :] = a * l_ref[qrows, :] + b * l_t
                    m_ref[qrows, :] = m_n
            return carry

        lax.fori_loop(0, ATT_SB // A_REACH // ATT_UNROLL, tiles, 0)
        if gi == 1:
            shifter.turn_reads_into_writes()

    o_ref[...] = (acc_ref[...] / l_ref[...]).astype(BF16)
    shifter.end_of_step()


def _attn_prompt(u, new3, cache3):
    nh = A_HEADS
    assert (SEQ // ATT_SB) * nh == DEC_BATCH and SHIFT_SLOTS * SHIFT_CHUNK == A_WINDOWS[2] - DEC_SEQ
    in_specs, scratch = [], []
    for g in range(3):
        in_specs.append(pl.BlockSpec((ATT_SB, LANES), lambda sb, h, g=g: (sb, g * nh + h)))
    for g, d in enumerate(A_DILATIONS):
        win = A_REACH * d
        per = ATT_SB // win
        for part in (3, 6):
            cb = part * nh + g * nh
            in_specs.append(pl.BlockSpec((ATT_SB, LANES), lambda sb, h, cb=cb: (sb, cb + h)))
            in_specs.append(pl.BlockSpec(
                (win, LANES), lambda sb, h, cb=cb, per=per: (jnp.maximum(sb * per - 1, 0), cb + h)))
        buf_rows = d * ATT_RESIDUE_PITCH if d == ATT_RESIDUE_MAJOR_D else win + ATT_SB
        scratch += [pltpu.VMEM((buf_rows, LANES), F32), pltpu.VMEM((buf_rows, LANES), F32)]
    scratch += [pltpu.VMEM((ATT_SB, LANES), F32)] * 3
    scratch += [pltpu.VMEM((SHIFT_SLOTS, SHIFT_CHUNK + 1, 2, A_HEADS, A_HEAD_DIM), F32),
                pltpu.SemaphoreType.DMA((SHIFT_SLOTS,)), pltpu.SemaphoreType.DMA((SHIFT_SLOTS,)),
                pltpu.SemaphoreType.DMA((1,))]
    in_specs += [pl.BlockSpec(new3.shape, lambda sb, h: (0, 0, 0, 0, 0)), pl.BlockSpec(memory_space=pl.ANY)]
    return pl.pallas_call(
        _attn_prompt_kernel,
        grid=(SEQ // ATT_SB, nh),
        in_specs=in_specs,
        out_specs=[pl.BlockSpec((ATT_SB, LANES), lambda sb, h: (sb, h)), pl.BlockSpec(memory_space=pl.ANY)],
        out_shape=[jax.ShapeDtypeStruct((SEQ, A_WIDTH), BF16), jax.ShapeDtypeStruct(cache3.shape, cache3.dtype)],
        scratch_shapes=scratch,
        compiler_params=_cp(("arbitrary", "arbitrary"), vmem=ATT_VMEM_LIMIT),
        name="attn_prompt",
    )(*([u] * 15), new3, cache3)


def _attn_sample_kernel(qkv_ref, c1_ref, c2_ref, c3_ref, o_ref):
    scale = A_HEAD_DIM ** -0.5
    pos = lax.broadcasted_iota(jnp.int32, (A_REACH, 1, 1), 0)

    def merge(state, m_t, l_t, num):
        if state is None:
            return m_t, l_t, num
        m_o, l_o, n_o = state
        m_n = jnp.maximum(m_o, m_t)
        a, b = jnp.exp(m_o - m_n), jnp.exp(m_t - m_n)
        return m_n, a * l_o + b * l_t, a * n_o + b * num

    for s in range(DEC_SEQ):
        state = None
        for g in range(3):
            q = qkv_ref[0, s, g]
            if g == 0:
                kt, vt = c1_ref[0, :, 0], c1_ref[0, :, 1]
                valid = pos >= s
                new_rows = range(s + 1)
            else:
                c_ref = c2_ref if g == 1 else c3_ref
                kt, vt = c_ref[0, :, 2 * s], c_ref[0, :, 2 * s + 1]
                valid = None
                new_rows = (s,)
            sc = jnp.sum(kt * q[None], axis=-1, keepdims=True) * scale
            if valid is not None:
                sc = jnp.where(valid, sc, -jnp.inf)
            s_new = [jnp.sum(qkv_ref[0, t, 3 + g] * q, axis=-1, keepdims=True) * scale for t in new_rows]
            m_t = jnp.max(sc, axis=0)
            for sn in s_new:
                m_t = jnp.maximum(m_t, sn)
            p = jnp.exp(sc - m_t[None])
            l_t = jnp.sum(p, axis=0)
            num = jnp.sum(p * vt, axis=0)
            for t, sn in zip(new_rows, s_new):
                pn = jnp.exp(sn - m_t)
                l_t = l_t + pn
                num = num + pn * qkv_ref[0, t, 6 + g]
            state = merge(state, m_t, l_t, num)
        o_ref[0, s] = state[2] / state[1]


def _attn_sample(qkv_s, c1, c2, c3):
    tile = (A_HEADS, A_HEAD_DIM)
    return pl.pallas_call(
        _attn_sample_kernel,
        grid=(DEC_BATCH,),
        in_specs=[
            pl.BlockSpec((1, DEC_SEQ, 9) + tile, lambda b: (b, 0, 0, 0, 0)),
            pl.BlockSpec((1, A_REACH, 2) + tile, lambda b: (b, 0, 0, 0, 0)),
            pl.BlockSpec((1, A_REACH, 2 * DEC_SEQ) + tile, lambda b: (b, 0, 0, 0, 0)),
            pl.BlockSpec((1, A_REACH, 2 * DEC_SEQ) + tile, lambda b: (b, 0, 0, 0, 0)),
        ],
        out_specs=pl.BlockSpec((1, DEC_SEQ) + tile, lambda b: (b, 0, 0, 0)),
        out_shape=jax.ShapeDtypeStruct((DEC_BATCH, DEC_SEQ) + tile, F32),
        compiler_params=_cp(("arbitrary",)),
        name="attn_sample",
    )(qkv_s, c1, c2, c3)


def _load_resident(step, pairs):
    @pl.when(step == 0)
    def _():
        for src, dst in pairs:
            pltpu.sync_copy(src, dst)


def _merge_ln1_kernel(oa_ref, og_ref, gt_ref, xn_ref, g_ref, b_ref, wa_hbm, wb_hbm, wo_hbm,
                      x1_ref, wa, wb, wo, mixed, resid):
    _load_resident(pl.program_id(0), ((wa_hbm, wa), (wb_hbm, wb), (wo_hbm, wo)))
    nj = D_MODEL // COL_TILE
    oa, og = oa_ref[...], og_ref[...]
    for t in range(nj):
        cs = slice(t * COL_TILE, (t + 1) * COL_TILE)
        gs = slice(D_MODEL + t * COL_TILE, D_MODEL + (t + 1) * COL_TILE)
        ya = _dot(oa, wa[:, cs])
        yb = _dot(og, wb[:, cs])
        ga, gb = gt_ref[:, cs].astype(F32), gt_ref[:, gs].astype(F32)
        mixed[:, cs] = (_sigmoid(ga) * ya + _sigmoid(gb) * yb).astype(BF16)
    mx = mixed[...]
    for t in range(nj):
        cs = slice(t * COL_TILE, (t + 1) * COL_TILE)
        resid[:, cs] = ALPHA * xn_ref[:, cs] + _dot(mx, wo[:, cs])
    x1_ref[...] = _layer_norm_rows(resid[...], g_ref[...], b_ref[...])


def _merge_ln1(o_a, o_g, gates, xn, g, b, wa, wb, wo):
    tm = MERGE_ROWS
    any_spec = pl.BlockSpec(memory_space=pl.ANY)
    return pl.pallas_call(
        _merge_ln1_kernel,
        grid=(M_ROWS // tm,),
        in_specs=[
            pl.BlockSpec((tm, A_WIDTH), lambda i: (i, 0)),
            pl.BlockSpec((tm, GLA_V), lambda i: (i, 0)),
            pl.BlockSpec((tm, 2 * D_MODEL), lambda i: (i, 0)),
            pl.BlockSpec((tm, D_MODEL), lambda i: (i, 0)),
            pl.BlockSpec((1, D_MODEL), lambda i: (0, 0)),
            pl.BlockSpec((1, D_MODEL), lambda i: (0, 0)),
            any_spec, any_spec, any_spec,
        ],
        out_specs=pl.BlockSpec((tm, D_MODEL), lambda i: (i, 0)),
        out_shape=jax.ShapeDtypeStruct((M_ROWS, D_MODEL), F32),
        scratch_shapes=[pltpu.VMEM((A_WIDTH, D_MODEL), BF16), pltpu.VMEM((GLA_V, D_MODEL), BF16),
                        pltpu.VMEM((D_MODEL, D_MODEL), BF16),
                        pltpu.VMEM((tm, D_MODEL), BF16), pltpu.VMEM((tm, D_MODEL), F32)],
        compiler_params=_cp(("arbitrary",)),
        name="merge_ln1",
    )(o_a, o_g, gates, xn, g.reshape(1, D_MODEL), b.reshape(1, D_MODEL), wa, wb, wo)


def _route_block(x, w, idx_ref, wt_ref, cnt_ref):
    tm = x.shape[0]
    xh, xm, _ = _split3(x)
    wh, wm, _ = _split3(w)
    logits = _dot(xh, wh) + _dot(xh, wm) + _dot(xm, wh)
    lane_i = lax.broadcasted_iota(jnp.int32, (tm, LANES), 1)
    lane = lane_i.astype(F32)
    lane_group = (lane_i >> 3).astype(F32)
    neg = -jnp.inf

    def top1(v):
        mx = jnp.max(v, axis=1, keepdims=True)
        ix = jnp.min(jnp.where(v == mx, lane, float(LANES)), axis=1, keepdims=True)
        return mx, ix

    gl = jnp.where(lane_i < MOE_GROUPS, logits[:, :LANES], neg)
    g_max, g_idx = top1(gl)
    g_p = 1.0 / jnp.sum(jnp.exp(gl - g_max), axis=1, keepdims=True)
    el = logits[:, LANES:]
    in_group = jnp.logical_and(lane_i < N_EXPERTS, lane_group == g_idx)
    e1v = jnp.where(in_group, el, neg)
    m1, i1 = top1(e1v)
    m2, i2 = top1(jnp.where(lane == i1, neg, e1v))
    t = jnp.exp(m2 - m1)
    w1 = g_p / (1.0 + t)
    w2 = g_p * t / (1.0 + t)
    wt_ref[...] = jnp.where(lane_i == 0, w1, jnp.where(lane_i == 1, w2, 0.0))

    @pl.when(pl.program_id(0) == 0)
    def _():
        cnt_ref[...] = jnp.zeros_like(cnt_ref)

    is1, is2 = lane == i1, lane == i2
    onehot = jnp.where(jnp.logical_or(is1, is2), 1.0, 0.0)
    r = lax.broadcasted_iota(jnp.int32, (tm, tm), 0)
    c = lax.broadcasted_iota(jnp.int32, (tm, tm), 1)
    before = _dot(jnp.where(r > c, 1.0, 0.0).astype(BF16), onehot.astype(BF16)) + cnt_ref[...]
    rank1 = jnp.sum(jnp.where(is1, before, 0.0), axis=1, keepdims=True)
    rank2 = jnp.sum(jnp.where(is2, before, 0.0), axis=1, keepdims=True)
    cnt_ref[...] += jnp.sum(onehot, axis=0, keepdims=True)
    idx_ref[...] = jnp.where(lane_i == 0, i1, jnp.where(lane_i == 1, i2, jnp.where(
        lane_i == 2, rank1, jnp.where(lane_i == 3, rank2, 0.0)))).astype(jnp.int32)


def _ple_router_kernel(x_ref, p_ref, wr_ref, wg_hbm, wp_hbm, o_ref, idx_ref, wt_ref, cnt_ref, wg, wp):
    _load_resident(pl.program_id(0), ((wg_hbm, wg), (wp_hbm, wp)))
    x = x_ref[...]
    xb = x.astype(BF16)
    p = p_ref[...]
    for t in range(D_MODEL // COL_TILE):
        cs = slice(t * COL_TILE, (t + 1) * COL_TILE)
        gate = _sigmoid(_dot(xb, wg[:, cs]))
        o_ref[:, cs] = ALPHA * x_ref[:, cs] + gate * _dot(p, wp[:, cs])
    _route_block(x, wr_ref[...], idx_ref, wt_ref, cnt_ref)


def _ple_router(x1, p, wpg, wple, w_r1, w_r2):
    wr = jnp.zeros((D_MODEL, 2 * LANES), F32)
    wr = wr.at[:, :MOE_GROUPS].set(w_r1).at[:, LANES:LANES + N_EXPERTS].set(w_r2)
    tm = ROW_TILE
    any_spec = pl.BlockSpec(memory_space=pl.ANY)
    return pl.pallas_call(
        _ple_router_kernel,
        grid=(M_ROWS // tm,),
        in_specs=[
            pl.BlockSpec((tm, D_MODEL), lambda i: (i, 0)),
            pl.BlockSpec((tm, PLE_DIM), lambda i: (i, 0)),
            pl.BlockSpec((D_MODEL, 2 * LANES), lambda i: (0, 0)),
            any_spec, any_spec,
        ],
        out_specs=[pl.BlockSpec((tm, D_MODEL), lambda i: (i, 0)),
                   pl.BlockSpec((tm, LANES), lambda i: (i, 0)),
                   pl.BlockSpec((tm, LANES), lambda i: (i, 0)),
                   pl.BlockSpec((1, LANES), lambda i: (0, 0))],
        out_shape=[jax.ShapeDtypeStruct((M_ROWS, D_MODEL), F32),
                   jax.ShapeDtypeStruct((M_ROWS, LANES), jnp.int32),
                   jax.ShapeDtypeStruct((M_ROWS, LANES), F32),
                   jax.ShapeDtypeStruct((1, LANES), F32)],
        scratch_shapes=[pltpu.VMEM((D_MODEL, D_MODEL), BF16), pltpu.VMEM((PLE_DIM, D_MODEL), BF16)],
        compiler_params=_cp(("arbitrary",)),
        name="ple_router",
    )(x1, p, wr, wpg, wple)


WEIGHT_DMA_PRIORITY = 1


def _moe_kernel(be_ref, nx_ref, nv_ref, tok_ref, x_hbm, wg_hbm, wu_hbm, wd_hbm, y_ref,
                sg, su, sd, wgb, wub, wdb, xbuf, wsem, xsem):
    i = pl.program_id(0)
    nv = nv_ref[0]
    e = be_ref[i]

    def weight_copies(ex):
        return (pltpu.make_async_copy(wg_hbm.at[ex], sg, wsem.at[0]),
                pltpu.make_async_copy(wu_hbm.at[ex], su, wsem.at[1]),
                pltpu.make_async_copy(wd_hbm.at[ex], sd, wsem.at[2]))

    def row_copy(blk, slot, r):
        src = tok_ref[blk * MOE_BLK + r]
        return pltpu.make_async_copy(x_hbm.at[pl.ds(src, 1)], xbuf.at[slot, pl.ds(r, 1)], xsem.at[slot])

    def for_rows(blk, slot, action):
        def body(r, carry):
            action(row_copy(blk, slot, r))
            return carry
        lax.fori_loop(0, MOE_BLK, body, 0, unroll=8)

    @pl.when(i == 0)
    def _():
        for cp in weight_copies(e):
            cp.start(priority=WEIGHT_DMA_PRIORITY)
        for_rows(0, 0, lambda cp: cp.start())

    @pl.when(i + 1 < nv)
    def _():
        for_rows(i + 1, (i + 1) & 1, lambda cp: cp.start())

    first_of_expert = jnp.logical_or(i == 0, e != be_ref[jnp.maximum(i - 1, 0)])

    @pl.when(jnp.logical_and(i < nv, first_of_expert))
    def _():
        for cp in weight_copies(e):
            cp.wait()
        for r in range(0, D_MODEL, MOE_BLK):
            wgb[r:r + MOE_BLK, :] = sg[r:r + MOE_BLK, :].astype(BF16)
            wub[r:r + MOE_BLK, :] = su[r:r + MOE_BLK, :].astype(BF16)
        for r in range(0, D_EXPERT, MOE_BLK):
            wdb[r:r + MOE_BLK, :] = sd[r:r + MOE_BLK, :].astype(BF16)

        @pl.when(nx_ref[i] >= 0)
        def _():
            for cp in weight_copies(nx_ref[i]):
                cp.start(priority=WEIGHT_DMA_PRIORITY)

    @pl.when(i < nv)
    def _():
        slot = i & 1
        for_rows(i, slot, lambda cp: cp.wait())
        x = xbuf[slot].astype(BF16)
        a = _dot(x, wgb[...])
        h = (a * _sigmoid(a) * _dot(x, wub[...])).astype(BF16)
        y_ref[...] = _dot(h, wdb[...])

    @pl.when(i >= nv)
    def _():
        y_ref[...] = jnp.zeros_like(y_ref)


def _moe_experts(x1, row_tok, blk_e, blk_next, n_valid, w_gate, w_up, w_down):
    any_spec = pl.BlockSpec(memory_space=pl.ANY)
    return pl.pallas_call(
        _moe_kernel,
        grid_spec=pltpu.PrefetchScalarGridSpec(
            num_scalar_prefetch=4,
            grid=(MOE_NBLK,),
            in_specs=[any_spec, any_spec, any_spec, any_spec],
            out_specs=pl.BlockSpec((MOE_BLK, D_MODEL), lambda i, be, nx, nv, tok: (i, 0)),
            scratch_shapes=[
                pltpu.VMEM((D_MODEL, D_EXPERT), F32), pltpu.VMEM((D_MODEL, D_EXPERT), F32),
                pltpu.VMEM((D_EXPERT, D_MODEL), F32),
                pltpu.VMEM((D_MODEL, D_EXPERT), BF16), pltpu.VMEM((D_MODEL, D_EXPERT), BF16),
                pltpu.VMEM((D_EXPERT, D_MODEL), BF16),
                pltpu.VMEM((2, MOE_BLK, D_MODEL), F32),
                pltpu.SemaphoreType.DMA((3,)), pltpu.SemaphoreType.DMA((2,)),
            ],
        ),
        out_shape=jax.ShapeDtypeStruct((MOE_ROWS, D_MODEL), F32),
        compiler_params=_cp(("arbitrary",)),
        name="moe_experts",
    )(blk_e, blk_next, n_valid, row_tok, x1, w_gate, w_up, w_down)


def _moe_plan(e_idx, rank, counts):
    ids = jnp.arange(N_EXPERTS, dtype=jnp.int32)
    pcounts = (counts + MOE_BLK - 1) // MOE_BLK * MOE_BLK
    pends = jnp.cumsum(pcounts)
    pstarts = pends - pcounts
    dest = jnp.sum(jnp.where(e_idx[..., None] == ids, pstarts, 0), axis=-1) + rank
    tok = jnp.broadcast_to(jnp.arange(M_ROWS, dtype=jnp.int32)[:, None], dest.shape)
    row_tok = jnp.zeros((MOE_ROWS,), jnp.int32).at[dest.reshape(-1)].set(tok.reshape(-1))
    n_valid = (pends[-1] // MOE_BLK).astype(jnp.int32)
    blk_start = jnp.minimum(jnp.arange(MOE_NBLK, dtype=jnp.int32) * MOE_BLK, pends[-1] - 1)
    blk_e = jnp.sum((pends[None, :] <= blk_start[:, None]).astype(jnp.int32), axis=1)
    blk_e = jnp.clip(blk_e, 0, N_EXPERTS - 1).astype(jnp.int32)
    later = jnp.where((counts[None, :] > 0) & (ids[None, :] > ids[:, None]), ids[None, :], N_EXPERTS)
    next_e = jnp.min(later, axis=1)
    next_e = jnp.where(next_e == N_EXPERTS, -1, next_e).astype(jnp.int32)
    blk_next = jnp.sum(jnp.where(blk_e[:, None] == ids, next_e, 0), axis=1).astype(jnp.int32)
    return dest, row_tok, blk_e, blk_next, n_valid.reshape(1)


def _final_row_copy(dest_ref, y_hbm, ybuf, sem, blk, slot, k, t):
    src = dest_ref[k * M_ROWS + blk * LN_ROWS + t]
    return pltpu.make_async_copy(y_hbm.at[pl.ds(src, 1)], ybuf.at[slot, pl.ds(k * LN_ROWS + t, 1)], sem.at[slot])


def _final_kernel(dest_ref, base_ref, wt_ref, g_ref, b_ref, y_hbm, op_ref, os_ref, ybuf, sem):
    i = pl.program_id(0)
    n_rows = 2 * LN_ROWS

    def for_rows(blk, slot, action):
        for k in range(2):
            for t in range(LN_ROWS):
                action(_final_row_copy(dest_ref, y_hbm, ybuf, sem, blk, slot, k, t))

    @pl.when(i == 0)
    def _():
        for k in range(2):
            def body(t, carry, k=k):
                _final_row_copy(dest_ref, y_hbm, ybuf, sem, 0, 0, k, t).start()
                return carry
            lax.fori_loop(0, LN_ROWS, body, 0, unroll=8)

    @pl.when(i + 1 < pl.num_programs(0))
    def _():
        for_rows(i + 1, (i + 1) & 1, lambda cp: cp.start())

    slot = i & 1
    for_rows(i, slot, lambda cp: cp.wait())

    wt = wt_ref[...]
    r = base_ref[...] + wt[:, 0:1] * ybuf[slot, 0:LN_ROWS, :] + wt[:, 1:2] * ybuf[slot, LN_ROWS:n_rows, :]
    y = _layer_norm_rows(r, g_ref[...], b_ref[...])

    @pl.when(i < SEQ // LN_ROWS)
    def _():
        op_ref[...] = y

    @pl.when(i >= SEQ // LN_ROWS)
    def _():
        os_ref[...] = y


def _final(dest_flat, base, y_rows, wt, g, b):
    npb = SEQ // LN_ROWS
    return pl.pallas_call(
        _final_kernel,
        grid_spec=pltpu.PrefetchScalarGridSpec(
            num_scalar_prefetch=1,
            grid=(npb + 1,),
            in_specs=[
                pl.BlockSpec((LN_ROWS, D_MODEL), lambda i, d: (i, 0)),
                pl.BlockSpec((LN_ROWS, LANES), lambda i, d: (i, 0)),
                pl.BlockSpec((1, D_MODEL), lambda i, d: (0, 0)),
                pl.BlockSpec((1, D_MODEL), lambda i, d: (0, 0)),
                pl.BlockSpec(memory_space=pl.ANY),
            ],
            out_specs=[pl.BlockSpec((LN_ROWS, D_MODEL), lambda i, d: (jnp.minimum(i, npb - 1), 0)),
                       pl.BlockSpec((N_SAMPLE, D_MODEL), lambda i, d: (0, 0))],
            scratch_shapes=[pltpu.VMEM((2, 2 * LN_ROWS, D_MODEL), F32), pltpu.SemaphoreType.DMA((2,))],
        ),
        out_shape=[jax.ShapeDtypeStruct((SEQ, D_MODEL), F32),
                   jax.ShapeDtypeStruct((N_SAMPLE, D_MODEL), F32)],
        compiler_params=_cp(("arbitrary",)),
        name="combine_ln2",
    )(dest_flat, base, wt, g.reshape(1, D_MODEL), b.reshape(1, D_MODEL), y_rows)


def kernel(x_prompt, x_sample, p_prompt, p_sample, cache_kv_a1, cache_kv_a2, cache_kv_a3, state_gla,
           ln_emb_g, ln_emb_b, ln1_g, ln1_b, ln2_g, ln2_b, w_in, w_gk2, b_gk, gla_norm_g, w_br_a, w_br_b,
           w_out, w_router_group, w_router_expert, w_gate, w_up, w_down, w_ple_gate, w_ple):
    xp = x_prompt.reshape(SEQ, D_MODEL)
    xs = x_sample.reshape(N_SAMPLE, D_MODEL)
    xn, xb = _ln_emb(xp, xs, ln_emb_g, ln_emb_b)

    w_t = jnp.transpose(w_in[0])
    rope = _rope_tables()
    u = _project(xb, w_t, 0, A_QKV_COLS, 2 * 3 * A_WIDTH // COL_TILE, rope, F32)
    u_gla = _project(xb, w_t, A_QKV_COLS, MAIN_COLS - A_QKV_COLS, 0, rope, BF16)
    gates = _project(xb, w_t, MAIN_COLS + GLA_LR, 2 * D_MODEL, 0, rope, BF16)
    gk = _gla_gate(xb, w_t[MAIN_COLS:MAIN_COLS + GLA_LR], w_gk2[0], b_gk[0])

    qkv_s = u[SEQ:].reshape(DEC_BATCH, DEC_SEQ, 9, A_HEADS, A_HEAD_DIM)
    kv_new = [jnp.stack([qkv_s[:, :, 3 + g], qkv_s[:, :, 6 + g]], axis=2) for g in range(3)]
    o_a, kv_s3 = _attn_prompt(u, kv_new[2], cache_kv_a3)
    c1 = cache_kv_a1.reshape(DEC_BATCH, A_REACH, 2, A_HEADS, A_HEAD_DIM)
    c2 = cache_kv_a2.reshape(DEC_BATCH, A_REACH, 4 * 2, A_HEADS, A_HEAD_DIM)
    c3 = cache_kv_a3.reshape(DEC_BATCH, A_REACH, 16 * 2, A_HEADS, A_HEAD_DIM)
    o_as = _attn_sample(qkv_s, c1, c2, c3)
    o_a = jnp.concatenate([o_a, o_as.reshape(N_SAMPLE, A_WIDTH).astype(BF16)], axis=0)

    o_g, st_t, kv_s1, kv_s2 = _gla_prompt(u_gla, gk, gla_norm_g[0], kv_new[:2], (cache_kv_a1, cache_kv_a2))

    def spad(a):
        a = a.astype(F32).reshape(DEC_BATCH, DEC_SEQ, a.shape[-1])
        return jnp.pad(a, ((0, 0), (0, GLA_SPAD - DEC_SEQ), (0, 0)))

    ug_s = u_gla[SEQ:]
    o_gs, st_s = _gla_sample(
        spad(ug_s[:, :GLA_QK]), spad(ug_s[:, GLA_QK:2 * GLA_QK]),
        spad(ug_s[:, 2 * GLA_QK:2 * GLA_QK + GLA_V]), spad(ug_s[:, 2 * GLA_QK + GLA_V:]),
        spad(gk[SEQ:]), gla_norm_g[0], state_gla[0])
    o_gs = o_gs[:, :DEC_SEQ].reshape(N_SAMPLE, GLA_V).astype(BF16)
    o_g = jnp.concatenate([o_g, o_gs], axis=0)

    x1 = _merge_ln1(o_a, o_g, gates, xn, ln1_g[0], ln1_b[0],
                    w_br_a[0].astype(BF16), w_br_b[0].astype(BF16), w_out[0].astype(BF16))

    p = jnp.concatenate([p_prompt[0].reshape(SEQ, PLE_DIM), p_sample[0].reshape(N_SAMPLE, PLE_DIM)], axis=0)
    base, ridx, rwt, rcnt = _ple_router(x1, p.astype(BF16), w_ple_gate[0].astype(BF16), w_ple[0].astype(BF16),
                                        w_router_group[0], w_router_expert[0])
    dest, row_tok, blk_e, blk_next, n_valid = _moe_plan(
        ridx[:, :2], ridx[:, 2:4], rcnt[0, :N_EXPERTS].astype(jnp.int32))
    y_rows =_moe_experts(x1, row_tok, blk_e, blk_next, n_valid, w_gate[0], w_up[0], w_down[0])
    y_p, y_s = _final(dest.T.reshape(-1), base, y_rows, rwt, ln2_g[0], ln2_b[0])

    def kv_prompt(g):
        w = A_WINDOWS[g]
        k = u[SEQ - w:SEQ, (3 + g) * A_WIDTH:(4 + g) * A_WIDTH].reshape(w, A_HEADS, A_HEAD_DIM)
        v = u[SEQ - w:SEQ, (6 + g) * A_WIDTH:(7 + g) * A_WIDTH].reshape(w, A_HEADS, A_HEAD_DIM)
        return jnp.stack([k, v], axis=1)[None, None]


    gla_state_prompt = st_t[None, None]
    return (y_p.reshape(1, SEQ, D_MODEL), y_s.reshape(DEC_BATCH, DEC_SEQ, D_MODEL),
            kv_prompt(0), kv_prompt(1), kv_prompt(2), gla_state_prompt,
            kv_s1, kv_s2, kv_s3, st_s[None])
```

```python
import functools

import jax
import jax.numpy as jnp
import numpy as np
from jax import lax
from jax.experimental import pallas as pl
from jax.experimental.pallas import tpu as pltpu

F32 = jnp.float32
BF16 = jnp.bfloat16

D_MODEL = 2048
SEQ = 8192
DEC_BATCH = 32
DEC_SEQ = 4
PAST_LEN = 16384
N_SAMPLE = DEC_BATCH * DEC_SEQ
M_ROWS = SEQ + N_SAMPLE
A_WINDOWS = (128, 512, 2048)
A_DILATIONS = (1, 4, 16)
A_REACH = 128
A_HEADS = 8
A_HEAD_DIM = 128
A_WIDTH = A_HEADS * A_HEAD_DIM
ROT_DIM = 32
ROPE_THETA = 500000.0
A_QKV_COLS = 9 * A_WIDTH
GLA_HEADS = 4
GLA_DK = 256
GLA_DV = 512
GLA_LR = 16
GLA_TAU = 16.0
GLA_CHUNK = 64
GLA_QK = GLA_HEADS * GLA_DK
GLA_V = GLA_HEADS * GLA_DV
MAIN_COLS = A_QKV_COLS + 2 * GLA_QK + 2 * GLA_V
MOE_GROUPS = 4
MOE_PER_GROUP = 8
N_EXPERTS = 32
D_EXPERT = 1024
PLE_DIM = 256
LN_EPS = 1e-5
ALPHA = 2.0 ** 0.25

LANES = 128
VMEM_LIMIT = 56 * 1024 * 1024
ROW_TILE_PROJ = 2080
ROW_TILE = 832
MERGE_ROWS = 416
COL_TILE = 512
LN_ROWS = 128
MOE_BLK = 256
MOE_NBLK = (2 * M_ROWS) // MOE_BLK + N_EXPERTS
MOE_ROWS = MOE_NBLK * MOE_BLK
ATT_SB = 2048
GLA_ROWS = 512
ATT_VMEM_LIMIT = 60 * 1024 * 1024
SHIFT_SLOTS = 4
SHIFT_CHUNK = 511
SMALL_SHIFT_ROWS = 2
ATT_UNROLL = 4
ATT_RESIDUE_MAJOR_D = 16
ATT_RESIDUE_PITCH = 264

def _cp(sem, vmem=VMEM_LIMIT):
    return pltpu.CompilerParams(dimension_semantics=sem, vmem_limit_bytes=vmem)


def _sigmoid(x):
    return 1.0 / (1.0 + jnp.exp(-x))


def _dot(a, b):
    return jnp.dot(a, b, preferred_element_type=F32)


def _dot_nt(a, b):
    return lax.dot_general(a, b, (((1,), (1,)), ((), ())), preferred_element_type=F32)


def _split3(x):
    h = x.astype(BF16)
    r = x - h.astype(F32)
    m = r.astype(BF16)
    l = (r - m.astype(F32)).astype(BF16)
    return h, m, l


def _layer_norm_rows(x, g, b):
    mu = jnp.mean(x, axis=-1, keepdims=True)
    xc = x - mu
    var = jnp.mean(xc * xc, axis=-1, keepdims=True)
    return xc * lax.rsqrt(var + LN_EPS) * g + b


LN_GROUP = 4


def _ln_emb_kernel(xp_ref, xs_ref, g_ref, b_ref, of_ref, ob_ref):
    is_prompt = pl.program_id(0) < SEQ // (LN_ROWS * LN_GROUP)
    x = jnp.where(is_prompt, xp_ref[...], xs_ref[...][None])
    y = _layer_norm_rows(x, g_ref[...], b_ref[...])
    of_ref[...] = y
    ob_ref[...] = y.astype(BF16)


def _ln_emb(xp, xs, g, b):
    n_groups = M_ROWS // LN_ROWS
    npb = SEQ // (LN_ROWS * LN_GROUP)
    blk = (LN_GROUP, LN_ROWS, D_MODEL)
    xn, xb = pl.pallas_call(
        _ln_emb_kernel,
        grid=(npb + 1,),
        in_specs=[
            pl.BlockSpec(blk, lambda i: (jnp.minimum(i, npb - 1), 0, 0)),
            pl.BlockSpec((N_SAMPLE, D_MODEL), lambda i: (0, 0)),
            pl.BlockSpec((1, D_MODEL), lambda i: (0, 0)),
            pl.BlockSpec((1, D_MODEL), lambda i: (0, 0)),
        ],
        out_specs=[pl.BlockSpec(blk, lambda i: (i, 0, 0)), pl.BlockSpec(blk, lambda i: (i, 0, 0))],
        out_shape=[jax.ShapeDtypeStruct((n_groups, LN_ROWS, D_MODEL), F32),
                   jax.ShapeDtypeStruct((n_groups, LN_ROWS, D_MODEL), BF16)],
        compiler_params=_cp(("arbitrary",)),
        name="ln_emb",
    )(xp.reshape(SEQ // LN_ROWS, LN_ROWS, D_MODEL), xs, g.reshape(1, D_MODEL), b.reshape(1, D_MODEL))
    return xn.reshape(M_ROWS, D_MODEL), xb.reshape(M_ROWS, D_MODEL)


def _proj_kernel(x_ref, w_ref, c_ref, s1_ref, s2_ref, o_ref, *, n_rope_blocks):
    j = pl.program_id(1)
    acc = _dot_nt(x_ref[...], w_ref[...].astype(BF16))

    @pl.when(j >= n_rope_blocks)
    def _():
        o_ref[...] = acc.astype(o_ref.dtype)

    @pl.when(j < n_rope_blocks)
    def _():
        c, s1, s2 = c_ref[...], s1_ref[...], s2_ref[...]
        for t in range(COL_TILE // LANES):
            a = acc[:, t * LANES:(t + 1) * LANES]
            o_ref[:, t * LANES:(t + 1) * LANES] = (
                a * c + pltpu.roll(a, LANES - ROT_DIM // 2, 1) * s1 + pltpu.roll(a, ROT_DIM // 2, 1) * s2
            ).astype(o_ref.dtype)


def _project(xb, w_t, row0, n_cols, n_rope_blocks, rope, out_dtype):
    c, s1, s2 = rope
    tm, tn = ROW_TILE_PROJ, COL_TILE
    return pl.pallas_call(
        functools.partial(_proj_kernel, n_rope_blocks=n_rope_blocks),
        grid=(M_ROWS // tm, n_cols // tn),
        in_specs=[
            pl.BlockSpec((tm, D_MODEL), lambda i, j: (i, 0)),
            pl.BlockSpec((pl.Element(tn), pl.Element(D_MODEL)),
                         lambda i, j: (pl.multiple_of(row0 + j * tn, 8), 0)),
            pl.BlockSpec((tm, LANES), lambda i, j: (i, 0)),
            pl.BlockSpec((tm, LANES), lambda i, j: (i, 0)),
            pl.BlockSpec((tm, LANES), lambda i, j: (i, 0)),
        ],
        out_specs=pl.BlockSpec((tm, tn), lambda i, j: (i, j)),
        out_shape=jax.ShapeDtypeStruct((M_ROWS, n_cols), out_dtype),
        compiler_params=_cp(("arbitrary", "arbitrary")),
        name="in_proj",
    )(xb, w_t, c, s1, s2)


def _rope_tables():
    half = ROT_DIM // 2
    inv = np.power(np.float32(ROPE_THETA), -np.arange(half, dtype=np.float32) * np.float32(2.0 / ROT_DIM))
    pos = np.concatenate([np.arange(SEQ), np.tile(PAST_LEN + np.arange(DEC_SEQ), DEC_BATCH)]).astype(np.float32)
    ang = (pos[:, None] * inv[None, :].astype(np.float32)).astype(np.float32)
    cos, sin = np.cos(ang).astype(np.float32), np.sin(ang).astype(np.float32)
    ones = np.ones((M_ROWS, LANES - ROT_DIM), np.float32)
    zeros = np.zeros((M_ROWS, LANES - ROT_DIM), np.float32)
    zh = np.zeros((M_ROWS, half), np.float32)
    c = np.concatenate([cos, cos, ones], axis=1)
    s1 = np.concatenate([-sin, zh, zeros], axis=1)
    s2 = np.concatenate([zh, sin, zeros], axis=1)
    return jnp.asarray(c), jnp.asarray(s1), jnp.asarray(s2)


def _gk_kernel(x_ref, wlr_ref, wgk_ref, b_ref, o_ref):
    blr = _dot_nt(x_ref[...], wlr_ref[...])
    z = _dot(blr.astype(BF16), wgk_ref[...]) + b_ref[...]
    o_ref[...] = (jnp.minimum(z, 0.0) - jnp.log1p(jnp.exp(-jnp.abs(z)))) * (1.0 / GLA_TAU)


def _gla_gate(xb, w_lr_t, w_gk2, b_gk):
    wlr = jnp.zeros((LANES, D_MODEL), BF16).at[:GLA_LR].set(w_lr_t.astype(BF16))
    wgk = jnp.zeros((LANES, GLA_QK), BF16).at[:GLA_LR].set(w_gk2.astype(BF16))
    tm = ROW_TILE
    return pl.pallas_call(
        _gk_kernel,
        grid=(M_ROWS // tm,),
        in_specs=[
            pl.BlockSpec((tm, D_MODEL), lambda i: (i, 0)),
            pl.BlockSpec((LANES, D_MODEL), lambda i: (0, 0)),
            pl.BlockSpec((LANES, GLA_QK), lambda i: (0, 0)),
            pl.BlockSpec((1, GLA_QK), lambda i: (0, 0)),
        ],
        out_specs=pl.BlockSpec((tm, GLA_QK), lambda i: (i, 0)),
        out_shape=jax.ShapeDtypeStruct((M_ROWS, GLA_QK), F32),
        compiler_params=_cp(("arbitrary",)),
        name="gla_gate",
    )(xb, wlr, wgk, b_gk.reshape(1, GLA_QK))


class _CacheShiftInBackground:
    def __init__(self, step, n_steps, pieces, news, buf, rsem, wsem, nsem, n_active=None):
        self.step, self.n_steps, self.pieces, self.news = step, n_steps, pieces, news
        self.buf, self.rsem, self.wsem, self.nsem = buf, rsem, wsem, nsem
        self.n_active = n_steps if n_active is None else n_active

    def _copies(self, step):
        out = []
        for slot, (cache, dst, b, pos0, n) in enumerate(self.pieces(step)):
            stage = self.buf.at[slot, pl.ds(0, n)]
            rd = pltpu.make_async_copy(cache.at[0, b, pl.ds(DEC_SEQ + pos0, n)], stage, self.rsem.at[slot])
            wr = pltpu.make_async_copy(stage, dst.at[0, b, pl.ds(pos0, n)], self.wsem.at[slot])
            out.append((rd, wr))
        return out

    def _new_rows(self):
        return [pltpu.make_async_copy(new, dst.at[0, :, pl.ds(w - DEC_SEQ, DEC_SEQ)], self.nsem.at[i])
                for i, (new, dst, w) in enumerate(self.news)]

    def top_of_step(self):
        @pl.when(self.step == 0)
        def _():
            for cp in self._new_rows():
                cp.start()

        @pl.when(jnp.logical_and(self.step > 0, self.step <= self.n_active))
        def _():
            for _, wr in self._copies(self.step - 1):
                wr.wait()

        @pl.when(self.step < self.n_active)
        def _():
            for rd, _ in self._copies(self.step):
                rd.start()

    def turn_reads_into_writes(self):
        @pl.when(self.step < self.n_active)
        def _():
            for rd, wr in self._copies(self.step):
                rd.wait()
                wr.start()

    def end_of_step(self):
        @pl.when(self.step == self.n_steps - 1)
        def _():
            if self.n_active == self.n_steps:
                for _, wr in self._copies(self.step):
                    wr.wait()
            for cp in self._new_rows():
                cp.wait()


def _gla_chunks(heads, gn, causal, tri_b, eye):
    n = heads[0][3].shape[0]
    rid = lax.broadcasted_iota(jnp.int32, (n + GLA_TPAD, GLA_DK), 0)
    bs = []
    for _, _, _, g, _, _ in heads:
        g1, g2, g3 = _split3(g)
        bs.append(_dot(tri_b, g1) + _dot(tri_b, g2) + _dot(tri_b, g3))
    ops = []
    for (q, k, v, _, _, _), b in zip(heads, bs):
        q, k = q.astype(F32), k.astype(F32)
        b_last = b[n - 1:n, :]
        q_dec = (q * ((GLA_DK ** -0.5) * jnp.exp(b))).astype(BF16)
        k_inv = (k * jnp.exp(-b)).astype(BF16)
        k_dec = k * jnp.exp(b_last - b)
        d1, d2, d3 = _split3(jnp.exp(b_last))
        rows = jnp.concatenate([k_dec, jnp.zeros((GLA_TPAD, GLA_DK), F32)], axis=0)
        rows = jnp.where(rid == n, d1.astype(F32), rows)
        rows = jnp.where(rid == n + 1, d2.astype(F32), rows)
        rows = jnp.where(rid == n + 2, d3.astype(F32), rows)
        ops.append((q_dec, k_inv, rows.astype(BF16), v.astype(BF16)))
    mids = []
    for q_dec, k_inv, rows, _ in ops:
        mids.append((jnp.where(causal, _dot_nt(q_dec, k_inv), 0.0).astype(BF16), _dot_nt(eye, rows)))
    outs = []
    for (q_dec, _, _, vb), (att, cols), (_, _, _, _, br, s0) in zip(ops, mids, heads):
        o = _dot(att, vb) + _dot(q_dec, s0.astype(BF16))
        dcol = cols[:, n:n + 1] + cols[:, n + 1:n + 2] + cols[:, n + 2:n + 3]
        s1 = s0 * dcol + _dot(cols[:, :n].astype(BF16), vb)
        br = br.astype(F32)
        ms = jnp.mean(o * o, axis=-1, keepdims=True)
        outs.append((o * lax.rsqrt(ms + LN_EPS) * gn * (br * _sigmoid(br)), s1))
    return outs


GLA_TPAD = 16


def _gla_consts(n):
    r = lax.broadcasted_iota(jnp.int32, (n, n), 0)
    c = lax.broadcasted_iota(jnp.int32, (n, n), 1)
    causal = r >= c
    er = lax.broadcasted_iota(jnp.int32, (GLA_DK, GLA_DK), 0)
    ec = lax.broadcasted_iota(jnp.int32, (GLA_DK, GLA_DK), 1)
    return causal, jnp.where(causal, 1.0, 0.0).astype(BF16), jnp.where(er == ec, 1.0, 0.0).astype(BF16)


def _whole_row_shifter(step, n_steps, window, new_ref, cache_hbm, out_hbm, cbuf, rsem, wsem, nsem):
    def pieces(s):
        return [(cache_hbm, out_hbm, s * SMALL_SHIFT_ROWS + j, 0, window - DEC_SEQ) for j in range(SMALL_SHIFT_ROWS)]
    return _CacheShiftInBackground(step, n_steps, pieces, [(new_ref, out_hbm, window)], cbuf, rsem, wsem, nsem,
                                   n_active=DEC_BATCH // SMALL_SHIFT_ROWS)


def _whole_row_shift_scratch(window):
    return [pltpu.VMEM((SMALL_SHIFT_ROWS, window - DEC_SEQ, 2, A_HEADS, A_HEAD_DIM), F32),
            pltpu.SemaphoreType.DMA((SMALL_SHIFT_ROWS,)), pltpu.SemaphoreType.DMA((SMALL_SHIFT_ROWS,)),
            pltpu.SemaphoreType.DMA((1,))]


def _gla_prompt_kernel(q_ref, k_ref, v_ref, br_ref, g_ref, gn_ref, n1_ref, c1_hbm,
                       o_ref, st_ref, kv1_hbm, cbuf, rsem, wsem, nsem):
    step = pl.program_id(0)
    shifter = _whole_row_shifter(step, SEQ // GLA_ROWS, A_WINDOWS[0], n1_ref, c1_hbm, kv1_hbm,
                                 cbuf, rsem, wsem, nsem)
    shifter.top_of_step()

    @pl.when(step == 0)
    def _():
        st_ref[...] = jnp.zeros_like(st_ref)

    causal, tri_b, eye = _gla_consts(GLA_CHUNK)
    gn = gn_ref[...]

    def body(cc, carry):
        rows = pl.ds(pl.multiple_of(cc * GLA_CHUNK, GLA_CHUNK), GLA_CHUNK)
        heads = []
        for h in range(GLA_HEADS):
            ks = slice(h * GLA_DK, (h + 1) * GLA_DK)
            vs = slice(h * GLA_DV, (h + 1) * GLA_DV)
            heads.append((q_ref[rows, ks], k_ref[rows, ks], v_ref[rows, vs], g_ref[rows, ks],
                          br_ref[rows, vs], st_ref[h]))
        for h, (o, s1) in enumerate(_gla_chunks(heads, gn, causal, tri_b, eye)):
            st_ref[h] = s1
            o_ref[rows, h * GLA_DV:(h + 1) * GLA_DV] = o.astype(BF16)
        return carry

    half = GLA_ROWS // GLA_CHUNK // 2
    lax.fori_loop(0, half, body, 0)
    shifter.turn_reads_into_writes()
    lax.fori_loop(half, 2 * half, body, 0)
    shifter.end_of_step()


def _gla_prompt(u, gk, gnorm, new1, cache1):
    n_steps = SEQ // GLA_ROWS
    assert n_steps >= DEC_BATCH // SMALL_SHIFT_ROWS
    any_spec = pl.BlockSpec(memory_space=pl.ANY)
    return pl.pallas_call(
        _gla_prompt_kernel,
        grid=(n_steps,),
        in_specs=[
            pl.BlockSpec((GLA_ROWS, GLA_QK), lambda c: (c, 0)),
            pl.BlockSpec((GLA_ROWS, GLA_QK), lambda c: (c, 1)),
            pl.BlockSpec((GLA_ROWS, GLA_V), lambda c: (c, 1)),
            pl.BlockSpec((GLA_ROWS, GLA_V), lambda c: (c, 2)),
            pl.BlockSpec((GLA_ROWS, GLA_QK), lambda c: (c, 0)),
            pl.BlockSpec((1, GLA_DV), lambda c: (0, 0)),
            pl.BlockSpec(new1.shape, lambda c: (0, 0, 0, 0, 0)),
            any_spec,
        ],
        out_specs=[
            pl.BlockSpec((GLA_ROWS, GLA_V), lambda c: (c, 0)),
            pl.BlockSpec((GLA_HEADS, GLA_DK, GLA_DV), lambda c: (0, 0, 0)),
            any_spec,
        ],
        out_shape=[jax.ShapeDtypeStruct((SEQ, GLA_V), BF16),
                   jax.ShapeDtypeStruct((GLA_HEADS, GLA_DK, GLA_DV), F32),
                   jax.ShapeDtypeStruct(cache1.shape, cache1.dtype)],
        scratch_shapes=_whole_row_shift_scratch(A_WINDOWS[0]),
        compiler_params=_cp(("arbitrary",)),
        name="gla_prompt",
    )(u, u, u, u, gk, gnorm.reshape(1, GLA_DV), new1, cache1)


GLA_SPAD = 16


def _gla_sample_kernel(q_ref, k_ref, v_ref, br_ref, g_ref, gn_ref, s0_ref, o_ref, s1_ref):
    causal, tri_b, eye = _gla_consts(GLA_SPAD)
    gn = gn_ref[...]
    heads = []
    for h in range(GLA_HEADS):
        ks = slice(h * GLA_DK, (h + 1) * GLA_DK)
        vs = slice(h * GLA_DV, (h + 1) * GLA_DV)
        heads.append((q_ref[0, :, ks], k_ref[0, :, ks], v_ref[0, :, vs], g_ref[0, :, ks],
                      br_ref[0, :, vs], s0_ref[0, h]))
    for h, (o, s1) in enumerate(_gla_chunks(heads, gn, causal, tri_b, eye)):
        s1_ref[0, h] = s1
        o_ref[0, :, h * GLA_DV:(h + 1) * GLA_DV] = o


def _gla_sample(q, k, v, br, g, gnorm, s0):
    def spec(c):
        return pl.BlockSpec((1, GLA_SPAD, c), lambda b: (b, 0, 0))
    st_spec = pl.BlockSpec((1, GLA_HEADS, GLA_DK, GLA_DV), lambda b: (b, 0, 0, 0))
    return pl.pallas_call(
        _gla_sample_kernel,
        grid=(DEC_BATCH,),
        in_specs=[spec(GLA_QK), spec(GLA_QK), spec(GLA_V), spec(GLA_V), spec(GLA_QK),
                  pl.BlockSpec((1, GLA_DV), lambda b: (0, 0)), st_spec],
        out_specs=[spec(GLA_V), st_spec],
        out_shape=[jax.ShapeDtypeStruct((DEC_BATCH, GLA_SPAD, GLA_V), F32),
                   jax.ShapeDtypeStruct((DEC_BATCH, GLA_HEADS, GLA_DK, GLA_DV), F32)],
        compiler_params=_cp(("arbitrary",)),
        name="gla_sample",
    )(q, k, v, br, g, gnorm.reshape(1, GLA_DV), s0)


def _attn_prompt_kernel(q0, q1, q2, kc0, vc0, kc1, vc1, kc2, vc2,
                        new3_ref, cache3_hbm, o_ref, kv3_hbm,
                        kb0, vb0, kb1, vb1, kb2, vb2, acc_ref, m_ref, l_ref, cbuf, rsem, wsem, nsem):
    sb = pl.program_id(1)
    step = pl.program_id(0) * pl.num_programs(1) + sb

    def pieces(s):
        return [(cache3_hbm, kv3_hbm, s, k * SHIFT_CHUNK, SHIFT_CHUNK) for k in range(SHIFT_SLOTS)]

    shifter = _CacheShiftInBackground(step, DEC_BATCH, pieces, [(new3_ref, kv3_hbm, A_WINDOWS[2])],
                                      cbuf, rsem, wsem, nsem)
    shifter.top_of_step()
    groups = ((q0, kc0, vc0, kb0, vb0, 1),
              (q1, kc1, vc1, kb1, vb1, 4),
              (q2, kc2, vc2, kb2, vb2, 16))

    @pl.when(step == 0)
    def _():
        for _, _, _, kb, vb, _ in groups:
            kb[...] = jnp.zeros_like(kb)
            vb[...] = jnp.zeros_like(vb)
    scale = A_HEAD_DIM ** -0.5
    n_win = 2 * A_REACH
    row = lax.broadcasted_iota(jnp.int32, (A_REACH, n_win), 0)
    col = lax.broadcasted_iota(jnp.int32, (A_REACH, n_win), 1)
    in_reach = jnp.logical_and(col >= row, col <= row + A_REACH)
    in_prev = col < A_REACH
    ones = jnp.ones((n_win, LANES), BF16)

    for gi, (q_ref, kc, vc, kb, vb, d) in enumerate(groups):
        win = A_REACH * d
        by_residue = d == ATT_RESIDUE_MAJOR_D
        if by_residue:
            def stage(jj, carry, kc=kc, vc=vc, kb=kb, vb=vb, d=d):
                src = pl.ds(pl.multiple_of(jj * d, d), d)
                for buf, cur in ((kb, kc), (vb, vc)):
                    buf[pl.ds(A_REACH + jj, d, stride=ATT_RESIDUE_PITCH), :] = cur[src, :]
                return carry
            lax.fori_loop(0, A_REACH, stage, 0, unroll=8)
        else:
            kb[win:win + ATT_SB, :] = kc[...]
            vb[win:win + ATT_SB, :] = vc[...]
        shift = {1: 0, 4: 2, 16: 4}[d]

        def tiles(it, carry, q_ref=q_ref, kb=kb, vb=vb, d=d, win=win, shift=shift, gi=gi, by_residue=by_residue):
            rows, scores = [], []
            for j in range(ATT_UNROLL):
                idx = it * ATT_UNROLL + j
                r = idx & (d - 1)
                nb = idx >> shift
                start = r + nb * win
                if d == 1:
                    start = pl.multiple_of(start, A_REACH)
                    qrows = pl.ds(start, A_REACH)
                    wrows = pl.ds(start, n_win)
                elif by_residue:
                    qrows = pl.ds(start, A_REACH, stride=d)
                    wrows = pl.ds(pl.multiple_of(r * ATT_RESIDUE_PITCH, 8), n_win)
                else:
                    qrows = pl.ds(start, A_REACH, stride=d)
                    wrows = pl.ds(start, n_win, stride=d)
                prev_bias = jnp.where(jnp.logical_or(sb > 0, nb > 0), 0.0, -jnp.inf)
                rows.append((qrows, wrows, prev_bias))
                scores.append(_dot_nt(q_ref[qrows, :].astype(BF16), kb[wrows, :].astype(BF16)))
            probs = []
            for s, (_, _, prev_bias) in zip(scores, rows):
                s = jnp.where(in_reach, s * scale, -jnp.inf) + jnp.where(in_prev, prev_bias, 0.0)
                m_t = jnp.max(s, axis=1, keepdims=True)
                probs.append((jnp.exp(s - m_t).astype(BF16), jnp.broadcast_to(m_t, (A_REACH, LANES))))
            sums = []
            for (p, _), (_, wrows, _) in zip(probs, rows):
                sums.append((_dot(p, ones), _dot(p, vb[wrows, :].astype(BF16))))
            for (l_t, num), (_, m_t), (qrows, _, _) in zip(sums, probs, rows):
                if gi == 0:
                    acc_ref[qrows, :] = num
                    m_ref[qrows, :] = m_t
                    l_ref[qrows, :] = l_t
                else:
                    m_o = m_ref[qrows, :]
                    m_n = jnp.maximum(m_o, m_t)
                    a = jnp.exp(m_o - m_n)
                    b = jnp.exp(m_t - m_n)
                    acc_ref[qrows, :] = a * acc_ref[qrows, :] + b * num
                    l_ref[qrows, :] = a * l_ref[qrows, :] + b * l_t
                    m_ref[qrows, :] = m_n
            return carry

        lax.fori_loop(0, ATT_SB // A_REACH // ATT_UNROLL, tiles, 0)
        for buf in (kb, vb):
            if by_residue:
                for r in range(d):
                    base = r * ATT_RESIDUE_PITCH
                    buf[base:base + A_REACH, :] = buf[base + A_REACH:base + 2 * A_REACH, :]
            else:
                buf[0:win, :] = buf[ATT_SB:ATT_SB + win, :]
        if gi == 1:
            shifter.turn_reads_into_writes()

    o_ref[...] = (acc_ref[...] / l_ref[...]).astype(BF16)
    shifter.end_of_step()


def _attn_prompt(u, new3, cache3):
    nh = A_HEADS
    assert (SEQ // ATT_SB) * nh == DEC_BATCH and SHIFT_SLOTS * SHIFT_CHUNK == A_WINDOWS[2] - DEC_SEQ
    in_specs, scratch = [], []
    for g in range(3):
        in_specs.append(pl.BlockSpec((ATT_SB, LANES), lambda h, sb, g=g: (sb, g * nh + h)))
    for g, d in enumerate(A_DILATIONS):
        win = A_REACH * d
        for part in (3, 6):
            cb = part * nh + g * nh
            in_specs.append(pl.BlockSpec((ATT_SB, LANES), lambda h, sb, cb=cb: (sb, cb + h)))
        buf_rows = d * ATT_RESIDUE_PITCH if d == ATT_RESIDUE_MAJOR_D else win + ATT_SB
        scratch += [pltpu.VMEM((buf_rows, LANES), F32), pltpu.VMEM((buf_rows, LANES), F32)]
    scratch += [pltpu.VMEM((ATT_SB, LANES), F32)] * 3
    scratch += [pltpu.VMEM((SHIFT_SLOTS, SHIFT_CHUNK + 1, 2, A_HEADS, A_HEAD_DIM), F32),
                pltpu.SemaphoreType.DMA((SHIFT_SLOTS,)), pltpu.SemaphoreType.DMA((SHIFT_SLOTS,)),
                pltpu.SemaphoreType.DMA((1,))]
    in_specs += [pl.BlockSpec(new3.shape, lambda h, sb: (0, 0, 0, 0, 0)), pl.BlockSpec(memory_space=pl.ANY)]
    return pl.pallas_call(
        _attn_prompt_kernel,
        grid=(nh, SEQ // ATT_SB),
        in_specs=in_specs,
        out_specs=[pl.BlockSpec((ATT_SB, LANES), lambda h, sb: (sb, h)), pl.BlockSpec(memory_space=pl.ANY)],
        out_shape=[jax.ShapeDtypeStruct((SEQ, A_WIDTH), BF16), jax.ShapeDtypeStruct(cache3.shape, cache3.dtype)],
        scratch_shapes=scratch,
        compiler_params=_cp(("arbitrary", "arbitrary"), vmem=ATT_VMEM_LIMIT),
        name="attn_prompt",
    )(*([u] * 9), new3, cache3)


def _attn_sample_kernel(qkv_ref, c1_ref, c2_ref, c3_ref, o_ref):
    scale = A_HEAD_DIM ** -0.5
    pos = lax.broadcasted_iota(jnp.int32, (A_REACH, 1, 1), 0)

    def merge(state, m_t, l_t, num):
        if state is None:
            return m_t, l_t, num
        m_o, l_o, n_o = state
        m_n = jnp.maximum(m_o, m_t)
        a, b = jnp.exp(m_o - m_n), jnp.exp(m_t - m_n)
        return m_n, a * l_o + b * l_t, a * n_o + b * num

    for s in range(DEC_SEQ):
        state = None
        for g in range(3):
            q = qkv_ref[0, s, g]
            if g == 0:
                kt, vt = c1_ref[0, :, 0], c1_ref[0, :, 1]
                valid = pos >= s
                new_rows = range(s + 1)
            else:
                c_ref = c2_ref if g == 1 else c3_ref
                kt, vt = c_ref[0, :, 2 * s], c_ref[0, :, 2 * s + 1]
                valid = None
                new_rows = (s,)
            sc = jnp.sum(kt * q[None], axis=-1, keepdims=True) * scale
            if valid is not None:
                sc = jnp.where(valid, sc, -jnp.inf)
            s_new = [jnp.sum(qkv_ref[0, t, 3 + g] * q, axis=-1, keepdims=True) * scale for t in new_rows]
            m_t = jnp.max(sc, axis=0)
            for sn in s_new:
                m_t = jnp.maximum(m_t, sn)
            p = jnp.exp(sc - m_t[None])
            l_t = jnp.sum(p, axis=0)
            num = jnp.sum(p * vt, axis=0)
            for t, sn in zip(new_rows, s_new):
                pn = jnp.exp(sn - m_t)
                l_t = l_t + pn
                num = num + pn * qkv_ref[0, t, 6 + g]
            state = merge(state, m_t, l_t, num)
        o_ref[0, s] = state[2] / state[1]


def _attn_sample(qkv_s, c1, c2, c3):
    tile = (A_HEADS, A_HEAD_DIM)
    return pl.pallas_call(
        _attn_sample_kernel,
        grid=(DEC_BATCH,),
        in_specs=[
            pl.BlockSpec((1, DEC_SEQ, 9) + tile, lambda b: (b, 0, 0, 0, 0)),
            pl.BlockSpec((1, A_REACH, 2) + tile, lambda b: (b, 0, 0, 0, 0)),
            pl.BlockSpec((1, A_REACH, 2 * DEC_SEQ) + tile, lambda b: (b, 0, 0, 0, 0)),
            pl.BlockSpec((1, A_REACH, 2 * DEC_SEQ) + tile, lambda b: (b, 0, 0, 0, 0)),
        ],
        out_specs=pl.BlockSpec((1, DEC_SEQ) + tile, lambda b: (b, 0, 0, 0)),
        out_shape=jax.ShapeDtypeStruct((DEC_BATCH, DEC_SEQ) + tile, F32),
        compiler_params=_cp(("arbitrary",)),
        name="attn_sample",
    )(qkv_s, c1, c2, c3)


def _load_resident(step, pairs):
    @pl.when(step == 0)
    def _():
        for src, dst in pairs:
            pltpu.sync_copy(src, dst)


def _merge_ln1_kernel(oa_ref, og_ref, gt_ref, xn_ref, g_ref, b_ref, n2_ref, wa_hbm, wb_hbm, wo_hbm, c2_hbm,
                      x1_ref, kv2_hbm, wa, wb, wo, mixed, resid, cbuf, rsem, wsem, nsem):
    shifter = _whole_row_shifter(pl.program_id(0), M_ROWS // MERGE_ROWS, A_WINDOWS[1], n2_ref, c2_hbm, kv2_hbm,
                                 cbuf, rsem, wsem, nsem)
    shifter.top_of_step()
    _load_resident(pl.program_id(0), ((wa_hbm, wa), (wb_hbm, wb), (wo_hbm, wo)))
    nj = D_MODEL // COL_TILE
    oa, og = oa_ref[...], og_ref[...]
    for t in range(nj):
        cs = slice(t * COL_TILE, (t + 1) * COL_TILE)
        gs = slice(D_MODEL + t * COL_TILE, D_MODEL + (t + 1) * COL_TILE)
        ya = _dot(oa, wa[:, cs])
        yb = _dot(og, wb[:, cs])
        ga, gb = gt_ref[:, cs].astype(F32), gt_ref[:, gs].astype(F32)
        mixed[:, cs] = (_sigmoid(ga) * ya + _sigmoid(gb) * yb).astype(BF16)
    shifter.turn_reads_into_writes()
    mx = mixed[...]
    for t in range(nj):
        cs = slice(t * COL_TILE, (t + 1) * COL_TILE)
        resid[:, cs] = ALPHA * xn_ref[:, cs] + _dot(mx, wo[:, cs])
    x1_ref[...] = _layer_norm_rows(resid[...], g_ref[...], b_ref[...])
    shifter.end_of_step()


def _merge_ln1(o_a, o_g, gates, xn, g, b, wa, wb, wo, new2, cache2):
    tm = MERGE_ROWS
    assert M_ROWS // tm >= DEC_BATCH // SMALL_SHIFT_ROWS
    any_spec = pl.BlockSpec(memory_space=pl.ANY)
    return pl.pallas_call(
        _merge_ln1_kernel,
        grid=(M_ROWS // tm,),
        in_specs=[
            pl.BlockSpec((tm, A_WIDTH), lambda i: (i, 0)),
            pl.BlockSpec((tm, GLA_V), lambda i: (i, 0)),
            pl.BlockSpec((tm, 2 * D_MODEL), lambda i: (i, 0)),
            pl.BlockSpec((tm, D_MODEL), lambda i: (i, 0)),
            pl.BlockSpec((1, D_MODEL), lambda i: (0, 0)),
            pl.BlockSpec((1, D_MODEL), lambda i: (0, 0)),
            pl.BlockSpec(new2.shape, lambda i: (0, 0, 0, 0, 0)),
            any_spec, any_spec, any_spec, any_spec,
        ],
        out_specs=[pl.BlockSpec((tm, D_MODEL), lambda i: (i, 0)), any_spec],
        out_shape=[jax.ShapeDtypeStruct((M_ROWS, D_MODEL), F32),
                   jax.ShapeDtypeStruct(cache2.shape, cache2.dtype)],
        scratch_shapes=[pltpu.VMEM((A_WIDTH, D_MODEL), BF16), pltpu.VMEM((GLA_V, D_MODEL), BF16),
                        pltpu.VMEM((D_MODEL, D_MODEL), BF16),
                        pltpu.VMEM((tm, D_MODEL), BF16), pltpu.VMEM((tm, D_MODEL), F32)]
        + _whole_row_shift_scratch(A_WINDOWS[1]),
        compiler_params=_cp(("arbitrary",), vmem=ATT_VMEM_LIMIT),
        name="merge_ln1",
    )(o_a, o_g, gates, xn, g.reshape(1, D_MODEL), b.reshape(1, D_MODEL), new2, wa, wb, wo, cache2)


def _route_block(x, w, idx_ref, wt_ref, cnt_ref):
    tm = x.shape[0]
    xh, xm, _ = _split3(x)
    wh, wm, _ = _split3(w)
    logits = _dot(xh, wh) + _dot(xh, wm) + _dot(xm, wh)
    lane_i = lax.broadcasted_iota(jnp.int32, (tm, LANES), 1)
    lane = lane_i.astype(F32)
    lane_group = (lane_i >> 3).astype(F32)
    neg = -jnp.inf

    def top1(v):
        mx = jnp.max(v, axis=1, keepdims=True)
        ix = jnp.min(jnp.where(v == mx, lane, float(LANES)), axis=1, keepdims=True)
        return mx, ix

    gl = jnp.where(lane_i < MOE_GROUPS, logits[:, :LANES], neg)
    g_max, g_idx = top1(gl)
    g_p = 1.0 / jnp.sum(jnp.exp(gl - g_max), axis=1, keepdims=True)
    el = logits[:, LANES:]
    in_group = jnp.logical_and(lane_i < N_EXPERTS, lane_group == g_idx)
    e1v = jnp.where(in_group, el, neg)
    m1, i1 = top1(e1v)
    m2, i2 = top1(jnp.where(lane == i1, neg, e1v))
    t = jnp.exp(m2 - m1)
    w1 = g_p / (1.0 + t)
    w2 = g_p * t / (1.0 + t)
    wt_ref[...] = jnp.where(lane_i == 0, w1, jnp.where(lane_i == 1, w2, 0.0))

    @pl.when(pl.program_id(0) == 0)
    def _():
        cnt_ref[...] = jnp.zeros_like(cnt_ref)

    is1, is2 = lane == i1, lane == i2
    onehot = jnp.where(jnp.logical_or(is1, is2), 1.0, 0.0)
    r = lax.broadcasted_iota(jnp.int32, (tm, tm), 0)
    c = lax.broadcasted_iota(jnp.int32, (tm, tm), 1)
    before = _dot(jnp.where(r > c, 1.0, 0.0).astype(BF16), onehot.astype(BF16)) + cnt_ref[...]
    rank1 = jnp.sum(jnp.where(is1, before, 0.0), axis=1, keepdims=True)
    rank2 = jnp.sum(jnp.where(is2, before, 0.0), axis=1, keepdims=True)
    cnt_ref[...] += jnp.sum(onehot, axis=0, keepdims=True)
    idx_ref[...] = jnp.where(lane_i == 0, i1, jnp.where(lane_i == 1, i2, jnp.where(
        lane_i == 2, rank1, jnp.where(lane_i == 3, rank2, 0.0)))).astype(jnp.int32)


def _ple_router_kernel(x_ref, p_ref, wr_ref, wg_hbm, wp_hbm, o_ref, idx_ref, wt_ref, cnt_ref, wg, wp):
    _load_resident(pl.program_id(0), ((wg_hbm, wg), (wp_hbm, wp)))
    x = x_ref[...]
    xb = x.astype(BF16)
    p = p_ref[...]
    for t in range(D_MODEL // COL_TILE):
        cs = slice(t * COL_TILE, (t + 1) * COL_TILE)
        gate = _sigmoid(_dot(xb, wg[:, cs]))
        o_ref[:, cs] = ALPHA * x_ref[:, cs] + gate * _dot(p, wp[:, cs])
    _route_block(x, wr_ref[...], idx_ref, wt_ref, cnt_ref)


def _ple_router(x1, p, wpg, wple, w_r1, w_r2):
    wr = jnp.zeros((D_MODEL, 2 * LANES), F32)
    wr = wr.at[:, :MOE_GROUPS].set(w_r1).at[:, LANES:LANES + N_EXPERTS].set(w_r2)
    tm = ROW_TILE
    any_spec = pl.BlockSpec(memory_space=pl.ANY)
    return pl.pallas_call(
        _ple_router_kernel,
        grid=(M_ROWS // tm,),
        in_specs=[
            pl.BlockSpec((tm, D_MODEL), lambda i: (i, 0)),
            pl.BlockSpec((tm, PLE_DIM), lambda i: (i, 0)),
            pl.BlockSpec((D_MODEL, 2 * LANES), lambda i: (0, 0)),
            any_spec, any_spec,
        ],
        out_specs=[pl.BlockSpec((tm, D_MODEL), lambda i: (i, 0)),
                   pl.BlockSpec((tm, LANES), lambda i: (i, 0)),
                   pl.BlockSpec((tm, LANES), lambda i: (i, 0)),
                   pl.BlockSpec((1, LANES), lambda i: (0, 0))],
        out_shape=[jax.ShapeDtypeStruct((M_ROWS, D_MODEL), F32),
                   jax.ShapeDtypeStruct((M_ROWS, LANES), jnp.int32),
                   jax.ShapeDtypeStruct((M_ROWS, LANES), F32),
                   jax.ShapeDtypeStruct((1, LANES), F32)],
        scratch_shapes=[pltpu.VMEM((D_MODEL, D_MODEL), BF16), pltpu.VMEM((PLE_DIM, D_MODEL), BF16)],
        compiler_params=_cp(("arbitrary",)),
        name="ple_router",
    )(x1, p, wr, wpg, wple)


WEIGHT_DMA_PRIORITY = 1


def _moe_kernel(be_ref, nx_ref, nv_ref, tok_ref, x_hbm, wg_hbm, wu_hbm, wd_hbm, y_ref,
                sg, su, sd, wgb, wub, wdb, xbuf, wsem, xsem):
    i = pl.program_id(0)
    nv = nv_ref[0]
    e = be_ref[i]

    def weight_copies(ex):
        return (pltpu.make_async_copy(wg_hbm.at[ex], sg, wsem.at[0]),
                pltpu.make_async_copy(wu_hbm.at[ex], su, wsem.at[1]),
                pltpu.make_async_copy(wd_hbm.at[ex], sd, wsem.at[2]))

    def row_copy(blk, slot, r):
        src = tok_ref[blk * MOE_BLK + r]
        return pltpu.make_async_copy(x_hbm.at[pl.ds(src, 1)], xbuf.at[slot, pl.ds(r, 1)], xsem.at[slot])

    def for_rows(blk, slot, action):
        def body(r, carry):
            action(row_copy(blk, slot, r))
            return carry
        lax.fori_loop(0, MOE_BLK, body, 0, unroll=8)

    @pl.when(i == 0)
    def _():
        for cp in weight_copies(e):
            cp.start(priority=WEIGHT_DMA_PRIORITY)
        for_rows(0, 0, lambda cp: cp.start())

    @pl.when(i + 1 < nv)
    def _():
        for_rows(i + 1, (i + 1) & 1, lambda cp: cp.start())

    first_of_expert = jnp.logical_or(i == 0, e != be_ref[jnp.maximum(i - 1, 0)])

    @pl.when(jnp.logical_and(i < nv, first_of_expert))
    def _():
        for cp in weight_copies(e):
            cp.wait()
        for r in range(0, D_MODEL, MOE_BLK):
            wgb[r:r + MOE_BLK, :] = sg[r:r + MOE_BLK, :].astype(BF16)
            wub[r:r + MOE_BLK, :] = su[r:r + MOE_BLK, :].astype(BF16)
        for r in range(0, D_EXPERT, MOE_BLK):
            wdb[r:r + MOE_BLK, :] = sd[r:r + MOE_BLK, :].astype(BF16)

        @pl.when(nx_ref[i] >= 0)
        def _():
            for cp in weight_copies(nx_ref[i]):
                cp.start(priority=WEIGHT_DMA_PRIORITY)

    @pl.when(i < nv)
    def _():
        slot = i & 1
        for_rows(i, slot, lambda cp: cp.wait())
        x = xbuf[slot].astype(BF16)
        a = _dot(x, wgb[...])
        h = (a * _sigmoid(a) * _dot(x, wub[...])).astype(BF16)
        y_ref[...] = _dot(h, wdb[...])

    @pl.when(i >= nv)
    def _():
        y_ref[...] = jnp.zeros_like(y_ref)


def _moe_experts(x1, row_tok, blk_e, blk_next, n_valid, w_gate, w_up, w_down):
    any_spec = pl.BlockSpec(memory_space=pl.ANY)
    return pl.pallas_call(
        _moe_kernel,
        grid_spec=pltpu.PrefetchScalarGridSpec(
            num_scalar_prefetch=4,
            grid=(MOE_NBLK,),
            in_specs=[any_spec, any_spec, any_spec, any_spec],
            out_specs=pl.BlockSpec((MOE_BLK, D_MODEL), lambda i, be, nx, nv, tok: (i, 0)),
            scratch_shapes=[
                pltpu.VMEM((D_MODEL, D_EXPERT), F32), pltpu.VMEM((D_MODEL, D_EXPERT), F32),
                pltpu.VMEM((D_EXPERT, D_MODEL), F32),
                pltpu.VMEM((D_MODEL, D_EXPERT), BF16), pltpu.VMEM((D_MODEL, D_EXPERT), BF16),
                pltpu.VMEM((D_EXPERT, D_MODEL), BF16),
                pltpu.VMEM((2, MOE_BLK, D_MODEL), F32),
                pltpu.SemaphoreType.DMA((3,)), pltpu.SemaphoreType.DMA((2,)),
            ],
        ),
        out_shape=jax.ShapeDtypeStruct((MOE_ROWS, D_MODEL), F32),
        compiler_params=_cp(("arbitrary",)),
        name="moe_experts",
    )(blk_e, blk_next, n_valid, row_tok, x1, w_gate, w_up, w_down)


def _moe_plan(e_idx, rank, counts):
    ids = jnp.arange(N_EXPERTS, dtype=jnp.int32)
    pcounts = (counts + MOE_BLK - 1) // MOE_BLK * MOE_BLK
    pends = jnp.cumsum(pcounts)
    pstarts = pends - pcounts
    dest = jnp.sum(jnp.where(e_idx[..., None] == ids, pstarts, 0), axis=-1) + rank
    tok = jnp.broadcast_to(jnp.arange(M_ROWS, dtype=jnp.int32)[:, None], dest.shape)
    row_tok = jnp.zeros((MOE_ROWS,), jnp.int32).at[dest.reshape(-1)].set(tok.reshape(-1))
    n_valid = (pends[-1] // MOE_BLK).astype(jnp.int32)
    blk_start = jnp.minimum(jnp.arange(MOE_NBLK, dtype=jnp.int32) * MOE_BLK, pends[-1] - 1)
    blk_e = jnp.sum((pends[None, :] <= blk_start[:, None]).astype(jnp.int32), axis=1)
    blk_e = jnp.clip(blk_e, 0, N_EXPERTS - 1).astype(jnp.int32)
    later = jnp.where((counts[None, :] > 0) & (ids[None, :] > ids[:, None]), ids[None, :], N_EXPERTS)
    next_e = jnp.min(later, axis=1)
    next_e = jnp.where(next_e == N_EXPERTS, -1, next_e).astype(jnp.int32)
    blk_next = jnp.sum(jnp.where(blk_e[:, None] == ids, next_e, 0), axis=1).astype(jnp.int32)
    return dest, row_tok, blk_e, blk_next, n_valid.reshape(1)


def _final_row_copy(dest_ref, y_hbm, ybuf, sem, blk, slot, k, t):
    src = dest_ref[k * M_ROWS + blk * LN_ROWS + t]
    return pltpu.make_async_copy(y_hbm.at[pl.ds(src, 1)], ybuf.at[slot, pl.ds(k * LN_ROWS + t, 1)], sem.at[slot])


def _final_kernel(dest_ref, base_ref, wt_ref, g_ref, b_ref, y_hbm, op_ref, os_ref, ybuf, sem):
    i = pl.program_id(0)
    n_rows = 2 * LN_ROWS

    def for_rows(blk, slot, action):
        for k in range(2):
            for t in range(LN_ROWS):
                action(_final_row_copy(dest_ref, y_hbm, ybuf, sem, blk, slot, k, t))

    @pl.when(i == 0)
    def _():
        for k in range(2):
            def body(t, carry, k=k):
                _final_row_copy(dest_ref, y_hbm, ybuf, sem, 0, 0, k, t).start()
                return carry
            lax.fori_loop(0, LN_ROWS, body, 0, unroll=8)

    @pl.when(i + 1 < pl.num_programs(0))
    def _():
        for_rows(i + 1, (i + 1) & 1, lambda cp: cp.start())

    slot = i & 1
    for_rows(i, slot, lambda cp: cp.wait())

    wt = wt_ref[...]
    r = base_ref[...] + wt[:, 0:1] * ybuf[slot, 0:LN_ROWS, :] + wt[:, 1:2] * ybuf[slot, LN_ROWS:n_rows, :]
    y = _layer_norm_rows(r, g_ref[...], b_ref[...])

    @pl.when(i < SEQ // LN_ROWS)
    def _():
        op_ref[...] = y

    @pl.when(i >= SEQ // LN_ROWS)
    def _():
        os_ref[...] = y


def _final(dest_flat, base, y_rows, wt, g, b):
    npb = SEQ // LN_ROWS
    return pl.pallas_call(
        _final_kernel,
        grid_spec=pltpu.PrefetchScalarGridSpec(
            num_scalar_prefetch=1,
            grid=(npb + 1,),
            in_specs=[
                pl.BlockSpec((LN_ROWS, D_MODEL), lambda i, d: (i, 0)),
                pl.BlockSpec((LN_ROWS, LANES), lambda i, d: (i, 0)),
                pl.BlockSpec((1, D_MODEL), lambda i, d: (0, 0)),
                pl.BlockSpec((1, D_MODEL), lambda i, d: (0, 0)),
                pl.BlockSpec(memory_space=pl.ANY),
            ],
            out_specs=[pl.BlockSpec((LN_ROWS, D_MODEL), lambda i, d: (jnp.minimum(i, npb - 1), 0)),
                       pl.BlockSpec((N_SAMPLE, D_MODEL), lambda i, d: (0, 0))],
            scratch_shapes=[pltpu.VMEM((2, 2 * LN_ROWS, D_MODEL), F32), pltpu.SemaphoreType.DMA((2,))],
        ),
        out_shape=[jax.ShapeDtypeStruct((SEQ, D_MODEL), F32),
                   jax.ShapeDtypeStruct((N_SAMPLE, D_MODEL), F32)],
        compiler_params=_cp(("arbitrary",)),
        name="combine_ln2",
    )(dest_flat, base, wt, g.reshape(1, D_MODEL), b.reshape(1, D_MODEL), y_rows)


def kernel(x_prompt, x_sample, p_prompt, p_sample, cache_kv_a1, cache_kv_a2, cache_kv_a3, state_gla,
           ln_emb_g, ln_emb_b, ln1_g, ln1_b, ln2_g, ln2_b, w_in, w_gk2, b_gk, gla_norm_g, w_br_a, w_br_b,
           w_out, w_router_group, w_router_expert, w_gate, w_up, w_down, w_ple_gate, w_ple):
    xp = x_prompt.reshape(SEQ, D_MODEL)
    xs = x_sample.reshape(N_SAMPLE, D_MODEL)
    xn, xb = _ln_emb(xp, xs, ln_emb_g, ln_emb_b)

    w_t = jnp.transpose(w_in[0])
    rope = _rope_tables()
    u = _project(xb, w_t, 0, A_QKV_COLS, 2 * 3 * A_WIDTH // COL_TILE, rope, F32)
    u_gla = _project(xb, w_t, A_QKV_COLS, MAIN_COLS - A_QKV_COLS, 0, rope, BF16)
    gates = _project(xb, w_t, MAIN_COLS + GLA_LR, 2 * D_MODEL, 0, rope, BF16)
    gk = _gla_gate(xb, w_t[MAIN_COLS:MAIN_COLS + GLA_LR], w_gk2[0], b_gk[0])

    qkv_s = u[SEQ:].reshape(DEC_BATCH, DEC_SEQ, 9, A_HEADS, A_HEAD_DIM)
    kv_new = [jnp.stack([qkv_s[:, :, 3 + g], qkv_s[:, :, 6 + g]], axis=2) for g in range(3)]
    o_a, kv_s3 = _attn_prompt(u, kv_new[2], cache_kv_a3)
    c1 = cache_kv_a1.reshape(DEC_BATCH, A_REACH, 2, A_HEADS, A_HEAD_DIM)
    c2 = cache_kv_a2.reshape(DEC_BATCH, A_REACH, 4 * 2, A_HEADS, A_HEAD_DIM)
    c3 = cache_kv_a3.reshape(DEC_BATCH, A_REACH, 16 * 2, A_HEADS, A_HEAD_DIM)
    o_as = _attn_sample(qkv_s, c1, c2, c3)
    o_a = jnp.concatenate([o_a, o_as.reshape(N_SAMPLE, A_WIDTH).astype(BF16)], axis=0)

    o_g, st_t, kv_s1 = _gla_prompt(u_gla, gk, gla_norm_g[0], kv_new[0], cache_kv_a1)

    def spad(a):
        a = a.astype(F32).reshape(DEC_BATCH, DEC_SEQ, a.shape[-1])
        return jnp.pad(a, ((0, 0), (0, GLA_SPAD - DEC_SEQ), (0, 0)))

    ug_s = u_gla[SEQ:]
    o_gs, st_s = _gla_sample(
        spad(ug_s[:, :GLA_QK]), spad(ug_s[:, GLA_QK:2 * GLA_QK]),
        spad(ug_s[:, 2 * GLA_QK:2 * GLA_QK + GLA_V]), spad(ug_s[:, 2 * GLA_QK + GLA_V:]),
        spad(gk[SEQ:]), gla_norm_g[0], state_gla[0])
    o_gs = o_gs[:, :DEC_SEQ].reshape(N_SAMPLE, GLA_V).astype(BF16)
    o_g = jnp.concatenate([o_g, o_gs], axis=0)

    x1, kv_s2 = _merge_ln1(o_a, o_g, gates, xn, ln1_g[0], ln1_b[0],
                           w_br_a[0].astype(BF16), w_br_b[0].astype(BF16), w_out[0].astype(BF16),
                           kv_new[1], cache_kv_a2)

    p = jnp.concatenate([p_prompt[0].reshape(SEQ, PLE_DIM), p_sample[0].reshape(N_SAMPLE, PLE_DIM)], axis=0)
    base, ridx, rwt, rcnt = _ple_router(x1, p.astype(BF16), w_ple_gate[0].astype(BF16), w_ple[0].astype(BF16),
                                        w_router_group[0], w_router_expert[0])
    dest, row_tok, blk_e, blk_next, n_valid = _moe_plan(
        ridx[:, :2], ridx[:, 2:4], rcnt[0, :N_EXPERTS].astype(jnp.int32))
    y_rows =_moe_experts(x1, row_tok, blk_e, blk_next, n_valid, w_gate[0], w_up[0], w_down[0])
    y_p, y_s = _final(dest.T.reshape(-1), base, y_rows, rwt, ln2_g[0], ln2_b[0])

    def kv_prompt(g):
        w = A_WINDOWS[g]
        k = u[SEQ - w:SEQ, (3 + g) * A_WIDTH:(4 + g) * A_WIDTH].reshape(w, A_HEADS, A_HEAD_DIM)
        v = u[SEQ - w:SEQ, (6 + g) * A_WIDTH:(7 + g) * A_WIDTH].reshape(w, A_HEADS, A_HEAD_DIM)
        return jnp.stack([k, v], axis=1)[None, None]


    gla_state_prompt = st_t[None, None]
    return (y_p.reshape(1, SEQ, D_MODEL), y_s.reshape(DEC_BATCH, DEC_SEQ, D_MODEL),
            kv_prompt(0), kv_prompt(1), kv_prompt(2), gla_state_prompt,
            kv_s1, kv_s2, kv_s3, st_s[None])
```

```python
import functools

import jax
import jax.numpy as jnp
import numpy as np
from jax import lax
from jax.experimental import pallas as pl
from jax.experimental.pallas import tpu as pltpu

F32 = jnp.float32
BF16 = jnp.bfloat16

D_MODEL = 2048
SEQ = 8192
DEC_BATCH = 32
DEC_SEQ = 4
PAST_LEN = 16384
N_SAMPLE = DEC_BATCH * DEC_SEQ
M_ROWS = SEQ + N_SAMPLE
A_WINDOWS = (128, 512, 2048)
A_DILATIONS = (1, 4, 16)
A_REACH = 128
A_HEADS = 8
A_HEAD_DIM = 128
A_WIDTH = A_HEADS * A_HEAD_DIM
ROT_DIM = 32
ROPE_THETA = 500000.0
A_QKV_COLS = 9 * A_WIDTH
GLA_HEADS = 4
GLA_DK = 256
GLA_DV = 512
GLA_LR = 16
GLA_TAU = 16.0
GLA_CHUNK = 64
GLA_QK = GLA_HEADS * GLA_DK
GLA_V = GLA_HEADS * GLA_DV
MAIN_COLS = A_QKV_COLS + 2 * GLA_QK + 2 * GLA_V
MOE_GROUPS = 4
MOE_PER_GROUP = 8
N_EXPERTS = 32
D_EXPERT = 1024
PLE_DIM = 256
LN_EPS = 1e-5
ALPHA = 2.0 ** 0.25

LANES = 128
VMEM_LIMIT = 56 * 1024 * 1024
ROW_TILE_PROJ = 2080
ROW_TILE = 832
MERGE_ROWS = 416
COL_TILE = 512
LN_ROWS = 128
MOE_BLK = 256
MOE_NBLK = (2 * M_ROWS) // MOE_BLK + N_EXPERTS
MOE_ROWS = MOE_NBLK * MOE_BLK
ATT_SB = 2048
GLA_ROWS = 512
ATT_VMEM_LIMIT = 60 * 1024 * 1024
SHIFT_SLOTS = 4
SHIFT_CHUNK = 511
SMALL_SHIFT_ROWS = 2
ATT_UNROLL = 4
ATT_RESIDUE_MAJOR_D = 16
ATT_RESIDUE_PITCH = 264

def _cp(sem, vmem=VMEM_LIMIT):
    return pltpu.CompilerParams(dimension_semantics=sem, vmem_limit_bytes=vmem)


def _sigmoid(x):
    return 1.0 / (1.0 + jnp.exp(-x))


def _dot(a, b):
    return jnp.dot(a, b, preferred_element_type=F32)


def _dot_nt(a, b):
    return lax.dot_general(a, b, (((1,), (1,)), ((), ())), preferred_element_type=F32)


def _split3(x):
    h = x.astype(BF16)
    r = x - h.astype(F32)
    m = r.astype(BF16)
    l = (r - m.astype(F32)).astype(BF16)
    return h, m, l


def _layer_norm_rows(x, g, b):
    mu = jnp.mean(x, axis=-1, keepdims=True)
    xc = x - mu
    var = jnp.mean(xc * xc, axis=-1, keepdims=True)
    return xc * lax.rsqrt(var + LN_EPS) * g + b


LN_GROUP = 4


def _ln_emb_kernel(xp_ref, xs_ref, g_ref, b_ref, of_ref, ob_ref):
    is_prompt = pl.program_id(0) < SEQ // (LN_ROWS * LN_GROUP)
    x = jnp.where(is_prompt, xp_ref[...], xs_ref[...][None])
    y = _layer_norm_rows(x, g_ref[...], b_ref[...])
    of_ref[...] = y
    ob_ref[...] = y.astype(BF16)


def _ln_emb(xp, xs, g, b):
    n_groups = M_ROWS // LN_ROWS
    npb = SEQ // (LN_ROWS * LN_GROUP)
    blk = (LN_GROUP, LN_ROWS, D_MODEL)
    xn, xb = pl.pallas_call(
        _ln_emb_kernel,
        grid=(npb + 1,),
        in_specs=[
            pl.BlockSpec(blk, lambda i: (jnp.minimum(i, npb - 1), 0, 0)),
            pl.BlockSpec((N_SAMPLE, D_MODEL), lambda i: (0, 0)),
            pl.BlockSpec((1, D_MODEL), lambda i: (0, 0)),
            pl.BlockSpec((1, D_MODEL), lambda i: (0, 0)),
        ],
        out_specs=[pl.BlockSpec(blk, lambda i: (i, 0, 0)), pl.BlockSpec(blk, lambda i: (i, 0, 0))],
        out_shape=[jax.ShapeDtypeStruct((n_groups, LN_ROWS, D_MODEL), F32),
                   jax.ShapeDtypeStruct((n_groups, LN_ROWS, D_MODEL), BF16)],
        compiler_params=_cp(("arbitrary",)),
        name="ln_emb",
    )(xp.reshape(SEQ // LN_ROWS, LN_ROWS, D_MODEL), xs, g.reshape(1, D_MODEL), b.reshape(1, D_MODEL))
    return xn.reshape(M_ROWS, D_MODEL), xb.reshape(M_ROWS, D_MODEL)


def _proj_kernel(x_ref, w_ref, c_ref, s1_ref, s2_ref, o_ref, *, n_rope_blocks):
    j = pl.program_id(1)
    acc = _dot_nt(x_ref[...], w_ref[...].astype(BF16))

    @pl.when(j >= n_rope_blocks)
    def _():
        o_ref[...] = acc.astype(o_ref.dtype)

    @pl.when(j < n_rope_blocks)
    def _():
        c, s1, s2 = c_ref[...], s1_ref[...], s2_ref[...]
        for t in range(COL_TILE // LANES):
            a = acc[:, t * LANES:(t + 1) * LANES]
            o_ref[:, t * LANES:(t + 1) * LANES] = (
                a * c + pltpu.roll(a, LANES - ROT_DIM // 2, 1) * s1 + pltpu.roll(a, ROT_DIM // 2, 1) * s2
            ).astype(o_ref.dtype)


def _project(xb, w_t, row0, n_cols, n_rope_blocks, rope, out_dtype):
    c, s1, s2 = rope
    tm, tn = ROW_TILE_PROJ, COL_TILE
    return pl.pallas_call(
        functools.partial(_proj_kernel, n_rope_blocks=n_rope_blocks),
        grid=(M_ROWS // tm, n_cols // tn),
        in_specs=[
            pl.BlockSpec((tm, D_MODEL), lambda i, j: (i, 0)),
            pl.BlockSpec((pl.Element(tn), pl.Element(D_MODEL)),
                         lambda i, j: (pl.multiple_of(row0 + j * tn, 8), 0)),
            pl.BlockSpec((tm, LANES), lambda i, j: (i, 0)),
            pl.BlockSpec((tm, LANES), lambda i, j: (i, 0)),
            pl.BlockSpec((tm, LANES), lambda i, j: (i, 0)),
        ],
        out_specs=pl.BlockSpec((tm, tn), lambda i, j: (i, j)),
        out_shape=jax.ShapeDtypeStruct((M_ROWS, n_cols), out_dtype),
        compiler_params=_cp(("arbitrary", "arbitrary")),
        name="in_proj",
    )(xb, w_t, c, s1, s2)


def _rope_tables():
    half = ROT_DIM // 2
    inv = np.power(np.float32(ROPE_THETA), -np.arange(half, dtype=np.float32) * np.float32(2.0 / ROT_DIM))
    pos = np.concatenate([np.arange(SEQ), np.tile(PAST_LEN + np.arange(DEC_SEQ), DEC_BATCH)]).astype(np.float32)
    ang = (pos[:, None] * inv[None, :].astype(np.float32)).astype(np.float32)
    cos, sin = np.cos(ang).astype(np.float32), np.sin(ang).astype(np.float32)
    ones = np.ones((M_ROWS, LANES - ROT_DIM), np.float32)
    zeros = np.zeros((M_ROWS, LANES - ROT_DIM), np.float32)
    zh = np.zeros((M_ROWS, half), np.float32)
    c = np.concatenate([cos, cos, ones], axis=1)
    s1 = np.concatenate([-sin, zh, zeros], axis=1)
    s2 = np.concatenate([zh, sin, zeros], axis=1)
    return jnp.asarray(c), jnp.asarray(s1), jnp.asarray(s2)


def _gk_kernel(x_ref, wlr_ref, wgk_ref, b_ref, o_ref):
    blr = _dot_nt(x_ref[...], wlr_ref[...])
    z = _dot(blr.astype(BF16), wgk_ref[...]) + b_ref[...]
    o_ref[...] = (jnp.minimum(z, 0.0) - jnp.log1p(jnp.exp(-jnp.abs(z)))) * (1.0 / GLA_TAU)


def _gla_gate(xb, w_lr_t, w_gk2, b_gk):
    wlr = jnp.zeros((LANES, D_MODEL), BF16).at[:GLA_LR].set(w_lr_t.astype(BF16))
    wgk = jnp.zeros((LANES, GLA_QK), BF16).at[:GLA_LR].set(w_gk2.astype(BF16))
    tm = ROW_TILE
    return pl.pallas_call(
        _gk_kernel,
        grid=(M_ROWS // tm,),
        in_specs=[
            pl.BlockSpec((tm, D_MODEL), lambda i: (i, 0)),
            pl.BlockSpec((LANES, D_MODEL), lambda i: (0, 0)),
            pl.BlockSpec((LANES, GLA_QK), lambda i: (0, 0)),
            pl.BlockSpec((1, GLA_QK), lambda i: (0, 0)),
        ],
        out_specs=pl.BlockSpec((tm, GLA_QK), lambda i: (i, 0)),
        out_shape=jax.ShapeDtypeStruct((M_ROWS, GLA_QK), F32),
        compiler_params=_cp(("arbitrary",)),
        name="gla_gate",
    )(xb, wlr, wgk, b_gk.reshape(1, GLA_QK))


class _CacheShiftInBackground:
    def __init__(self, step, n_steps, pieces, news, buf, rsem, wsem, nsem, n_active=None):
        self.step, self.n_steps, self.pieces, self.news = step, n_steps, pieces, news
        self.buf, self.rsem, self.wsem, self.nsem = buf, rsem, wsem, nsem
        self.n_active = n_steps if n_active is None else n_active

    def _copies(self, step):
        out = []
        for slot, (cache, dst, b, pos0, n) in enumerate(self.pieces(step)):
            stage = self.buf.at[slot, pl.ds(0, n)]
            rd = pltpu.make_async_copy(cache.at[0, b, pl.ds(DEC_SEQ + pos0, n)], stage, self.rsem.at[slot])
            wr = pltpu.make_async_copy(stage, dst.at[0, b, pl.ds(pos0, n)], self.wsem.at[slot])
            out.append((rd, wr))
        return out

    def _new_rows(self):
        return [pltpu.make_async_copy(new, dst.at[0, :, pl.ds(w - DEC_SEQ, DEC_SEQ)], self.nsem.at[i])
                for i, (new, dst, w) in enumerate(self.news)]

    def top_of_step(self):
        @pl.when(self.step == 0)
        def _():
            for cp in self._new_rows():
                cp.start()

        @pl.when(jnp.logical_and(self.step > 0, self.step <= self.n_active))
        def _():
            for _, wr in self._copies(self.step - 1):
                wr.wait()

        @pl.when(self.step < self.n_active)
        def _():
            for rd, _ in self._copies(self.step):
                rd.start()

    def turn_reads_into_writes(self):
        @pl.when(self.step < self.n_active)
        def _():
            for rd, wr in self._copies(self.step):
                rd.wait()
                wr.start()

    def end_of_step(self):
        @pl.when(self.step == self.n_steps - 1)
        def _():
            if self.n_active == self.n_steps:
                for _, wr in self._copies(self.step):
                    wr.wait()
            for cp in self._new_rows():
                cp.wait()


def _gla_chunks(heads, gn, causal, tri_b, eye):
    n = heads[0][3].shape[0]
    rid = lax.broadcasted_iota(jnp.int32, (n + GLA_TPAD, GLA_DK), 0)
    bs = []
    for _, _, _, g, _, _ in heads:
        g1, g2, g3 = _split3(g)
        bs.append(_dot(tri_b, g1) + _dot(tri_b, g2) + _dot(tri_b, g3))
    ops = []
    for (q, k, v, _, _, _), b in zip(heads, bs):
        q, k = q.astype(F32), k.astype(F32)
        b_last = b[n - 1:n, :]
        q_dec = (q * ((GLA_DK ** -0.5) * jnp.exp(b))).astype(BF16)
        k_inv = (k * jnp.exp(-b)).astype(BF16)
        k_dec = k * jnp.exp(b_last - b)
        d1, d2, d3 = _split3(jnp.exp(b_last))
        rows = jnp.concatenate([k_dec, jnp.zeros((GLA_TPAD, GLA_DK), F32)], axis=0)
        rows = jnp.where(rid == n, d1.astype(F32), rows)
        rows = jnp.where(rid == n + 1, d2.astype(F32), rows)
        rows = jnp.where(rid == n + 2, d3.astype(F32), rows)
        ops.append((q_dec, k_inv, rows.astype(BF16), v.astype(BF16)))
    mids = []
    for q_dec, k_inv, rows, _ in ops:
        mids.append((jnp.where(causal, _dot_nt(q_dec, k_inv), 0.0).astype(BF16), _dot_nt(eye, rows)))
    outs = []
    for (q_dec, _, _, vb), (att, cols), (_, _, _, _, br, s0) in zip(ops, mids, heads):
        o = _dot(att, vb) + _dot(q_dec, s0.astype(BF16))
        dcol = cols[:, n:n + 1] + cols[:, n + 1:n + 2] + cols[:, n + 2:n + 3]
        s1 = s0 * dcol + _dot(cols[:, :n].astype(BF16), vb)
        br = br.astype(F32)
        ms = jnp.mean(o * o, axis=-1, keepdims=True)
        outs.append((o * lax.rsqrt(ms + LN_EPS) * gn * (br * _sigmoid(br)), s1))
    return outs


GLA_TPAD = 16


def _gla_consts(n):
    r = lax.broadcasted_iota(jnp.int32, (n, n), 0)
    c = lax.broadcasted_iota(jnp.int32, (n, n), 1)
    causal = r >= c
    er = lax.broadcasted_iota(jnp.int32, (GLA_DK, GLA_DK), 0)
    ec = lax.broadcasted_iota(jnp.int32, (GLA_DK, GLA_DK), 1)
    return causal, jnp.where(causal, 1.0, 0.0).astype(BF16), jnp.where(er == ec, 1.0, 0.0).astype(BF16)


def _whole_row_shifter(step, n_steps, window, new_ref, cache_hbm, out_hbm, cbuf, rsem, wsem, nsem):
    def pieces(s):
        return [(cache_hbm, out_hbm, s * SMALL_SHIFT_ROWS + j, 0, window - DEC_SEQ) for j in range(SMALL_SHIFT_ROWS)]
    return _CacheShiftInBackground(step, n_steps, pieces, [(new_ref, out_hbm, window)], cbuf, rsem, wsem, nsem,
                                   n_active=DEC_BATCH // SMALL_SHIFT_ROWS)


def _whole_row_shift_scratch(window):
    return [pltpu.VMEM((SMALL_SHIFT_ROWS, window - DEC_SEQ, 2, A_HEADS, A_HEAD_DIM), F32),
            pltpu.SemaphoreType.DMA((SMALL_SHIFT_ROWS,)), pltpu.SemaphoreType.DMA((SMALL_SHIFT_ROWS,)),
            pltpu.SemaphoreType.DMA((1,))]


def _gla_prompt_kernel(q_ref, k_ref, v_ref, br_ref, g_ref, gn_ref, n1_ref, c1_hbm,
                       o_ref, st_ref, kv1_hbm, cbuf, rsem, wsem, nsem):
    step = pl.program_id(0)
    shifter = _whole_row_shifter(step, SEQ // GLA_ROWS, A_WINDOWS[0], n1_ref, c1_hbm, kv1_hbm,
                                 cbuf, rsem, wsem, nsem)
    shifter.top_of_step()

    @pl.when(step == 0)
    def _():
        st_ref[...] = jnp.zeros_like(st_ref)

    causal, tri_b, eye = _gla_consts(GLA_CHUNK)
    gn = gn_ref[...]

    def body(cc, carry):
        rows = pl.ds(pl.multiple_of(cc * GLA_CHUNK, GLA_CHUNK), GLA_CHUNK)
        heads = []
        for h in range(GLA_HEADS):
            ks = slice(h * GLA_DK, (h + 1) * GLA_DK)
            vs = slice(h * GLA_DV, (h + 1) * GLA_DV)
            heads.append((q_ref[rows, ks], k_ref[rows, ks], v_ref[rows, vs], g_ref[rows, ks],
                          br_ref[rows, vs], st_ref[h]))
        for h, (o, s1) in enumerate(_gla_chunks(heads, gn, causal, tri_b, eye)):
            st_ref[h] = s1
            o_ref[rows, h * GLA_DV:(h + 1) * GLA_DV] = o.astype(BF16)
        return carry

    half = GLA_ROWS // GLA_CHUNK // 2
    lax.fori_loop(0, half, body, 0)
    shifter.turn_reads_into_writes()
    lax.fori_loop(half, 2 * half, body, 0)
    shifter.end_of_step()


def _gla_prompt(u, gk, gnorm, new1, cache1):
    n_steps = SEQ // GLA_ROWS
    assert n_steps >= DEC_BATCH // SMALL_SHIFT_ROWS
    any_spec = pl.BlockSpec(memory_space=pl.ANY)
    return pl.pallas_call(
        _gla_prompt_kernel,
        grid=(n_steps,),
        in_specs=[
            pl.BlockSpec((GLA_ROWS, GLA_QK), lambda c: (c, 0)),
            pl.BlockSpec((GLA_ROWS, GLA_QK), lambda c: (c, 1)),
            pl.BlockSpec((GLA_ROWS, GLA_V), lambda c: (c, 1)),
            pl.BlockSpec((GLA_ROWS, GLA_V), lambda c: (c, 2)),
            pl.BlockSpec((GLA_ROWS, GLA_QK), lambda c: (c, 0)),
            pl.BlockSpec((1, GLA_DV), lambda c: (0, 0)),
            pl.BlockSpec(new1.shape, lambda c: (0, 0, 0, 0, 0)),
            any_spec,
        ],
        out_specs=[
            pl.BlockSpec((GLA_ROWS, GLA_V), lambda c: (c, 0)),
            pl.BlockSpec((GLA_HEADS, GLA_DK, GLA_DV), lambda c: (0, 0, 0)),
            any_spec,
        ],
        out_shape=[jax.ShapeDtypeStruct((SEQ, GLA_V), BF16),
                   jax.ShapeDtypeStruct((GLA_HEADS, GLA_DK, GLA_DV), F32),
                   jax.ShapeDtypeStruct(cache1.shape, cache1.dtype)],
        scratch_shapes=_whole_row_shift_scratch(A_WINDOWS[0]),
        compiler_params=_cp(("arbitrary",)),
        name="gla_prompt",
    )(u, u, u, u, gk, gnorm.reshape(1, GLA_DV), new1, cache1)


GLA_SPAD = 16


def _gla_sample_kernel(q_ref, k_ref, v_ref, br_ref, g_ref, gn_ref, s0_ref, o_ref, s1_ref):
    causal, tri_b, eye = _gla_consts(GLA_SPAD)
    gn = gn_ref[...]
    heads = []
    for h in range(GLA_HEADS):
        ks = slice(h * GLA_DK, (h + 1) * GLA_DK)
        vs = slice(h * GLA_DV, (h + 1) * GLA_DV)
        heads.append((q_ref[0, :, ks], k_ref[0, :, ks], v_ref[0, :, vs], g_ref[0, :, ks],
                      br_ref[0, :, vs], s0_ref[0, h]))
    for h, (o, s1) in enumerate(_gla_chunks(heads, gn, causal, tri_b, eye)):
        s1_ref[0, h] = s1
        o_ref[0, :, h * GLA_DV:(h + 1) * GLA_DV] = o


def _gla_sample(q, k, v, br, g, gnorm, s0):
    def spec(c):
        return pl.BlockSpec((1, GLA_SPAD, c), lambda b: (b, 0, 0))
    st_spec = pl.BlockSpec((1, GLA_HEADS, GLA_DK, GLA_DV), lambda b: (b, 0, 0, 0))
    return pl.pallas_call(
        _gla_sample_kernel,
        grid=(DEC_BATCH,),
        in_specs=[spec(GLA_QK), spec(GLA_QK), spec(GLA_V), spec(GLA_V), spec(GLA_QK),
                  pl.BlockSpec((1, GLA_DV), lambda b: (0, 0)), st_spec],
        out_specs=[spec(GLA_V), st_spec],
        out_shape=[jax.ShapeDtypeStruct((DEC_BATCH, GLA_SPAD, GLA_V), F32),
                   jax.ShapeDtypeStruct((DEC_BATCH, GLA_HEADS, GLA_DK, GLA_DV), F32)],
        compiler_params=_cp(("arbitrary",)),
        name="gla_sample",
    )(q, k, v, br, g, gnorm.reshape(1, GLA_DV), s0)


def _attn_prompt_kernel(q0, q1, q2, kc0, vc0, kc1, vc1, kc2, vc2,
                        new3_ref, cache3_hbm, o_ref, kv3_hbm,
                        kb0, vb0, kb1, vb1, kb2, vb2, acc_ref, m_ref, l_ref, cbuf, rsem, wsem, nsem):
    sb = pl.program_id(1)
    step = pl.program_id(0) * pl.num_programs(1) + sb

    def pieces(s):
        return [(cache3_hbm, kv3_hbm, s, k * SHIFT_CHUNK, SHIFT_CHUNK) for k in range(SHIFT_SLOTS)]

    shifter = _CacheShiftInBackground(step, DEC_BATCH, pieces, [(new3_ref, kv3_hbm, A_WINDOWS[2])],
                                      cbuf, rsem, wsem, nsem)
    shifter.top_of_step()
    groups = ((q0, kc0, vc0, kb0, vb0, 1),
              (q1, kc1, vc1, kb1, vb1, 4),
              (q2, kc2, vc2, kb2, vb2, 16))

    @pl.when(step == 0)
    def _():
        for _, _, _, kb, vb, _ in groups:
            kb[...] = jnp.zeros_like(kb)
            vb[...] = jnp.zeros_like(vb)
    scale = A_HEAD_DIM ** -0.5
    n_win = 2 * A_REACH
    row = lax.broadcasted_iota(jnp.int32, (A_REACH, n_win), 0)
    col = lax.broadcasted_iota(jnp.int32, (A_REACH, n_win), 1)
    in_reach = jnp.logical_and(col >= row, col <= row + A_REACH)
    in_prev = col < A_REACH
    ones = jnp.ones((n_win, LANES), BF16)

    for gi, (q_ref, kc, vc, kb, vb, d) in enumerate(groups):
        win = A_REACH * d
        by_residue = d == ATT_RESIDUE_MAJOR_D
        if by_residue:
            def stage(jj, carry, kc=kc, vc=vc, kb=kb, vb=vb, d=d):
                src = pl.ds(pl.multiple_of(jj * d, d), d)
                for buf, cur in ((kb, kc), (vb, vc)):
                    buf[pl.ds(A_REACH + jj, d, stride=ATT_RESIDUE_PITCH), :] = cur[src, :]
                return carry
            lax.fori_loop(0, A_REACH, stage, 0, unroll=8)
        else:
            kb[win:win + ATT_SB, :] = kc[...]
            vb[win:win + ATT_SB, :] = vc[...]
        shift = {1: 0, 4: 2, 16: 4}[d]

        def tiles(it, carry, q_ref=q_ref, kb=kb, vb=vb, d=d, win=win, shift=shift, gi=gi, by_residue=by_residue):
            rows, scores = [], []
            for j in range(ATT_UNROLL):
                idx = it * ATT_UNROLL + j
                r = idx & (d - 1)
                nb = idx >> shift
                start = r + nb * win
                if d == 1:
                    start = pl.multiple_of(start, A_REACH)
                    qrows = pl.ds(start, A_REACH)
                    wrows = pl.ds(start, n_win)
                elif by_residue:
                    qrows = pl.ds(start, A_REACH, stride=d)
                    wrows = pl.ds(pl.multiple_of(r * ATT_RESIDUE_PITCH, 8), n_win)
                else:
                    qrows = pl.ds(start, A_REACH, stride=d)
                    wrows = pl.ds(start, n_win, stride=d)
                prev_bias = jnp.where(jnp.logical_or(sb > 0, nb > 0), 0.0, -jnp.inf)
                rows.append((qrows, wrows, prev_bias))
                scores.append(_dot_nt(q_ref[qrows, :].astype(BF16), kb[wrows, :].astype(BF16)))
            probs = []
            for s, (_, _, prev_bias) in zip(scores, rows):
                s = jnp.where(in_reach, s * scale, -jnp.inf) + jnp.where(in_prev, prev_bias, 0.0)
                m_t = jnp.max(s, axis=1, keepdims=True)
                probs.append((jnp.exp(s - m_t).astype(BF16), jnp.broadcast_to(m_t, (A_REACH, LANES))))
            sums = []
            for (p, _), (_, wrows, _) in zip(probs, rows):
                sums.append((_dot(p, ones), _dot(p, vb[wrows, :].astype(BF16))))
            for (l_t, num), (_, m_t), (qrows, _, _) in zip(sums, probs, rows):
                if gi == 0:
                    acc_ref[qrows, :] = num
                    m_ref[qrows, :] = m_t
                    l_ref[qrows, :] = l_t
                else:
                    m_o = m_ref[qrows, :]
                    m_n = jnp.maximum(m_o, m_t)
                    a = jnp.exp(m_o - m_n)
                    b = jnp.exp(m_t - m_n)
                    acc_ref[qrows, :] = a * acc_ref[qrows, :] + b * num
                    l_ref[qrows, :] = a * l_ref[qrows, :] + b * l_t
                    m_ref[qrows, :] = m_n
            return carry

        lax.fori_loop(0, ATT_SB // A_REACH // ATT_UNROLL, tiles, 0)
        for buf in (kb, vb):
            if by_residue:
                for r in range(d):
                    base = r * ATT_RESIDUE_PITCH
                    buf[base:base + A_REACH, :] = buf[base + A_REACH:base + 2 * A_REACH, :]
            else:
                buf[0:win, :] = buf[ATT_SB:ATT_SB + win, :]
        if gi == 1:
            shifter.turn_reads_into_writes()

    o_ref[...] = (acc_ref[...] / l_ref[...]).astype(BF16)
    shifter.end_of_step()


def _attn_prompt(u, new3, cache3):
    nh = A_HEADS
    assert (SEQ // ATT_SB) * nh == DEC_BATCH and SHIFT_SLOTS * SHIFT_CHUNK == A_WINDOWS[2] - DEC_SEQ
    in_specs, scratch = [], []
    for g in range(3):
        in_specs.append(pl.BlockSpec((ATT_SB, LANES), lambda h, sb, g=g: (sb, g * nh + h)))
    for g, d in enumerate(A_DILATIONS):
        win = A_REACH * d
        for part in (3, 6):
            cb = part * nh + g * nh
            in_specs.append(pl.BlockSpec((ATT_SB, LANES), lambda h, sb, cb=cb: (sb, cb + h)))
        buf_rows = d * ATT_RESIDUE_PITCH if d == ATT_RESIDUE_MAJOR_D else win + ATT_SB
        scratch += [pltpu.VMEM((buf_rows, LANES), F32), pltpu.VMEM((buf_rows, LANES), F32)]
    scratch += [pltpu.VMEM((ATT_SB, LANES), F32)] * 3
    scratch += [pltpu.VMEM((SHIFT_SLOTS, SHIFT_CHUNK + 1, 2, A_HEADS, A_HEAD_DIM), F32),
                pltpu.SemaphoreType.DMA((SHIFT_SLOTS,)), pltpu.SemaphoreType.DMA((SHIFT_SLOTS,)),
                pltpu.SemaphoreType.DMA((1,))]
    in_specs += [pl.BlockSpec(new3.shape, lambda h, sb: (0, 0, 0, 0, 0)), pl.BlockSpec(memory_space=pl.ANY)]
    return pl.pallas_call(
        _attn_prompt_kernel,
        grid=(nh, SEQ // ATT_SB),
        in_specs=in_specs,
        out_specs=[pl.BlockSpec((ATT_SB, LANES), lambda h, sb: (sb, h)), pl.BlockSpec(memory_space=pl.ANY)],
        out_shape=[jax.ShapeDtypeStruct((SEQ, A_WIDTH), BF16), jax.ShapeDtypeStruct(cache3.shape, cache3.dtype)],
        scratch_shapes=scratch,
        compiler_params=_cp(("arbitrary", "arbitrary"), vmem=ATT_VMEM_LIMIT),
        name="attn_prompt",
    )(*([u] * 9), new3, cache3)


def _attn_sample_kernel(qkv_ref, c1_ref, c2_ref, c3_ref, o_ref):
    scale = A_HEAD_DIM ** -0.5
    pos = lax.broadcasted_iota(jnp.int32, (A_REACH, 1, 1), 0)

    def merge(state, m_t, l_t, num):
        if state is None:
            return m_t, l_t, num
        m_o, l_o, n_o = state
        m_n = jnp.maximum(m_o, m_t)
        a, b = jnp.exp(m_o - m_n), jnp.exp(m_t - m_n)
        return m_n, a * l_o + b * l_t, a * n_o + b * num

    for s in range(DEC_SEQ):
        state = None
        for g in range(3):
            q = qkv_ref[0, s, g]
            if g == 0:
                kt, vt = c1_ref[0, :, 0], c1_ref[0, :, 1]
                valid = pos >= s
                new_rows = range(s + 1)
            else:
                c_ref = c2_ref if g == 1 else c3_ref
                kt, vt = c_ref[0, :, 2 * s], c_ref[0, :, 2 * s + 1]
                valid = None
                new_rows = (s,)
            sc = jnp.sum(kt * q[None], axis=-1, keepdims=True) * scale
            if valid is not None:
                sc = jnp.where(valid, sc, -jnp.inf)
            s_new = [jnp.sum(qkv_ref[0, t, 3 + g] * q, axis=-1, keepdims=True) * scale for t in new_rows]
            m_t = jnp.max(sc, axis=0)
            for sn in s_new:
                m_t = jnp.maximum(m_t, sn)
            p = jnp.exp(sc - m_t[None])
            l_t = jnp.sum(p, axis=0)
            num = jnp.sum(p * vt, axis=0)
            for t, sn in zip(new_rows, s_new):
                pn = jnp.exp(sn - m_t)
                l_t = l_t + pn
                num = num + pn * qkv_ref[0, t, 6 + g]
            state = merge(state, m_t, l_t, num)
        o_ref[0, s] = state[2] / state[1]


def _attn_sample(qkv_s, c1, c2, c3):
    tile = (A_HEADS, A_HEAD_DIM)
    return pl.pallas_call(
        _attn_sample_kernel,
        grid=(DEC_BATCH,),
        in_specs=[
            pl.BlockSpec((1, DEC_SEQ, 9) + tile, lambda b: (b, 0, 0, 0, 0)),
            pl.BlockSpec((1, A_REACH, 2) + tile, lambda b: (b, 0, 0, 0, 0)),
            pl.BlockSpec((1, A_REACH, 2 * DEC_SEQ) + tile, lambda b: (b, 0, 0, 0, 0)),
            pl.BlockSpec((1, A_REACH, 2 * DEC_SEQ) + tile, lambda b: (b, 0, 0, 0, 0)),
        ],
        out_specs=pl.BlockSpec((1, DEC_SEQ) + tile, lambda b: (b, 0, 0, 0)),
        out_shape=jax.ShapeDtypeStruct((DEC_BATCH, DEC_SEQ) + tile, F32),
        compiler_params=_cp(("arbitrary",)),
        name="attn_sample",
    )(qkv_s, c1, c2, c3)


def _load_resident(step, pairs):
    @pl.when(step == 0)
    def _():
        for src, dst in pairs:
            pltpu.sync_copy(src, dst)


def _merge_ln1_kernel(oa_ref, og_ref, gt_ref, xn_ref, g_ref, b_ref, wa_hbm, wb_hbm, wo_hbm,
                      x1_ref, wa, wb, wo, mixed, resid):
    _load_resident(pl.program_id(0), ((wa_hbm, wa), (wb_hbm, wb), (wo_hbm, wo)))
    nj = D_MODEL // COL_TILE
    oa, og = oa_ref[...], og_ref[...]
    for t in range(nj):
        cs = slice(t * COL_TILE, (t + 1) * COL_TILE)
        gs = slice(D_MODEL + t * COL_TILE, D_MODEL + (t + 1) * COL_TILE)
        ya = _dot(oa, wa[:, cs])
        yb = _dot(og, wb[:, cs])
        ga, gb = gt_ref[:, cs].astype(F32), gt_ref[:, gs].astype(F32)
        mixed[:, cs] = (_sigmoid(ga) * ya + _sigmoid(gb) * yb).astype(BF16)
    mx = mixed[...]
    for t in range(nj):
        cs = slice(t * COL_TILE, (t + 1) * COL_TILE)
        resid[:, cs] = ALPHA * xn_ref[:, cs] + _dot(mx, wo[:, cs])
    x1_ref[...] = _layer_norm_rows(resid[...], g_ref[...], b_ref[...])


def _merge_ln1(o_a, o_g, gates, xn, g, b, wa, wb, wo):
    tm = MERGE_ROWS
    any_spec = pl.BlockSpec(memory_space=pl.ANY)
    return pl.pallas_call(
        _merge_ln1_kernel,
        grid=(M_ROWS // tm,),
        in_specs=[
            pl.BlockSpec((tm, A_WIDTH), lambda i: (i, 0)),
            pl.BlockSpec((tm, GLA_V), lambda i: (i, 0)),
            pl.BlockSpec((tm, 2 * D_MODEL), lambda i: (i, 0)),
            pl.BlockSpec((tm, D_MODEL), lambda i: (i, 0)),
            pl.BlockSpec((1, D_MODEL), lambda i: (0, 0)),
            pl.BlockSpec((1, D_MODEL), lambda i: (0, 0)),
            any_spec, any_spec, any_spec,
        ],
        out_specs=pl.BlockSpec((tm, D_MODEL), lambda i: (i, 0)),
        out_shape=jax.ShapeDtypeStruct((M_ROWS, D_MODEL), F32),
        scratch_shapes=[pltpu.VMEM((A_WIDTH, D_MODEL), BF16), pltpu.VMEM((GLA_V, D_MODEL), BF16),
                        pltpu.VMEM((D_MODEL, D_MODEL), BF16),
                        pltpu.VMEM((tm, D_MODEL), BF16), pltpu.VMEM((tm, D_MODEL), F32)],
        compiler_params=_cp(("arbitrary",)),
        name="merge_ln1",
    )(o_a, o_g, gates, xn, g.reshape(1, D_MODEL), b.reshape(1, D_MODEL), wa, wb, wo)


def _route_block(x, w, idx_ref, wt_ref, cnt_ref):
    tm = x.shape[0]
    xh, xm, _ = _split3(x)
    wh, wm, _ = _split3(w)
    logits = _dot(xh, wh) + _dot(xh, wm) + _dot(xm, wh)
    lane_i = lax.broadcasted_iota(jnp.int32, (tm, LANES), 1)
    lane = lane_i.astype(F32)
    lane_group = (lane_i >> 3).astype(F32)
    neg = -jnp.inf

    def top1(v):
        mx = jnp.max(v, axis=1, keepdims=True)
        ix = jnp.min(jnp.where(v == mx, lane, float(LANES)), axis=1, keepdims=True)
        return mx, ix

    gl = jnp.where(lane_i < MOE_GROUPS, logits[:, :LANES], neg)
    g_max, g_idx = top1(gl)
    g_p = 1.0 / jnp.sum(jnp.exp(gl - g_max), axis=1, keepdims=True)
    el = logits[:, LANES:]
    in_group = jnp.logical_and(lane_i < N_EXPERTS, lane_group == g_idx)
    e1v = jnp.where(in_group, el, neg)
    m1, i1 = top1(e1v)
    m2, i2 = top1(jnp.where(lane == i1, neg, e1v))
    t = jnp.exp(m2 - m1)
    w1 = g_p / (1.0 + t)
    w2 = g_p * t / (1.0 + t)
    wt_ref[...] = jnp.where(lane_i == 0, w1, jnp.where(lane_i == 1, w2, 0.0))

    @pl.when(pl.program_id(0) == 0)
    def _():
        cnt_ref[...] = jnp.zeros_like(cnt_ref)

    is1, is2 = lane == i1, lane == i2
    onehot = jnp.where(jnp.logical_or(is1, is2), 1.0, 0.0)
    r = lax.broadcasted_iota(jnp.int32, (tm, tm), 0)
    c = lax.broadcasted_iota(jnp.int32, (tm, tm), 1)
    before = _dot(jnp.where(r > c, 1.0, 0.0).astype(BF16), onehot.astype(BF16)) + cnt_ref[...]
    rank1 = jnp.sum(jnp.where(is1, before, 0.0), axis=1, keepdims=True)
    rank2 = jnp.sum(jnp.where(is2, before, 0.0), axis=1, keepdims=True)
    cnt_ref[...] += jnp.sum(onehot, axis=0, keepdims=True)
    idx_ref[...] = jnp.where(lane_i == 0, i1, jnp.where(lane_i == 1, i2, jnp.where(
        lane_i == 2, rank1, jnp.where(lane_i == 3, rank2, 0.0)))).astype(jnp.int32)


def _ple_router_kernel(x_ref, p_ref, wr_ref, n2_ref, wg_hbm, wp_hbm, c2_hbm,
                       o_ref, idx_ref, wt_ref, cnt_ref, kv2_hbm, wg, wp, cbuf, rsem, wsem, nsem):
    shifter = _whole_row_shifter(pl.program_id(0), M_ROWS // MERGE_ROWS, A_WINDOWS[1], n2_ref, c2_hbm, kv2_hbm,
                                 cbuf, rsem, wsem, nsem)
    shifter.top_of_step()
    _load_resident(pl.program_id(0), ((wg_hbm, wg), (wp_hbm, wp)))
    x = x_ref[...]
    xb = x.astype(BF16)
    p = p_ref[...]
    for t in range(D_MODEL // COL_TILE):
        cs = slice(t * COL_TILE, (t + 1) * COL_TILE)
        gate = _sigmoid(_dot(xb, wg[:, cs]))
        o_ref[:, cs] = ALPHA * x_ref[:, cs] + gate * _dot(p, wp[:, cs])
    shifter.turn_reads_into_writes()
    _route_block(x, wr_ref[...], idx_ref, wt_ref, cnt_ref)
    shifter.end_of_step()


def _ple_router(x1, p, wpg, wple, w_r1, w_r2, new2, cache2):
    wr = jnp.zeros((D_MODEL, 2 * LANES), F32)
    wr = wr.at[:, :MOE_GROUPS].set(w_r1).at[:, LANES:LANES + N_EXPERTS].set(w_r2)
    tm = MERGE_ROWS
    assert M_ROWS // tm >= DEC_BATCH // SMALL_SHIFT_ROWS
    any_spec = pl.BlockSpec(memory_space=pl.ANY)
    return pl.pallas_call(
        _ple_router_kernel,
        grid=(M_ROWS // tm,),
        in_specs=[
            pl.BlockSpec((tm, D_MODEL), lambda i: (i, 0)),
            pl.BlockSpec((tm, PLE_DIM), lambda i: (i, 0)),
            pl.BlockSpec((D_MODEL, 2 * LANES), lambda i: (0, 0)),
            pl.BlockSpec(new2.shape, lambda i: (0, 0, 0, 0, 0)),
            any_spec, any_spec, any_spec,
        ],
        out_specs=[pl.BlockSpec((tm, D_MODEL), lambda i: (i, 0)),
                   pl.BlockSpec((tm, LANES), lambda i: (i, 0)),
                   pl.BlockSpec((tm, LANES), lambda i: (i, 0)),
                   pl.BlockSpec((1, LANES), lambda i: (0, 0)),
                   any_spec],
        out_shape=[jax.ShapeDtypeStruct((M_ROWS, D_MODEL), F32),
                   jax.ShapeDtypeStruct((M_ROWS, LANES), jnp.int32),
                   jax.ShapeDtypeStruct((M_ROWS, LANES), F32),
                   jax.ShapeDtypeStruct((1, LANES), F32),
                   jax.ShapeDtypeStruct(cache2.shape, cache2.dtype)],
        scratch_shapes=[pltpu.VMEM((D_MODEL, D_MODEL), BF16), pltpu.VMEM((PLE_DIM, D_MODEL), BF16)]
        + _whole_row_shift_scratch(A_WINDOWS[1]),
        compiler_params=_cp(("arbitrary",)),
        name="ple_router",
    )(x1, p, wr, new2, wpg, wple, cache2)


WEIGHT_DMA_PRIORITY = 1


def _moe_kernel(be_ref, nx_ref, nv_ref, tok_ref, x_hbm, wg_hbm, wu_hbm, wd_hbm, y_ref,
                sg, su, sd, wgb, wub, wdb, xbuf, wsem, xsem):
    i = pl.program_id(0)
    nv = nv_ref[0]
    e = be_ref[i]

    def weight_copies(ex):
        return (pltpu.make_async_copy(wg_hbm.at[ex], sg, wsem.at[0]),
                pltpu.make_async_copy(wu_hbm.at[ex], su, wsem.at[1]),
                pltpu.make_async_copy(wd_hbm.at[ex], sd, wsem.at[2]))

    def row_copy(blk, slot, r):
        src = tok_ref[blk * MOE_BLK + r]
        return pltpu.make_async_copy(x_hbm.at[pl.ds(src, 1)], xbuf.at[slot, pl.ds(r, 1)], xsem.at[slot])

    def for_rows(blk, slot, action):
        def body(r, carry):
            action(row_copy(blk, slot, r))
            return carry
        lax.fori_loop(0, MOE_BLK, body, 0, unroll=8)

    @pl.when(i == 0)
    def _():
        for cp in weight_copies(e):
            cp.start(priority=WEIGHT_DMA_PRIORITY)
        for_rows(0, 0, lambda cp: cp.start())

    @pl.when(i + 1 < nv)
    def _():
        for_rows(i + 1, (i + 1) & 1, lambda cp: cp.start())

    first_of_expert = jnp.logical_or(i == 0, e != be_ref[jnp.maximum(i - 1, 0)])

    @pl.when(jnp.logical_and(i < nv, first_of_expert))
    def _():
        for cp in weight_copies(e):
            cp.wait()
        for r in range(0, D_MODEL, MOE_BLK):
            wgb[r:r + MOE_BLK, :] = sg[r:r + MOE_BLK, :].astype(BF16)
            wub[r:r + MOE_BLK, :] = su[r:r + MOE_BLK, :].astype(BF16)
        for r in range(0, D_EXPERT, MOE_BLK):
            wdb[r:r + MOE_BLK, :] = sd[r:r + MOE_BLK, :].astype(BF16)

        @pl.when(nx_ref[i] >= 0)
        def _():
            for cp in weight_copies(nx_ref[i]):
                cp.start(priority=WEIGHT_DMA_PRIORITY)

    @pl.when(i < nv)
    def _():
        slot = i & 1
        for_rows(i, slot, lambda cp: cp.wait())
        x = xbuf[slot].astype(BF16)
        a = _dot(x, wgb[...])
        h = (a * _sigmoid(a) * _dot(x, wub[...])).astype(BF16)
        y_ref[...] = _dot(h, wdb[...])

    @pl.when(i >= nv)
    def _():
        y_ref[...] = jnp.zeros_like(y_ref)


def _moe_experts(x1, row_tok, blk_e, blk_next, n_valid, w_gate, w_up, w_down):
    any_spec = pl.BlockSpec(memory_space=pl.ANY)
    return pl.pallas_call(
        _moe_kernel,
        grid_spec=pltpu.PrefetchScalarGridSpec(
            num_scalar_prefetch=4,
            grid=(MOE_NBLK,),
            in_specs=[any_spec, any_spec, any_spec, any_spec],
            out_specs=pl.BlockSpec((MOE_BLK, D_MODEL), lambda i, be, nx, nv, tok: (i, 0)),
            scratch_shapes=[
                pltpu.VMEM((D_MODEL, D_EXPERT), F32), pltpu.VMEM((D_MODEL, D_EXPERT), F32),
                pltpu.VMEM((D_EXPERT, D_MODEL), F32),
                pltpu.VMEM((D_MODEL, D_EXPERT), BF16), pltpu.VMEM((D_MODEL, D_EXPERT), BF16),
                pltpu.VMEM((D_EXPERT, D_MODEL), BF16),
                pltpu.VMEM((2, MOE_BLK, D_MODEL), F32),
                pltpu.SemaphoreType.DMA((3,)), pltpu.SemaphoreType.DMA((2,)),
            ],
        ),
        out_shape=jax.ShapeDtypeStruct((MOE_ROWS, D_MODEL), F32),
        compiler_params=_cp(("arbitrary",)),
        name="moe_experts",
    )(blk_e, blk_next, n_valid, row_tok, x1, w_gate, w_up, w_down)


def _moe_plan(e_idx, rank, counts):
    ids = jnp.arange(N_EXPERTS, dtype=jnp.int32)
    pcounts = (counts + MOE_BLK - 1) // MOE_BLK * MOE_BLK
    pends = jnp.cumsum(pcounts)
    pstarts = pends - pcounts
    dest = jnp.sum(jnp.where(e_idx[..., None] == ids, pstarts, 0), axis=-1) + rank
    tok = jnp.broadcast_to(jnp.arange(M_ROWS, dtype=jnp.int32)[:, None], dest.shape)
    row_tok = jnp.zeros((MOE_ROWS,), jnp.int32).at[dest.reshape(-1)].set(tok.reshape(-1))
    n_valid = (pends[-1] // MOE_BLK).astype(jnp.int32)
    blk_start = jnp.minimum(jnp.arange(MOE_NBLK, dtype=jnp.int32) * MOE_BLK, pends[-1] - 1)
    blk_e = jnp.sum((pends[None, :] <= blk_start[:, None]).astype(jnp.int32), axis=1)
    blk_e = jnp.clip(blk_e, 0, N_EXPERTS - 1).astype(jnp.int32)
    later = jnp.where((counts[None, :] > 0) & (ids[None, :] > ids[:, None]), ids[None, :], N_EXPERTS)
    next_e = jnp.min(later, axis=1)
    next_e = jnp.where(next_e == N_EXPERTS, -1, next_e).astype(jnp.int32)
    blk_next = jnp.sum(jnp.where(blk_e[:, None] == ids, next_e, 0), axis=1).astype(jnp.int32)
    return dest, row_tok, blk_e, blk_next, n_valid.reshape(1)


def _final_row_copy(dest_ref, y_hbm, ybuf, sem, blk, slot, k, t):
    src = dest_ref[k * M_ROWS + blk * LN_ROWS + t]
    return pltpu.make_async_copy(y_hbm.at[pl.ds(src, 1)], ybuf.at[slot, pl.ds(k * LN_ROWS + t, 1)], sem.at[slot])


def _final_kernel(dest_ref, base_ref, wt_ref, g_ref, b_ref, y_hbm, op_ref, os_ref, ybuf, sem):
    i = pl.program_id(0)
    n_rows = 2 * LN_ROWS

    def for_rows(blk, slot, action):
        for k in range(2):
            for t in range(LN_ROWS):
                action(_final_row_copy(dest_ref, y_hbm, ybuf, sem, blk, slot, k, t))

    @pl.when(i == 0)
    def _():
        for k in range(2):
            def body(t, carry, k=k):
                _final_row_copy(dest_ref, y_hbm, ybuf, sem, 0, 0, k, t).start()
                return carry
            lax.fori_loop(0, LN_ROWS, body, 0, unroll=8)

    @pl.when(i + 1 < pl.num_programs(0))
    def _():
        for_rows(i + 1, (i + 1) & 1, lambda cp: cp.start())

    slot = i & 1
    for_rows(i, slot, lambda cp: cp.wait())

    wt = wt_ref[...]
    r = base_ref[...] + wt[:, 0:1] * ybuf[slot, 0:LN_ROWS, :] + wt[:, 1:2] * ybuf[slot, LN_ROWS:n_rows, :]
    y = _layer_norm_rows(r, g_ref[...], b_ref[...])

    @pl.when(i < SEQ // LN_ROWS)
    def _():
        op_ref[...] = y

    @pl.when(i >= SEQ // LN_ROWS)
    def _():
        os_ref[...] = y


def _final(dest_flat, base, y_rows, wt, g, b):
    npb = SEQ // LN_ROWS
    return pl.pallas_call(
        _final_kernel,
        grid_spec=pltpu.PrefetchScalarGridSpec(
            num_scalar_prefetch=1,
            grid=(npb + 1,),
            in_specs=[
                pl.BlockSpec((LN_ROWS, D_MODEL), lambda i, d: (i, 0)),
                pl.BlockSpec((LN_ROWS, LANES), lambda i, d: (i, 0)),
                pl.BlockSpec((1, D_MODEL), lambda i, d: (0, 0)),
                pl.BlockSpec((1, D_MODEL), lambda i, d: (0, 0)),
                pl.BlockSpec(memory_space=pl.ANY),
            ],
            out_specs=[pl.BlockSpec((LN_ROWS, D_MODEL), lambda i, d: (jnp.minimum(i, npb - 1), 0)),
                       pl.BlockSpec((N_SAMPLE, D_MODEL), lambda i, d: (0, 0))],
            scratch_shapes=[pltpu.VMEM((2, 2 * LN_ROWS, D_MODEL), F32), pltpu.SemaphoreType.DMA((2,))],
        ),
        out_shape=[jax.ShapeDtypeStruct((SEQ, D_MODEL), F32),
                   jax.ShapeDtypeStruct((N_SAMPLE, D_MODEL), F32)],
        compiler_params=_cp(("arbitrary",)),
        name="combine_ln2",
    )(dest_flat, base, wt, g.reshape(1, D_MODEL), b.reshape(1, D_MODEL), y_rows)


def kernel(x_prompt, x_sample, p_prompt, p_sample, cache_kv_a1, cache_kv_a2, cache_kv_a3, state_gla,
           ln_emb_g, ln_emb_b, ln1_g, ln1_b, ln2_g, ln2_b, w_in, w_gk2, b_gk, gla_norm_g, w_br_a, w_br_b,
           w_out, w_router_group, w_router_expert, w_gate, w_up, w_down, w_ple_gate, w_ple):
    xp = x_prompt.reshape(SEQ, D_MODEL)
    xs = x_sample.reshape(N_SAMPLE, D_MODEL)
    xn, xb = _ln_emb(xp, xs, ln_emb_g, ln_emb_b)

    w_t = jnp.transpose(w_in[0])
    rope = _rope_tables()
    u = _project(xb, w_t, 0, A_QKV_COLS, 2 * 3 * A_WIDTH // COL_TILE, rope, F32)
    u_gla = _project(xb, w_t, A_QKV_COLS, MAIN_COLS - A_QKV_COLS, 0, rope, BF16)
    gates = _project(xb, w_t, MAIN_COLS + GLA_LR, 2 * D_MODEL, 0, rope, BF16)
    gk = _gla_gate(xb, w_t[MAIN_COLS:MAIN_COLS + GLA_LR], w_gk2[0], b_gk[0])

    qkv_s = u[SEQ:].reshape(DEC_BATCH, DEC_SEQ, 9, A_HEADS, A_HEAD_DIM)
    kv_new = [jnp.stack([qkv_s[:, :, 3 + g], qkv_s[:, :, 6 + g]], axis=2) for g in range(3)]
    o_a, kv_s3 = _attn_prompt(u, kv_new[2], cache_kv_a3)
    c1 = cache_kv_a1.reshape(DEC_BATCH, A_REACH, 2, A_HEADS, A_HEAD_DIM)
    c2 = cache_kv_a2.reshape(DEC_BATCH, A_REACH, 4 * 2, A_HEADS, A_HEAD_DIM)
    c3 = cache_kv_a3.reshape(DEC_BATCH, A_REACH, 16 * 2, A_HEADS, A_HEAD_DIM)
    o_as = _attn_sample(qkv_s, c1, c2, c3)
    o_a = jnp.concatenate([o_a, o_as.reshape(N_SAMPLE, A_WIDTH).astype(BF16)], axis=0)

    o_g, st_t, kv_s1 = _gla_prompt(u_gla, gk, gla_norm_g[0], kv_new[0], cache_kv_a1)

    def spad(a):
        a = a.astype(F32).reshape(DEC_BATCH, DEC_SEQ, a.shape[-1])
        return jnp.pad(a, ((0, 0), (0, GLA_SPAD - DEC_SEQ), (0, 0)))

    ug_s = u_gla[SEQ:]
    o_gs, st_s = _gla_sample(
        spad(ug_s[:, :GLA_QK]), spad(ug_s[:, GLA_QK:2 * GLA_QK]),
        spad(ug_s[:, 2 * GLA_QK:2 * GLA_QK + GLA_V]), spad(ug_s[:, 2 * GLA_QK + GLA_V:]),
        spad(gk[SEQ:]), gla_norm_g[0], state_gla[0])
    o_gs = o_gs[:, :DEC_SEQ].reshape(N_SAMPLE, GLA_V).astype(BF16)
    o_g = jnp.concatenate([o_g, o_gs], axis=0)

    x1 = _merge_ln1(o_a, o_g, gates, xn, ln1_g[0], ln1_b[0],
                    w_br_a[0].astype(BF16), w_br_b[0].astype(BF16), w_out[0].astype(BF16))

    p = jnp.concatenate([p_prompt[0].reshape(SEQ, PLE_DIM), p_sample[0].reshape(N_SAMPLE, PLE_DIM)], axis=0)
    base, ridx, rwt, rcnt, kv_s2 = _ple_router(
        x1, p.astype(BF16), w_ple_gate[0].astype(BF16), w_ple[0].astype(BF16),
        w_router_group[0], w_router_expert[0], kv_new[1], cache_kv_a2)
    dest, row_tok, blk_e, blk_next, n_valid = _moe_plan(
        ridx[:, :2], ridx[:, 2:4], rcnt[0, :N_EXPERTS].astype(jnp.int32))
    y_rows =_moe_experts(x1, row_tok, blk_e, blk_next, n_valid, w_gate[0], w_up[0], w_down[0])
    y_p, y_s = _final(dest.T.reshape(-1), base, y_rows, rwt, ln2_g[0], ln2_b[0])

    def kv_prompt(g):
        w = A_WINDOWS[g]
        k = u[SEQ - w:SEQ, (3 + g) * A_WIDTH:(4 + g) * A_WIDTH].reshape(w, A_HEADS, A_HEAD_DIM)
        v = u[SEQ - w:SEQ, (6 + g) * A_WIDTH:(7 + g) * A_WIDTH].reshape(w, A_HEADS, A_HEAD_DIM)
        return jnp.stack([k, v], axis=1)[None, None]


    gla_state_prompt = st_t[None, None]
    return (y_p.reshape(1, SEQ, D_MODEL), y_s.reshape(DEC_BATCH, DEC_SEQ, D_MODEL),
            kv_prompt(0), kv_prompt(1), kv_prompt(2), gla_state_prompt,
            kv_s1, kv_s2, kv_s3, st_s[None])
```
